```python
import math
import jax, jax.numpy as jnp
from jax import lax
import numpy as np

D_MODEL = 1024
BATCH = 4
SEQ = 8192
DEPTH = 2

N_MIXERS = 2
DIFF_HEAD_DIM = 64
DIFF_HEADS = D_MODEL // (2 * DIFF_HEAD_DIM)
SB_HEAD_DIM = 64
SB_HEADS = D_MODEL // SB_HEAD_DIM
PROJ_WIDTH = 3 * D_MODEL
ROPE_THETA = 500000.0
ROPE_DIMS = DIFF_HEAD_DIM // 4
Q_BLOCK = 128
N_EXPERTS = 32
N_GROUPS = 4
EXPERTS_PER_GROUP = N_EXPERTS // N_GROUPS
TOP_K = 2
D_EXPERT = D_MODEL // 2
EXPERT_BLOCK = 256
NORM_EPS = 1e-6
N_DIFF_LAYERS = (DEPTH + 1) // 2

kernel_name = "hybrid_diffattn_stickbreak_grouped_moe"


def rms_norm(x, g):
    x32 = x.astype(jnp.float32)
    y = x32 * lax.rsqrt(jnp.mean(x32 * x32, axis=-1, keepdims=True) + NORM_EPS)
    return (y * g.astype(jnp.float32)).astype(x.dtype)


def partial_rope(x, positions):
    half = ROPE_DIMS // 2
    inv_freq = jnp.power(jnp.float32(ROPE_THETA), -jnp.arange(half, dtype=jnp.float32) * (2.0 / ROPE_DIMS))
    ang = positions.astype(jnp.float32)[..., None] * inv_freq
    ang = ang.reshape(ang.shape[:2] + (1,) * (x.ndim - 3) + (half,))
    cos, sin = jnp.cos(ang), jnp.sin(ang)
    xr = x[..., :ROPE_DIMS].astype(jnp.float32)
    x1, x2 = xr[..., :half], xr[..., half:]
    rot = jnp.concatenate([x1 * cos - x2 * sin, x2 * cos + x1 * sin], axis=-1)
    return jnp.concatenate([rot.astype(x.dtype), x[..., ROPE_DIMS:]], axis=-1)


def diff_attention(h, positions, w_in, w_out, lam, subln_g, layer_idx):
    B, S, _ = h.shape
    H, DH = DIFF_HEADS, DIFF_HEAD_DIM
    proj = h @ w_in
    q = proj[..., :D_MODEL].reshape(B, S, H, 2, DH)
    k = proj[..., D_MODEL:2 * D_MODEL].reshape(B, S, H, 2, DH)
    v = proj[..., 2 * D_MODEL:].reshape(B, S, H, 2 * DH).astype(jnp.float32)
    q = partial_rope(q, positions).astype(jnp.float32) * (DH ** -0.5)
    k = partial_rope(k, positions).astype(jnp.float32)
    lam_init = 0.8 - 0.6 * math.exp(-0.3 * layer_idx)
    lam32 = lam.astype(jnp.float32)
    lam_full = jnp.exp(jnp.sum(lam32[0] * lam32[1])) - jnp.exp(jnp.sum(lam32[2] * lam32[3])) + lam_init
    nb = S // Q_BLOCK
    qb = q.reshape(B, nb, Q_BLOCK, H, 2, DH).swapaxes(0, 1)
    key_pos = jnp.arange(S)

    def block(args):
        q_blk, i = args
        t = i * Q_BLOCK + jnp.arange(Q_BLOCK)
        causal = key_pos[None, :] <= t[:, None]
        s = jnp.einsum('bqhcd,bkhcd->bhcqk', q_blk, k)
        p = jax.nn.softmax(jnp.where(causal, s, -jnp.inf), axis=-1)
        a = p[:, :, 0] - lam_full * p[:, :, 1]
        return jnp.einsum('bhqk,bkhe->bqhe', a, v)

    o = lax.map(block, (qb, jnp.arange(nb)))
    o = o.swapaxes(0, 1).reshape(B, S, H, 2 * DH)
    o = rms_norm(o, subln_g) * (1.0 - lam_init)
    return o.reshape(B, S, D_MODEL).astype(h.dtype) @ w_out


def stick_breaking_attention(h, w_in, w_out):
    B, S, _ = h.shape
    H, DH = SB_HEADS, SB_HEAD_DIM
    proj = h @ w_in
    q = proj[..., :D_MODEL].reshape(B, S, H, DH).astype(jnp.float32) * (DH ** -0.5)
    k = proj[..., D_MODEL:2 * D_MODEL].reshape(B, S, H, DH).astype(jnp.float32)
    v = proj[..., 2 * D_MODEL:].reshape(B, S, H, DH).astype(jnp.float32)
    nb = S // Q_BLOCK
    qb = q.reshape(B, nb, Q_BLOCK, H, DH).swapaxes(0, 1)
    key_pos = jnp.arange(S)

    def block(args):
        q_blk, i = args
        t = i * Q_BLOCK + jnp.arange(Q_BLOCK)
        strict = key_pos[None, :] < t[:, None]
        z = jnp.einsum('bqhd,bkhd->bhqk', q_blk, k)
        log_beta = jax.nn.log_sigmoid(z)
        log_rest = jnp.where(strict, log_beta - z, 0.0)
        after = lax.cumsum(log_rest, axis=3, reverse=True) - log_rest
        a = jnp.where(strict, jnp.exp(log_beta + after), 0.0)
        return jnp.einsum('bhqk,bkhd->bqhd', a, v)

    o = lax.map(block, (qb, jnp.arange(nb)))
    o = o.swapaxes(0, 1).reshape(B, S, D_MODEL)
    return o.astype(h.dtype) @ w_out


def grouped_moe(h, w_router, b_router, w_gate, w_up, w_down):
    B, S, D = h.shape
    N = B * S
    E, K, EB = N_EXPERTS, TOP_K, EXPERT_BLOCK
    xf = h.reshape(N, D)
    scores = jax.nn.softmax((xf @ w_router).astype(jnp.float32), axis=-1)
    sel = scores + b_router.astype(jnp.float32)
    sel_g = sel.reshape(N, N_GROUPS, EXPERTS_PER_GROUP)
    group_score = lax.top_k(sel_g, K)[0].sum(-1)
    g_best = jnp.argmax(group_score, axis=-1)
    sel_in = jnp.take_along_axis(sel_g, g_best[:, None, None], axis=1)[:, 0]
    _, local = lax.top_k(sel_in, K)
    expert_idx = (g_best[:, None] * EXPERTS_PER_GROUP + local).astype(jnp.int32)
    gate = jnp.take_along_axis(scores, expert_idx, axis=1)
    gate = gate / jnp.sum(gate, axis=-1, keepdims=True)

    flat_e = expert_idx.reshape(-1)
    flat_w = gate.reshape(-1)
    flat_tok = jnp.arange(N * K, dtype=jnp.int32) // K
    order = jnp.argsort(flat_e)
    sorted_e = flat_e[order]
    counts = jnp.zeros((E,), jnp.int32).at[flat_e].add(1)
    padded = ((counts + EB - 1) // EB) * EB
    offs = jnp.cumsum(counts) - counts
    poffs_end = jnp.cumsum(padded)
    poffs = poffs_end - padded
    dest = poffs[sorted_e] + (jnp.arange(N * K, dtype=jnp.int32) - offs[sorted_e])
    P = N * K + E * EB
    n_blk = P // EB
    row_tok = jnp.full((P,), N, jnp.int32).at[dest].set(flat_tok[order])
    row_w = jnp.zeros((P,), jnp.float32).at[dest].set(flat_w[order])
    x_pad = jnp.concatenate([xf, jnp.zeros((1, D), xf.dtype)], axis=0)
    x_rows = x_pad[row_tok].reshape(n_blk, EB, D)
    blk_e = jnp.minimum(jnp.searchsorted(poffs_end, jnp.arange(n_blk) * EB, side='right'), E - 1)

    def expert_block(args):
        xb, e = args
        hid = jax.nn.silu(xb @ w_gate[e]) * (xb @ w_up[e])
        return hid @ w_down[e]

    y_rows = lax.map(expert_block, (x_rows, blk_e)).reshape(P, D)
    y = jax.ops.segment_sum(y_rows.astype(jnp.float32) * row_w[:, None], row_tok, num_segments=N + 1)[:N]
    return y.reshape(B, S, D).astype(h.dtype)


def setup_inputs(seed: int = 0) -> dict:
    key = jax.random.key(seed)
    ks = jax.random.split(key, 17)
    D, E, F = D_MODEL, N_EXPERTS, D_EXPERT
    nrm = jax.random.normal
    x = nrm(ks[0], (BATCH, SEQ, D), jnp.float32)
    c = nrm(ks[1], (BATCH, D), jnp.float32)
    offsets = jax.random.randint(ks[2], (BATCH, 1), 0, 4096, dtype=jnp.int32)
    positions = offsets + jnp.arange(SEQ, dtype=jnp.int32)[None, :]
    w_ada = nrm(ks[3], (DEPTH, D, 6 * D), jnp.float32) * (0.5 * D ** -0.5)
    b_ada = nrm(ks[4], (DEPTH, 6 * D), jnp.float32) * 0.02
    norm_g = 1.0 + 0.02 * nrm(ks[5], (DEPTH, 2, D), jnp.float32)
    w_in = nrm(ks[6], (DEPTH, D, PROJ_WIDTH), jnp.float32) * D ** -0.5
    w_out = nrm(ks[7], (DEPTH, D, D), jnp.float32) * D ** -0.5
    diff_lambda = nrm(ks[8], (N_DIFF_LAYERS, 4, DIFF_HEAD_DIM), jnp.float32) * 0.1
    diff_subln_g = 1.0 + 0.02 * nrm(ks[9], (N_DIFF_LAYERS, 2 * DIFF_HEAD_DIM), jnp.float32)
    w_router = nrm(ks[10], (D, E), jnp.float32) * D ** -0.5
    b_router = nrm(ks[11], (E,), jnp.float32) * 0.01
    w_gate = nrm(ks[12], (DEPTH, E, D, F), jnp.float32) * D ** -0.5
    w_up = nrm(ks[13], (DEPTH, E, D, F), jnp.float32) * D ** -0.5
    w_down = nrm(ks[14], (DEPTH, E, F, D), jnp.float32) * F ** -0.5
    final_norm_g = 1.0 + 0.02 * nrm(ks[15], (D,), jnp.float32)
    return {"x": x, "c": c, "positions": positions, "w_ada": w_ada, "b_ada": b_ada,
            "norm_g": norm_g, "w_in": w_in, "w_out": w_out, "diff_lambda": diff_lambda,
            "diff_subln_g": diff_subln_g, "w_router": w_router, "b_router": b_router,
            "w_gate": w_gate, "w_up": w_up, "w_down": w_down, "final_norm_g": final_norm_g}


def reference(x, c, positions, w_ada, b_ada, norm_g, w_in, w_out, diff_lambda, diff_subln_g,
              w_router, b_router, w_gate, w_up, w_down, final_norm_g):
    c_act = jax.nn.silu(c)
    for i in range(DEPTH):
        mod = c_act @ w_ada[i] + b_ada[i]
        sh_a, sc_a, g_a, sh_m, sc_m, g_m = jnp.split(mod, 6, axis=-1)
        h = rms_norm(x, norm_g[i, 0]) * (1.0 + sc_a[:, None]) + sh_a[:, None]
        if i % N_MIXERS == 0:
            mix = diff_attention(h, positions, w_in[i], w_out[i], diff_lambda[i // N_MIXERS],
                                 diff_subln_g[i // N_MIXERS], i)
        else:
            mix = stick_breaking_attention(h, w_in[i], w_out[i])
        x = x + g_a[:, None] * mix
        h = rms_norm(x, norm_g[i, 1]) * (1.0 + sc_m[:, None]) + sh_m[:, None]
        x = x + g_m[:, None] * grouped_moe(h, w_router, b_router, w_gate[i], w_up[i], w_down[i])
    return rms_norm(x, final_norm_g)
```

```python
import functools
import math

import jax
import jax.numpy as jnp
from jax import lax
from jax.experimental import pallas as pl
from jax.experimental.pallas import tpu as pltpu

F32 = jnp.float32
BF16 = jnp.bfloat16
I32 = jnp.int32

LANES = 128
HEAD_DIM = 64
ROPE_DIMS = HEAD_DIM // 4
ROPE_THETA = 500000.0
N_EXPERTS = 32
N_GROUPS = 4
GROUP_SIZE = N_EXPERTS // N_GROUPS
TOP_K = 2
NORM_EPS = 1e-6
EXPERT_ROWS = 256
SB_DEAD_LOG = -110.0
MASK_VALUE = -1e30
VMEM_LIMIT = 56 * 1024 * 1024


def _cparams(semantics, vmem=VMEM_LIMIT):
    return pltpu.CompilerParams(dimension_semantics=semantics, vmem_limit_bytes=vmem)


def _split_bf16(a):
    hi = a.astype(BF16)
    lo = (a - hi.astype(F32)).astype(BF16)
    return hi, lo


def _dot3(a, b):
    ah, al = _split_bf16(a)
    bh, bl = _split_bf16(b)
    d = functools.partial(jnp.dot, preferred_element_type=F32)
    return d(ah, bh) + d(ah, bl) + d(al, bh)


def _rms(x, g):
    ms = jnp.mean(x * x, axis=-1, keepdims=True)
    return x * lax.rsqrt(ms + NORM_EPS) * g


def _mod_kernel(c_ref, w_ref, b_ref, o_ref):
    c = c_ref[...]
    c_act = c / (1.0 + jnp.exp(-c))
    o_ref[...] = _dot3(c_act, w_ref[...]) + b_ref[...]


def _modulation(c, w_ada, b_ada):
    depth, d, six_d = w_ada.shape
    b = c.shape[0]
    rows = 16
    c_pad = jnp.zeros((rows, d), F32).at[:b].set(c)
    tn = 1536
    out = pl.pallas_call(
        _mod_kernel,
        out_shape=jax.ShapeDtypeStruct((depth, rows, six_d), F32),
        grid=(depth, six_d // tn),
        in_specs=[
            pl.BlockSpec((rows, d), lambda l, j: (0, 0)),
            pl.BlockSpec((None, d, tn), lambda l, j: (l, 0, j)),
            pl.BlockSpec((None, 1, tn), lambda l, j: (l, 0, j)),
        ],
        out_specs=pl.BlockSpec((None, rows, tn), lambda l, j: (l, 0, j)),
        compiler_params=_cparams(("arbitrary", "arbitrary")),
        name="adaln_mod",
    )(c_pad, w_ada, b_ada.reshape(depth, 1, six_d))
    return out[:, :b]


def _proj_kernel(*refs, d, rope):
    if rope:
        x_ref, g_ref, sc_ref, sh_ref, w_ref, pos_ref, pat_ref, o_ref = refs
    else:
        x_ref, g_ref, sc_ref, sh_ref, w_ref, o_ref = refs
    h = _rms(x_ref[...], g_ref[...]) * (1.0 + sc_ref[...]) + sh_ref[...]
    p = jnp.dot(h.astype(BF16), w_ref[...], preferred_element_type=F32)
    n_qk = 2 * d // LANES
    if rope:
        ang = pos_ref[...].astype(F32) * pat_ref[...]
        cs, sn = jnp.cos(ang), jnp.sin(ang)
        lane = lax.broadcasted_iota(I32, ang.shape, 1) & (HEAD_DIM - 1)
        half = ROPE_DIMS // 2
        s_lo = jnp.where(lane < half, -sn, 0.0)
        s_hi = jnp.where((lane >= half) & (lane < ROPE_DIMS), sn, 0.0)
    for j in range(n_qk):
        blk = p[:, j * LANES:(j + 1) * LANES]
        if rope:
            blk = (blk * cs + pltpu.roll(blk, half, 1) * s_hi
                   + pltpu.roll(blk, LANES - half, 1) * s_lo)
        if j < d // LANES:
            blk = blk * (HEAD_DIM ** -0.5)
        o_ref[:, j * LANES:(j + 1) * LANES] = blk.astype(BF16)
    o_ref[:, 2 * d:] = p[:, 2 * d:].astype(BF16)


def _norm_proj(x, g, sc, sh, w_bf16, positions, rope, seq):
    n, d = x.shape
    tm = min(512, seq)
    per_b = seq // tm
    row = lambda i: (i, 0)
    bat = lambda i: (i // per_b, 0, 0)
    in_specs = [
        pl.BlockSpec((tm, d), row),
        pl.BlockSpec((1, d), lambda i: (0, 0)),
        pl.BlockSpec((None, 1, d), bat),
        pl.BlockSpec((None, 1, d), bat),
        pl.BlockSpec((d, 3 * d), lambda i: (0, 0)),
    ]
    args = [x, g.reshape(1, d), sc, sh, w_bf16]
    if rope:
        half = ROPE_DIMS // 2
        inv_freq = jnp.power(jnp.float32(ROPE_THETA), -jnp.arange(half, dtype=F32) * (2.0 / ROPE_DIMS))
        lane = jnp.arange(LANES) % HEAD_DIM
        pat = jnp.where(lane < ROPE_DIMS, inv_freq[lane % half], 0.0).astype(F32).reshape(1, LANES)
        in_specs += [pl.BlockSpec((tm, 1), row), pl.BlockSpec((1, LANES), lambda i: (0, 0))]
        args += [positions.reshape(n, 1), pat]
    return pl.pallas_call(
        functools.partial(_proj_kernel, d=d, rope=rope),
        out_shape=jax.ShapeDtypeStruct((n, 3 * d), BF16),
        grid=(n // tm,),
        in_specs=in_specs,
        out_specs=pl.BlockSpec((tm, 3 * d), row),
        compiler_params=_cparams(("arbitrary",)),
        name="norm_proj_rope" if rope else "norm_proj",
    )(*args)


def _stack_halves(q):
    lane = lax.broadcasted_iota(I32, q.shape, 1)
    zero = jnp.zeros_like(q)
    return jnp.concatenate([jnp.where(lane < HEAD_DIM, q, zero), jnp.where(lane >= HEAD_DIM, q, zero)], axis=0)


def _qk(qs, kb):
    return lax.dot_general(qs, kb, (((1,), (1,)), ((), ())), preferred_element_type=F32)


def _block_positions(shape, t, i, j):
    r = lax.broadcasted_iota(I32, shape, 0)
    qpos = i * t + jnp.where(r >= t, r - t, r)
    kpos = j * t + lax.broadcasted_iota(I32, shape, 1)
    return qpos, kpos


def _diff_attn_kernel(lam_ref, g_ref, q_ref, k_ref, v_ref, o_ref, m_ref, l_ref, acc_ref, *, t, lam_init):
    i = pl.program_id(2)
    qs = _stack_halves(q_ref[...])
    m_ref[...] = jnp.full(m_ref.shape, MASK_VALUE, F32)
    l_ref[...] = jnp.zeros(l_ref.shape, F32)
    acc_ref[...] = jnp.zeros(acc_ref.shape, F32)

    def block(j, masked):
        start = pl.multiple_of(j * t, t)
        s = _qk(qs, k_ref[pl.ds(start, t), :])
        if masked:
            qpos, kpos = _block_positions(s.shape, t, i, j)
            s = jnp.where(kpos <= qpos, s, MASK_VALUE)
        m_old = m_ref[...]
        m_new = jnp.maximum(m_old, jnp.max(s, axis=1, keepdims=True))
        alpha = jnp.exp(m_old - m_new)
        p = jnp.exp(s - m_new)
        l_ref[...] = l_ref[...] * alpha + jnp.sum(p, axis=1, keepdims=True)
        pv = jnp.dot(p.astype(BF16), v_ref[pl.ds(start, t), :], preferred_element_type=F32)
        acc_ref[...] = acc_ref[...] * alpha + pv
        m_ref[...] = m_new

    def off_diag(j, carry):
        block(j, False)
        return carry

    lax.fori_loop(0, i, off_diag, 0)
    block(i, True)

    lam = lam_ref[...]
    lam_full = (jnp.exp(jnp.sum(lam[0:1] * lam[1:2], axis=1, keepdims=True))
                - jnp.exp(jnp.sum(lam[2:3] * lam[3:4], axis=1, keepdims=True)) + lam_init)
    o_all = acc_ref[...] / l_ref[...]
    o = o_all[:t] - lam_full * o_all[t:]
    o_ref[...] = (_rms(o, g_ref[...]) * (1.0 - lam_init)).astype(BF16)


def _diff_attention(proj, lam, subln_g, batch, seq, d, layer_idx):
    n = batch * seq
    heads = d // LANES
    t = min(512, seq)
    nq = seq // t
    lam_init = 0.8 - 0.6 * math.exp(-0.3 * layer_idx)
    return pl.pallas_call(
        functools.partial(_diff_attn_kernel, t=t, lam_init=lam_init),
        out_shape=jax.ShapeDtypeStruct((n, d), BF16),
        grid=(batch, heads, nq),
        in_specs=[
            pl.BlockSpec((4, HEAD_DIM), lambda b, h, i: (0, 0)),
            pl.BlockSpec((1, LANES), lambda b, h, i: (0, 0)),
            pl.BlockSpec((t, LANES), lambda b, h, i: (b * nq + i, h)),
            pl.BlockSpec((seq, LANES), lambda b, h, i: (b, heads + h)),
            pl.BlockSpec((seq, LANES), lambda b, h, i: (b, 2 * heads + h)),
        ],
        out_specs=pl.BlockSpec((t, LANES), lambda b, h, i: (b * nq + i, h)),
        scratch_shapes=[
            pltpu.VMEM((2 * t, 1), F32),
            pltpu.VMEM((2 * t, 1), F32),
            pltpu.VMEM((2 * t, LANES), F32),
        ],
        compiler_params=_cparams(("arbitrary", "arbitrary", "arbitrary")),
        name="diff_attention",
    )(lam, subln_g.reshape(1, LANES), proj, proj, proj)


def _sb_attn_kernel(q_ref, k_ref, v_ref, tri_ref, o_ref, c_ref, acc_ref, *, t):
    i = pl.program_id(2)
    qs = _stack_halves(q_ref[...])
    c_ref[...] = jnp.zeros(c_ref.shape, F32)
    acc_ref[...] = jnp.zeros(acc_ref.shape, F32)

    def block(j, masked):
        start = pl.multiple_of(j * t, t)
        z = _qk(qs, k_ref[pl.ds(start, t), :])
        log_beta = jnp.minimum(z, 0.0) - jnp.log1p(jnp.exp(-jnp.abs(z)))
        log_rest = log_beta - z
        if masked:
            qpos, kpos = _block_positions(z.shape, t, i, j)
            strict = kpos < qpos
            log_rest = jnp.where(strict, log_rest, 0.0)
        hi, lo = _split_bf16(log_rest)
        tri = tri_ref[...]
        within = (jnp.dot(hi, tri, preferred_element_type=F32)
                  + jnp.dot(lo, tri, preferred_element_type=F32))
        c = c_ref[...]
        a = jnp.exp(log_beta + within + c)
        if masked:
            a = jnp.where(strict, a, 0.0)
        acc_ref[...] += jnp.dot(a.astype(BF16), v_ref[pl.ds(start, t), :], preferred_element_type=F32)
        c_new = c + jnp.sum(log_rest, axis=1, keepdims=True)
        c_ref[...] = c_new
        return jnp.max(c_new)

    c_max = block(i, True)

    def cond(state):
        j, c_max = state
        return (j >= 0) & (c_max > SB_DEAD_LOG)

    def body(state):
        j, _ = state
        return j - 1, block(j, False)

    lax.while_loop(cond, body, (i - 1, c_max))

    lane = lax.broadcasted_iota(I32, (t, LANES), 1)
    acc = acc_ref[...]
    o_ref[...] = jnp.where(lane < HEAD_DIM, acc[:t], acc[t:]).astype(BF16)


def _sb_attention(proj, batch, seq, d):
    n = batch * seq
    pairs = d // LANES
    t = min(256, seq)
    nq = seq // t
    idx = jnp.arange(t)
    tri = (idx[:, None] > idx[None, :]).astype(BF16)
    return pl.pallas_call(
        functools.partial(_sb_attn_kernel, t=t),
        out_shape=jax.ShapeDtypeStruct((n, d), BF16),
        grid=(batch, pairs, nq),
        in_specs=[
            pl.BlockSpec((t, LANES), lambda b, h, i: (b * nq + i, h)),
            pl.BlockSpec((seq, LANES), lambda b, h, i: (b, pairs + h)),
            pl.BlockSpec((seq, LANES), lambda b, h, i: (b, 2 * pairs + h)),
            pl.BlockSpec((t, t), lambda b, h, i: (0, 0)),
        ],
        out_specs=pl.BlockSpec((t, LANES), lambda b, h, i: (b * nq + i, h)),
        scratch_shapes=[
            pltpu.VMEM((2 * t, 1), F32),
            pltpu.VMEM((2 * t, LANES), F32),
        ],
        compiler_params=_cparams(("arbitrary", "arbitrary", "arbitrary")),
        name="sb_attention",
    )(proj, proj, proj, tri)


def _first_index_of(mask, lane, sentinel):
    return jnp.min(jnp.where(mask, lane, float(sentinel)), axis=1, keepdims=True)


def _top2(v, lane):
    neg = -jnp.inf
    m1 = jnp.max(v, axis=1, keepdims=True)
    i1 = _first_index_of(v == m1, lane, N_EXPERTS)
    v2 = jnp.where(lane == i1, neg, v)
    m2 = jnp.max(v2, axis=1, keepdims=True)
    i2 = _first_index_of(v2 == m2, lane, N_EXPERTS)
    return m1, i1, m2, i2


def _outproj_route_kernel(o_ref, x_ref, wo_ref, ga_ref, g_ref, sc_ref, sh_ref, wr_ref, br_ref, tril_ref,
                          xo_ref, h_ref, meta_ref, cnt_ref):
    @pl.when(pl.program_id(0) == 0)
    def _():
        cnt_ref[...] = jnp.zeros(cnt_ref.shape, F32)

    mix = jnp.dot(o_ref[...], wo_ref[...], preferred_element_type=F32)
    xn = x_ref[...] + ga_ref[...] * mix
    xo_ref[...] = xn
    h = _rms(xn, g_ref[...]) * (1.0 + sc_ref[...]) + sh_ref[...]
    h_ref[...] = h

    logits = _dot3(h, wr_ref[...])
    e = jnp.exp(logits - jnp.max(logits, axis=1, keepdims=True))
    scores = e / jnp.sum(e, axis=1, keepdims=True)
    sel = scores + br_ref[...]
    lane_i = lax.broadcasted_iota(I32, sel.shape, 1)
    lane = lane_i.astype(F32)
    grp = lax.shift_right_logical(lane_i, int(math.log2(GROUP_SIZE))).astype(F32)
    neg = -jnp.inf

    best = None
    g_best = None
    for g in range(N_GROUPS):
        m1, _, m2, _ = _top2(jnp.where(grp == float(g), sel, neg), lane)
        score = m1 + m2
        if g == 0:
            best, g_best = score, jnp.zeros(score.shape, F32)
        else:
            better = score > best
            g_best = jnp.where(better, float(g), g_best)
            best = jnp.where(better, score, best)
    _, e1, _, e2 = _top2(jnp.where(grp == g_best, sel, neg), lane)
    hot1, hot2 = lane == e1, lane == e2
    s1 = jnp.sum(jnp.where(hot1, scores, 0.0), axis=1, keepdims=True)
    s2 = jnp.sum(jnp.where(hot2, scores, 0.0), axis=1, keepdims=True)
    tot = s1 + s2

    tril = tril_ref[...]
    one1, one2 = jnp.where(hot1, 1.0, 0.0), jnp.where(hot2, 1.0, 0.0)
    c1 = jnp.dot(tril, one1.astype(BF16), preferred_element_type=F32)
    c2 = jnp.dot(tril, one2.astype(BF16), preferred_element_type=F32)
    tot1 = jnp.sum(one1, axis=0, keepdims=True)
    tot2 = jnp.sum(one2, axis=0, keepdims=True)
    base = cnt_ref[...]
    r1 = jnp.sum(jnp.where(hot1, base + c1, 0.0), axis=1, keepdims=True)
    r2 = jnp.sum(jnp.where(hot2, base + tot1 + c2, 0.0), axis=1, keepdims=True)
    cnt_ref[...] = base + tot1 + tot2

    ml = lax.broadcasted_iota(I32, meta_ref.shape, 1)
    meta = jnp.where(ml == 0, s1 / tot, 0.0)
    for k, col in enumerate((s2 / tot, e1, e2, r1, r2)):
        meta = jnp.where(ml == k + 1, col, meta)
    meta_ref[...] = meta


def _outproj_route(o, x, wo_bf16, ga, g, sc, sh, w_router, b_router, seq):
    n, d = x.shape
    tm = min(512, seq)
    per_b = seq // tm
    row = lambda i: (i, 0)
    bat = lambda i: (i // per_b, 0, 0)
    const = lambda i: (0, 0)
    idx = jnp.arange(tm)
    tril = (idx[None, :] < idx[:, None]).astype(BF16)
    return pl.pallas_call(
        _outproj_route_kernel,
        out_shape=(
            jax.ShapeDtypeStruct((n, d), F32),
            jax.ShapeDtypeStruct((n, d), F32),
            jax.ShapeDtypeStruct((n, LANES), F32),
            jax.ShapeDtypeStruct((1, N_EXPERTS), F32),
        ),
        grid=(n // tm,),
        in_specs=[
            pl.BlockSpec((tm, d), row),
            pl.BlockSpec((tm, d), row),
            pl.BlockSpec((d, d), const),
            pl.BlockSpec((None, 1, d), bat),
            pl.BlockSpec((1, d), const),
            pl.BlockSpec((None, 1, d), bat),
            pl.BlockSpec((None, 1, d), bat),
            pl.BlockSpec((d, N_EXPERTS), const),
            pl.BlockSpec((1, N_EXPERTS), const),
            pl.BlockSpec((tm, tm), const),
        ],
        out_specs=(
            pl.BlockSpec((tm, d), row),
            pl.BlockSpec((tm, d), row),
            pl.BlockSpec((tm, LANES), row),
            pl.BlockSpec((1, N_EXPERTS), const),
        ),
        compiler_params=_cparams(("arbitrary",)),
        name="outproj_route",
    )(o, x, wo_bf16, ga, g.reshape(1, d), sc, sh, w_router, b_router.reshape(1, N_EXPERTS), tril)


def _row_copy(src_hbm, src_row, dst, dst_row, sem):
    return pltpu.make_async_copy(src_hbm.at[pl.ds(src_row, 1)], dst.at[pl.ds(dst_row, 1)], sem)


def _dispatch_kernel(dest_hbm, h_hbm, init_hbm, rows_hbm, dest_smem, idx_sem, row_sem, *, tc):
    del init_hbm
    c = pl.program_id(0)
    idx_copy = pltpu.make_async_copy(dest_hbm.at[pl.ds(c * 2 * tc, 2 * tc)], dest_smem, idx_sem)
    idx_copy.start()
    idx_copy.wait()

    def issue(tk, carry):
        tok = c * tc + tk
        _row_copy(h_hbm, tok, rows_hbm, dest_smem[tk], row_sem).start()
        _row_copy(h_hbm, tok, rows_hbm, dest_smem[tc + tk], row_sem).start()
        return carry

    lax.fori_loop(0, tc, issue, 0)

    def drain(tk, carry):
        _row_copy(h_hbm, 0, rows_hbm, 0, row_sem).wait()
        _row_copy(h_hbm, 0, rows_hbm, 0, row_sem).wait()
        return carry

    lax.fori_loop(0, tc, drain, 0)


def _dispatch(dest_flat, h, n_rows, tc):
    n, d = h.shape
    any_spec = pl.BlockSpec(memory_space=pl.ANY)
    return pl.pallas_call(
        functools.partial(_dispatch_kernel, tc=tc),
        out_shape=jax.ShapeDtypeStruct((n_rows, d), F32),
        grid=(n // tc,),
        in_specs=[any_spec, any_spec, any_spec],
        out_specs=any_spec,
        scratch_shapes=[pltpu.SMEM((2 * tc,), I32), pltpu.SemaphoreType.DMA, pltpu.SemaphoreType.DMA],
        input_output_aliases={2: 0},
        compiler_params=_cparams(("arbitrary",)),
        name="moe_dispatch",
    )(dest_flat, h, jnp.zeros((n_rows, d), F32))


def _experts_kernel(blk_e_ref, n_used_ref, x_ref, wg_ref, wu_ref, wd_ref, y_ref, wg_s, wu_s, wd_s):
    i = pl.program_id(0)
    used = i < n_used_ref[0]
    prev = blk_e_ref[jnp.maximum(i - 1, 0)]
    new_expert = (i == 0) | (blk_e_ref[i] != prev)

    @pl.when(used & new_expert)
    def _():
        wg_s[...] = wg_ref[...].astype(BF16)
        wu_s[...] = wu_ref[...].astype(BF16)
        wd_s[...] = wd_ref[...].astype(BF16)

    @pl.when(used)
    def _():
        xb = x_ref[...].astype(BF16)
        gate = jnp.dot(xb, wg_s[...], preferred_element_type=F32)
        up = jnp.dot(xb, wu_s[...], preferred_element_type=F32)
        hid = gate / (1.0 + jnp.exp(-gate)) * up
        y_ref[...] = jnp.dot(hid.astype(BF16), wd_s[...], preferred_element_type=F32)

    @pl.when(jnp.logical_not(used))
    def _():
        y_ref[...] = jnp.zeros(y_ref.shape, F32)


def _experts(blk_e, n_used, x_rows, w_gate, w_up, w_down):
    n_rows, d = x_rows.shape
    f = w_gate.shape[-1]
    n_blk = n_rows // EXPERT_ROWS
    last_used = lambda n_used_ref: jnp.maximum(n_used_ref[0] - 1, 0)
    x_map = lambda i, be, nu: (jnp.minimum(i, last_used(nu)), 0)
    w_map = lambda i, be, nu: (be[jnp.minimum(i, last_used(nu))], 0, 0)
    grid_spec = pltpu.PrefetchScalarGridSpec(
        num_scalar_prefetch=2,
        grid=(n_blk,),
        in_specs=[
            pl.BlockSpec((EXPERT_ROWS, d), x_map),
            pl.BlockSpec((None, d, f), w_map),
            pl.BlockSpec((None, d, f), w_map),
            pl.BlockSpec((None, f, d), w_map),
        ],
        out_specs=pl.BlockSpec((EXPERT_ROWS, d), lambda i, be, nu: (i, 0)),
        scratch_shapes=[pltpu.VMEM((d, f), BF16), pltpu.VMEM((d, f), BF16), pltpu.VMEM((f, d), BF16)],
    )
    return pl.pallas_call(
        _experts_kernel,
        out_shape=jax.ShapeDtypeStruct((n_rows, d), F32),
        grid_spec=grid_spec,
        compiler_params=_cparams(("arbitrary",)),
        name="moe_experts",
    )(blk_e, n_used, x_rows, w_gate, w_up, w_down)


def _combine_kernel(*refs, tc, final):
    if final:
        dest_hbm, y_hbm, x_ref, meta_ref, gm_ref, fg_ref, o_ref, dest_smem, buf, idx_sem, row_sem = refs
    else:
        dest_hbm, y_hbm, x_ref, meta_ref, gm_ref, o_ref, dest_smem, buf, idx_sem, row_sem = refs
    c = pl.program_id(0)
    idx_copy = pltpu.make_async_copy(dest_hbm.at[pl.ds(c * 2 * tc, 2 * tc)], dest_smem, idx_sem)
    idx_copy.start()
    idx_copy.wait()

    def issue(tk, carry):
        _row_copy(y_hbm, dest_smem[tk], buf.at[0], tk, row_sem).start()
        _row_copy(y_hbm, dest_smem[tc + tk], buf.at[1], tk, row_sem).start()
        return carry

    lax.fori_loop(0, tc, issue, 0)

    def drain(tk, carry):
        _row_copy(y_hbm, 0, buf.at[0], 0, row_sem).wait()
        _row_copy(y_hbm, 0, buf.at[1], 0, row_sem).wait()
        return carry

    lax.fori_loop(0, tc, drain, 0)

    meta = meta_ref[...]
    y = meta[:, 0:1] * buf[0] + meta[:, 1:2] * buf[1]
    xn = x_ref[...] + gm_ref[...] * y
    if final:
        xn = _rms(xn, fg_ref[...])
    o_ref[...] = xn


def _combine(dest_flat, y_rows, x, meta, gm, final_g, seq, tc):
    n, d = x.shape
    per_b = seq // tc
    row = lambda i: (i, 0)
    any_spec = pl.BlockSpec(memory_space=pl.ANY)
    final = final_g is not None
    in_specs = [
        any_spec,
        any_spec,
        pl.BlockSpec((tc, d), row),
        pl.BlockSpec((tc, LANES), row),
        pl.BlockSpec((None, 1, d), lambda i: (i // per_b, 0, 0)),
    ]
    args = [dest_flat, y_rows, x, meta, gm]
    if final:
        in_specs.append(pl.BlockSpec((1, d), lambda i: (0, 0)))
        args.append(final_g.reshape(1, d))
    return pl.pallas_call(
        functools.partial(_combine_kernel, tc=tc, final=final),
        out_shape=jax.ShapeDtypeStruct((n, d), F32),
        grid=(n // tc,),
        in_specs=in_specs,
        out_specs=pl.BlockSpec((tc, d), row),
        scratch_shapes=[
            pltpu.SMEM((2 * tc,), I32),
            pltpu.VMEM((2, tc, d), F32),
            pltpu.SemaphoreType.DMA,
            pltpu.SemaphoreType.DMA,
        ],
        compiler_params=_cparams(("arbitrary",)),
        name="moe_combine_final" if final else "moe_combine",
    )(*args)


def _moe(h, meta, counts, x, gm, w_gate, w_up, w_down, final_g, seq):
    n, d = h.shape
    tc = min(512, seq)
    n_rows = n * TOP_K + N_EXPERTS * EXPERT_ROWS
    n_blk = n_rows // EXPERT_ROWS
    cnt = counts[0].astype(I32)
    padded = ((cnt + EXPERT_ROWS - 1) // EXPERT_ROWS) * EXPERT_ROWS
    seg_end = jnp.cumsum(padded)
    seg_start = seg_end - padded
    n_used = (seg_end[-1:] // EXPERT_ROWS).astype(I32)
    blk_e = jnp.minimum(jnp.searchsorted(seg_end, jnp.arange(n_blk, dtype=I32) * EXPERT_ROWS, side="right"),
                        N_EXPERTS - 1).astype(I32)
    expert = meta[:, 2:4].astype(I32)
    rank = meta[:, 4:6].astype(I32)
    dest = seg_start[expert] + rank
    dest_flat = dest.reshape(n // tc, tc, TOP_K).transpose(0, 2, 1).reshape(-1)

    x_rows = _dispatch(dest_flat, h, n_rows, tc)
    y_rows = _experts(blk_e, n_used, x_rows, w_gate, w_up, w_down)
    return _combine(dest_flat, y_rows, x, meta, gm, final_g, seq, tc)


def kernel(x, c, positions, w_ada, b_ada, norm_g, w_in, w_out, diff_lambda, diff_subln_g, w_router, b_router,
           w_gate, w_up, w_down, final_norm_g):
    batch, seq, d = x.shape
    depth = w_ada.shape[0]
    n = batch * seq
    mod = _modulation(c, w_ada, b_ada)
    xf = x.reshape(n, d)
    for i in range(depth):
        sh_a, sc_a, g_a, sh_m, sc_m, g_m = [m.reshape(batch, 1, d) for m in jnp.split(mod[i], 6, axis=-1)]
        diff = i % 2 == 0
        proj = _norm_proj(xf, norm_g[i, 0], sc_a, sh_a, w_in[i].astype(BF16), positions, diff, seq)
        if diff:
            o = _diff_attention(proj, diff_lambda[i // 2], diff_subln_g[i // 2], batch, seq, d, i)
        else:
            o = _sb_attention(proj, batch, seq, d)
        xf, h, meta, counts = _outproj_route(o, xf, w_out[i].astype(BF16), g_a, norm_g[i, 1], sc_m, sh_m,
                                             w_router, b_router, seq)
        final_g = final_norm_g if i == depth - 1 else None
        xf = _moe(h, meta, counts, xf, g_m, w_gate[i], w_up[i], w_down[i], final_g, seq)
    return xf.reshape(batch, seq, d)
```

```python
import functools
import math

import jax
import jax.numpy as jnp
from jax import lax
from jax.experimental import pallas as pl
from jax.experimental.pallas import tpu as pltpu

F32 = jnp.float32
BF16 = jnp.bfloat16
I32 = jnp.int32

LANES = 128
HEAD_DIM = 64
ROPE_DIMS = HEAD_DIM // 4
ROPE_THETA = 500000.0
N_EXPERTS = 32
N_GROUPS = 4
GROUP_SIZE = N_EXPERTS // N_GROUPS
TOP_K = 2
NORM_EPS = 1e-6
EXPERT_ROWS = 256
SB_DEAD_LOG = -110.0
MASK_VALUE = -1e30
VMEM_LIMIT = 56 * 1024 * 1024


def _cparams(semantics, vmem=VMEM_LIMIT):
    return pltpu.CompilerParams(dimension_semantics=semantics, vmem_limit_bytes=vmem)


def _split_bf16(a):
    hi = a.astype(BF16)
    lo = (a - hi.astype(F32)).astype(BF16)
    return hi, lo


def _dot3(a, b):
    ah, al = _split_bf16(a)
    bh, bl = _split_bf16(b)
    d = functools.partial(jnp.dot, preferred_element_type=F32)
    return d(ah, bh) + d(ah, bl) + d(al, bh)


def _rms(x, g):
    ms = jnp.mean(x * x, axis=-1, keepdims=True)
    return x * lax.rsqrt(ms + NORM_EPS) * g


def _mod_kernel(c_ref, w_ref, b_ref, o_ref):
    c = c_ref[...]
    c_act = c / (1.0 + jnp.exp(-c))
    o_ref[...] = _dot3(c_act, w_ref[...]) + b_ref[...]


def _modulation(c, w_ada, b_ada):
    depth, d, six_d = w_ada.shape
    b = c.shape[0]
    rows = 16
    c_pad = jnp.zeros((rows, d), F32).at[:b].set(c)
    tn = 1536
    out = pl.pallas_call(
        _mod_kernel,
        out_shape=jax.ShapeDtypeStruct((depth, rows, six_d), F32),
        grid=(depth, six_d // tn),
        in_specs=[
            pl.BlockSpec((rows, d), lambda l, j: (0, 0)),
            pl.BlockSpec((None, d, tn), lambda l, j: (l, 0, j)),
            pl.BlockSpec((None, 1, tn), lambda l, j: (l, 0, j)),
        ],
        out_specs=pl.BlockSpec((None, rows, tn), lambda l, j: (l, 0, j)),
        compiler_params=_cparams(("arbitrary", "arbitrary")),
        name="adaln_mod",
    )(c_pad, w_ada, b_ada.reshape(depth, 1, six_d))
    return out[:, :b]


def _proj_kernel(*refs, d, rope):
    if rope:
        x_ref, g_ref, sc_ref, sh_ref, w_ref, pos_ref, pat_ref, o_ref = refs
    else:
        x_ref, g_ref, sc_ref, sh_ref, w_ref, o_ref = refs
    h = _rms(x_ref[...], g_ref[...]) * (1.0 + sc_ref[...]) + sh_ref[...]
    p = jnp.dot(h.astype(BF16), w_ref[...], preferred_element_type=F32)
    n_qk = 2 * d // LANES
    if rope:
        ang = pos_ref[...].astype(F32) * pat_ref[...]
        cs, sn = jnp.cos(ang), jnp.sin(ang)
        lane = lax.broadcasted_iota(I32, ang.shape, 1) & (HEAD_DIM - 1)
        half = ROPE_DIMS // 2
        s_lo = jnp.where(lane < half, -sn, 0.0)
        s_hi = jnp.where((lane >= half) & (lane < ROPE_DIMS), sn, 0.0)
    for j in range(n_qk):
        blk = p[:, j * LANES:(j + 1) * LANES]
        if rope:
            blk = (blk * cs + pltpu.roll(blk, half, 1) * s_hi
                   + pltpu.roll(blk, LANES - half, 1) * s_lo)
        if j < d // LANES:
            blk = blk * (HEAD_DIM ** -0.5)
        o_ref[:, j * LANES:(j + 1) * LANES] = blk.astype(BF16)
    o_ref[:, 2 * d:] = p[:, 2 * d:].astype(BF16)


def _norm_proj(x, g, sc, sh, w_bf16, positions, rope, seq):
    n, d = x.shape
    tm = min(512, seq)
    per_b = seq // tm
    row = lambda i: (i, 0)
    bat = lambda i: (i // per_b, 0, 0)
    in_specs = [
        pl.BlockSpec((tm, d), row),
        pl.BlockSpec((1, d), lambda i: (0, 0)),
        pl.BlockSpec((None, 1, d), bat),
        pl.BlockSpec((None, 1, d), bat),
        pl.BlockSpec((d, 3 * d), lambda i: (0, 0)),
    ]
    args = [x, g.reshape(1, d), sc, sh, w_bf16]
    if rope:
        half = ROPE_DIMS // 2
        inv_freq = jnp.power(jnp.float32(ROPE_THETA), -jnp.arange(half, dtype=F32) * (2.0 / ROPE_DIMS))
        lane = jnp.arange(LANES) % HEAD_DIM
        pat = jnp.where(lane < ROPE_DIMS, inv_freq[lane % half], 0.0).astype(F32).reshape(1, LANES)
        in_specs += [pl.BlockSpec((tm, 1), row), pl.BlockSpec((1, LANES), lambda i: (0, 0))]
        args += [positions.reshape(n, 1), pat]
    return pl.pallas_call(
        functools.partial(_proj_kernel, d=d, rope=rope),
        out_shape=jax.ShapeDtypeStruct((n, 3 * d), BF16),
        grid=(n // tm,),
        in_specs=in_specs,
        out_specs=pl.BlockSpec((tm, 3 * d), row),
        compiler_params=_cparams(("arbitrary",)),
        name="norm_proj_rope" if rope else "norm_proj",
    )(*args)


def _stack_halves(q):
    lane = lax.broadcasted_iota(I32, q.shape, 1)
    zero = jnp.zeros_like(q)
    return jnp.concatenate([jnp.where(lane < HEAD_DIM, q, zero), jnp.where(lane >= HEAD_DIM, q, zero)], axis=0)


def _qk(qs, kb):
    return lax.dot_general(qs, kb, (((1,), (1,)), ((), ())), preferred_element_type=F32)


def _lane_chunks(x):
    return [x[:, c * LANES:(c + 1) * LANES] for c in range(x.shape[1] // LANES)]


def _block_positions(shape, t, i, j):
    r = lax.broadcasted_iota(I32, shape, 0)
    qpos = i * t + jnp.where(r >= t, r - t, r)
    kpos = j * t + lax.broadcasted_iota(I32, shape, 1)
    return qpos, kpos


def _diff_attn_kernel(lam_ref, g_ref, q_ref, k_ref, v_ref, o_ref, m_ref, l_ref, acc_ref, *, t, lam_init):
    i = pl.program_id(2)
    qs = _stack_halves(q_ref[...])
    m_ref[...] = jnp.full(m_ref.shape, MASK_VALUE, F32)
    l_ref[...] = jnp.zeros(l_ref.shape, F32)
    acc_ref[...] = jnp.zeros(acc_ref.shape, F32)

    def block(j, masked):
        start = pl.multiple_of(j * t, t)
        s = _qk(qs, k_ref[pl.ds(start, t), :])
        if masked:
            qpos, kpos = _block_positions(s.shape, t, i, j)
            s = jnp.where(kpos <= qpos, s, MASK_VALUE)
        chunks = _lane_chunks(s)
        m_old = m_ref[...]
        m_new = jnp.maximum(m_old, jnp.max(functools.reduce(jnp.maximum, chunks), axis=1, keepdims=True))
        alpha = jnp.exp(m_old - m_new)
        ps = [jnp.exp(ch - m_new) for ch in chunks]
        l_ref[...] = l_ref[...] * alpha + functools.reduce(jnp.add, ps)
        p = jnp.concatenate([x.astype(BF16) for x in ps], axis=1)
        pv = jnp.dot(p, v_ref[pl.ds(start, t), :], preferred_element_type=F32)
        acc_ref[...] = acc_ref[...] * alpha + pv
        m_ref[...] = m_new

    def off_diag(j, carry):
        block(j, False)
        return carry

    lax.fori_loop(0, i, off_diag, 0)
    block(i, True)

    lam = lam_ref[...]
    lam_full = (jnp.exp(jnp.sum(lam[0:1] * lam[1:2], axis=1, keepdims=True))
                - jnp.exp(jnp.sum(lam[2:3] * lam[3:4], axis=1, keepdims=True)) + lam_init)
    o_all = acc_ref[...] / jnp.sum(l_ref[...], axis=1, keepdims=True)
    o = o_all[:t] - lam_full * o_all[t:]
    o_ref[...] = (_rms(o, g_ref[...]) * (1.0 - lam_init)).astype(BF16)


def _diff_attention(proj, lam, subln_g, batch, seq, d, layer_idx):
    n = batch * seq
    heads = d // LANES
    t = min(512, seq)
    nq = seq // t
    lam_init = 0.8 - 0.6 * math.exp(-0.3 * layer_idx)
    return pl.pallas_call(
        functools.partial(_diff_attn_kernel, t=t, lam_init=lam_init),
        out_shape=jax.ShapeDtypeStruct((n, d), BF16),
        grid=(batch, heads, nq),
        in_specs=[
            pl.BlockSpec((4, HEAD_DIM), lambda b, h, i: (0, 0)),
            pl.BlockSpec((1, LANES), lambda b, h, i: (0, 0)),
            pl.BlockSpec((t, LANES), lambda b, h, i: (b * nq + i, h)),
            pl.BlockSpec((seq, LANES), lambda b, h, i: (b, heads + h)),
            pl.BlockSpec((seq, LANES), lambda b, h, i: (b, 2 * heads + h)),
        ],
        out_specs=pl.BlockSpec((t, LANES), lambda b, h, i: (b * nq + i, h)),
        scratch_shapes=[
            pltpu.VMEM((2 * t, LANES), F32),
            pltpu.VMEM((2 * t, LANES), F32),
            pltpu.VMEM((2 * t, LANES), F32),
        ],
        compiler_params=_cparams(("arbitrary", "arbitrary", "arbitrary")),
        name="diff_attention",
    )(lam, subln_g.reshape(1, LANES), proj, proj, proj)


def _sb_attn_kernel(q_ref, k_ref, v_ref, tri_ref, o_ref, c_ref, acc_ref, *, t):
    i = pl.program_id(2)
    qs = _stack_halves(q_ref[...])
    c_ref[...] = jnp.zeros(c_ref.shape, F32)
    acc_ref[...] = jnp.zeros(acc_ref.shape, F32)

    def block(j, masked):
        start = pl.multiple_of(j * t, t)
        z = _qk(qs, k_ref[pl.ds(start, t), :])
        log_beta = jnp.minimum(z, 0.0) - jnp.log1p(jnp.exp(-jnp.abs(z)))
        log_rest = log_beta - z
        if masked:
            qpos, kpos = _block_positions(z.shape, t, i, j)
            strict = kpos < qpos
            log_rest = jnp.where(strict, log_rest, 0.0)
        hi, lo = _split_bf16(log_rest)
        tri = tri_ref[...]
        within = (jnp.dot(hi, tri, preferred_element_type=F32)
                  + jnp.dot(lo, tri, preferred_element_type=F32))
        c = c_ref[...]
        arg = log_beta + within
        a = jnp.concatenate([jnp.exp(ch + c) for ch in _lane_chunks(arg)], axis=1)
        if masked:
            a = jnp.where(strict, a, 0.0)
        acc_ref[...] += jnp.dot(a.astype(BF16), v_ref[pl.ds(start, t), :], preferred_element_type=F32)
        row_sum = jnp.sum(functools.reduce(jnp.add, _lane_chunks(log_rest)), axis=1, keepdims=True)
        c_new = c + row_sum
        c_ref[...] = c_new
        return jnp.max(c_new)

    c_max = block(i, True)

    def cond(state):
        j, c_max = state
        return (j >= 0) & (c_max > SB_DEAD_LOG)

    def body(state):
        j, _ = state
        return j - 1, block(j, False)

    lax.while_loop(cond, body, (i - 1, c_max))

    lane = lax.broadcasted_iota(I32, (t, LANES), 1)
    acc = acc_ref[...]
    o_ref[...] = jnp.where(lane < HEAD_DIM, acc[:t], acc[t:]).astype(BF16)


def _sb_attention(proj, batch, seq, d):
    n = batch * seq
    pairs = d // LANES
    t = min(256, seq)
    nq = seq // t
    idx = jnp.arange(t)
    tri = (idx[:, None] > idx[None, :]).astype(BF16)
    return pl.pallas_call(
        functools.partial(_sb_attn_kernel, t=t),
        out_shape=jax.ShapeDtypeStruct((n, d), BF16),
        grid=(batch, pairs, nq),
        in_specs=[
            pl.BlockSpec((t, LANES), lambda b, h, i: (b * nq + i, h)),
            pl.BlockSpec((seq, LANES), lambda b, h, i: (b, pairs + h)),
            pl.BlockSpec((seq, LANES), lambda b, h, i: (b, 2 * pairs + h)),
            pl.BlockSpec((t, t), lambda b, h, i: (0, 0)),
        ],
        out_specs=pl.BlockSpec((t, LANES), lambda b, h, i: (b * nq + i, h)),
        scratch_shapes=[
            pltpu.VMEM((2 * t, LANES), F32),
            pltpu.VMEM((2 * t, LANES), F32),
        ],
        compiler_params=_cparams(("arbitrary", "arbitrary", "arbitrary")),
        name="sb_attention",
    )(proj, proj, proj, tri)


def _first_index_of(mask, lane, sentinel):
    return jnp.min(jnp.where(mask, lane, float(sentinel)), axis=1, keepdims=True)


def _top2(v, lane):
    neg = -jnp.inf
    m1 = jnp.max(v, axis=1, keepdims=True)
    i1 = _first_index_of(v == m1, lane, N_EXPERTS)
    v2 = jnp.where(lane == i1, neg, v)
    m2 = jnp.max(v2, axis=1, keepdims=True)
    i2 = _first_index_of(v2 == m2, lane, N_EXPERTS)
    return m1, i1, m2, i2


def _outproj_route_kernel(o_ref, x_ref, wo_ref, ga_ref, g_ref, sc_ref, sh_ref, wr_ref, br_ref, tril_ref,
                          xo_ref, h_ref, meta_ref, meta_t_ref, cnt_ref):
    @pl.when(pl.program_id(0) == 0)
    def _():
        cnt_ref[...] = jnp.zeros(cnt_ref.shape, F32)

    mix = jnp.dot(o_ref[...], wo_ref[...], preferred_element_type=F32)
    xn = x_ref[...] + ga_ref[...] * mix
    xo_ref[...] = xn
    h = _rms(xn, g_ref[...]) * (1.0 + sc_ref[...]) + sh_ref[...]
    h_ref[...] = h

    logits = _dot3(h, wr_ref[...])
    e = jnp.exp(logits - jnp.max(logits, axis=1, keepdims=True))
    scores = e / jnp.sum(e, axis=1, keepdims=True)
    sel = scores + br_ref[...]
    lane_i = lax.broadcasted_iota(I32, sel.shape, 1)
    lane = lane_i.astype(F32)
    grp = lax.shift_right_logical(lane_i, int(math.log2(GROUP_SIZE))).astype(F32)
    neg = -jnp.inf

    best = None
    g_best = None
    for g in range(N_GROUPS):
        m1, _, m2, _ = _top2(jnp.where(grp == float(g), sel, neg), lane)
        score = m1 + m2
        if g == 0:
            best, g_best = score, jnp.zeros(score.shape, F32)
        else:
            better = score > best
            g_best = jnp.where(better, float(g), g_best)
            best = jnp.where(better, score, best)
    _, e1, _, e2 = _top2(jnp.where(grp == g_best, sel, neg), lane)
    hot1, hot2 = lane == e1, lane == e2
    s1 = jnp.sum(jnp.where(hot1, scores, 0.0), axis=1, keepdims=True)
    s2 = jnp.sum(jnp.where(hot2, scores, 0.0), axis=1, keepdims=True)
    tot = s1 + s2

    tril = tril_ref[...]
    one1, one2 = jnp.where(hot1, 1.0, 0.0), jnp.where(hot2, 1.0, 0.0)
    c1 = jnp.dot(tril, one1.astype(BF16), preferred_element_type=F32)
    c2 = jnp.dot(tril, one2.astype(BF16), preferred_element_type=F32)
    tot1 = jnp.sum(one1, axis=0, keepdims=True)
    tot2 = jnp.sum(one2, axis=0, keepdims=True)
    base = cnt_ref[...]
    r1 = jnp.sum(jnp.where(hot1, base + c1, 0.0), axis=1, keepdims=True)
    r2 = jnp.sum(jnp.where(hot2, base + tot1 + c2, 0.0), axis=1, keepdims=True)
    cnt_ref[...] = base + tot1 + tot2

    ml = lax.broadcasted_iota(I32, meta_ref.shape, 1)
    meta = jnp.where(ml == 0, s1 / tot, 0.0)
    for k, col in enumerate((s2 / tot, e1, e2, r1, r2)):
        meta = jnp.where(ml == k + 1, col, meta)
    meta_ref[...] = meta
    meta_t_ref[...] = meta.T[:meta_t_ref.shape[0]]


def _outproj_route(o, x, wo_bf16, ga, g, sc, sh, w_router, b_router, seq):
    n, d = x.shape
    tm = min(512, seq)
    per_b = seq // tm
    row = lambda i: (i, 0)
    bat = lambda i: (i // per_b, 0, 0)
    const = lambda i: (0, 0)
    idx = jnp.arange(tm)
    tril = (idx[None, :] < idx[:, None]).astype(BF16)
    return pl.pallas_call(
        _outproj_route_kernel,
        out_shape=(
            jax.ShapeDtypeStruct((n, d), F32),
            jax.ShapeDtypeStruct((n, d), F32),
            jax.ShapeDtypeStruct((n, LANES), F32),
            jax.ShapeDtypeStruct((8, n), F32),
            jax.ShapeDtypeStruct((1, N_EXPERTS), F32),
        ),
        grid=(n // tm,),
        in_specs=[
            pl.BlockSpec((tm, d), row),
            pl.BlockSpec((tm, d), row),
            pl.BlockSpec((d, d), const),
            pl.BlockSpec((None, 1, d), bat),
            pl.BlockSpec((1, d), const),
            pl.BlockSpec((None, 1, d), bat),
            pl.BlockSpec((None, 1, d), bat),
            pl.BlockSpec((d, N_EXPERTS), const),
            pl.BlockSpec((1, N_EXPERTS), const),
            pl.BlockSpec((tm, tm), const),
        ],
        out_specs=(
            pl.BlockSpec((tm, d), row),
            pl.BlockSpec((tm, d), row),
            pl.BlockSpec((tm, LANES), row),
            pl.BlockSpec((8, tm), lambda i: (0, i)),
            pl.BlockSpec((1, N_EXPERTS), const),
        ),
        compiler_params=_cparams(("arbitrary",)),
        name="outproj_route",
    )(o, x, wo_bf16, ga, g.reshape(1, d), sc, sh, w_router, b_router.reshape(1, N_EXPERTS), tril)


def _row_copy(src_hbm, src_row, dst, dst_row, sem):
    return pltpu.make_async_copy(src_hbm.at[pl.ds(src_row, 1)], dst.at[pl.ds(dst_row, 1)], sem)


def _dispatch_kernel(dest_hbm, h_ref, init_hbm, rows_hbm, dest_smem, idx_sem, row_sem, *, tc):
    del init_hbm
    c = pl.program_id(0)
    idx_copy = pltpu.make_async_copy(dest_hbm.at[pl.ds(c * 2 * tc, 2 * tc)], dest_smem, idx_sem)
    idx_copy.start()
    idx_copy.wait()

    def issue(tk, carry):
        _row_copy(h_ref, tk, rows_hbm, dest_smem[tk], row_sem).start()
        _row_copy(h_ref, tk, rows_hbm, dest_smem[tc + tk], row_sem).start()
        return carry

    lax.fori_loop(0, tc, issue, 0, unroll=8)

    def drain(tk, carry):
        _row_copy(h_ref, 0, rows_hbm, 0, row_sem).wait()
        _row_copy(h_ref, 0, rows_hbm, 0, row_sem).wait()
        return carry

    lax.fori_loop(0, tc, drain, 0, unroll=8)


def _dispatch(dest_flat, h, n_rows, tc):
    n, d = h.shape
    any_spec = pl.BlockSpec(memory_space=pl.ANY)
    return pl.pallas_call(
        functools.partial(_dispatch_kernel, tc=tc),
        out_shape=jax.ShapeDtypeStruct((n_rows, d), F32),
        grid=(n // tc,),
        in_specs=[any_spec, pl.BlockSpec((tc, d), lambda i: (i, 0)), any_spec],
        out_specs=any_spec,
        scratch_shapes=[pltpu.SMEM((2 * tc,), I32), pltpu.SemaphoreType.DMA, pltpu.SemaphoreType.DMA],
        input_output_aliases={2: 0},
        compiler_params=_cparams(("arbitrary",)),
        name="moe_dispatch",
    )(dest_flat, h, jnp.zeros((n_rows, d), F32))


def _experts_kernel(blk_e_ref, n_used_ref, x_ref, wg_ref, wu_ref, wd_ref, y_ref, wg_s, wu_s, wd_s):
    i = pl.program_id(0)
    used = i < n_used_ref[0]
    prev = blk_e_ref[jnp.maximum(i - 1, 0)]
    new_expert = (i == 0) | (blk_e_ref[i] != prev)

    @pl.when(used & new_expert)
    def _():
        wg_s[...] = wg_ref[...].astype(BF16)
        wu_s[...] = wu_ref[...].astype(BF16)
        wd_s[...] = wd_ref[...].astype(BF16)

    @pl.when(used)
    def _():
        xb = x_ref[...].astype(BF16)
        gate = jnp.dot(xb, wg_s[...], preferred_element_type=F32)
        up = jnp.dot(xb, wu_s[...], preferred_element_type=F32)
        hid = gate / (1.0 + jnp.exp(-gate)) * up
        y_ref[...] = jnp.dot(hid.astype(BF16), wd_s[...], preferred_element_type=F32)

    @pl.when(jnp.logical_not(used))
    def _():
        y_ref[...] = jnp.zeros(y_ref.shape, F32)


def _experts(blk_e, n_used, x_rows, w_gate, w_up, w_down):
    n_rows, d = x_rows.shape
    f = w_gate.shape[-1]
    n_blk = n_rows // EXPERT_ROWS
    last_used = lambda n_used_ref: jnp.maximum(n_used_ref[0] - 1, 0)
    x_map = lambda i, be, nu: (jnp.minimum(i, last_used(nu)), 0)
    w_map = lambda i, be, nu: (be[jnp.minimum(i, last_used(nu))], 0, 0)
    grid_spec = pltpu.PrefetchScalarGridSpec(
        num_scalar_prefetch=2,
        grid=(n_blk,),
        in_specs=[
            pl.BlockSpec((EXPERT_ROWS, d), x_map),
            pl.BlockSpec((None, d, f), w_map),
            pl.BlockSpec((None, d, f), w_map),
            pl.BlockSpec((None, f, d), w_map),
        ],
        out_specs=pl.BlockSpec((EXPERT_ROWS, d), lambda i, be, nu: (i, 0)),
        scratch_shapes=[pltpu.VMEM((d, f), BF16), pltpu.VMEM((d, f), BF16), pltpu.VMEM((f, d), BF16)],
    )
    return pl.pallas_call(
        _experts_kernel,
        out_shape=jax.ShapeDtypeStruct((n_rows, d), F32),
        grid_spec=grid_spec,
        compiler_params=_cparams(("arbitrary",)),
        name="moe_experts",
    )(blk_e, n_used, x_rows, w_gate, w_up, w_down)


def _combine_kernel(*refs, tc, final):
    if final:
        dest_hbm, y_hbm, x_ref, meta_ref, gm_ref, fg_ref, o_ref, dest_smem, buf, idx_sem, row_sem = refs
    else:
        dest_hbm, y_hbm, x_ref, meta_ref, gm_ref, o_ref, dest_smem, buf, idx_sem, row_sem = refs
    c = pl.program_id(0)
    idx_copy = pltpu.make_async_copy(dest_hbm.at[pl.ds(c * 2 * tc, 2 * tc)], dest_smem, idx_sem)
    idx_copy.start()
    idx_copy.wait()

    def issue(tk, carry):
        _row_copy(y_hbm, dest_smem[tk], buf.at[0], tk, row_sem).start()
        _row_copy(y_hbm, dest_smem[tc + tk], buf.at[1], tk, row_sem).start()
        return carry

    lax.fori_loop(0, tc, issue, 0, unroll=8)

    def drain(tk, carry):
        _row_copy(y_hbm, 0, buf.at[0], 0, row_sem).wait()
        _row_copy(y_hbm, 0, buf.at[1], 0, row_sem).wait()
        return carry

    lax.fori_loop(0, tc, drain, 0, unroll=8)

    meta = meta_ref[...]
    y = meta[:, 0:1] * buf[0] + meta[:, 1:2] * buf[1]
    xn = x_ref[...] + gm_ref[...] * y
    if final:
        xn = _rms(xn, fg_ref[...])
    o_ref[...] = xn


def _combine(dest_flat, y_rows, x, meta, gm, final_g, seq, tc):
    n, d = x.shape
    per_b = seq // tc
    row = lambda i: (i, 0)
    any_spec = pl.BlockSpec(memory_space=pl.ANY)
    final = final_g is not None
    in_specs = [
        any_spec,
        any_spec,
        pl.BlockSpec((tc, d), row),
        pl.BlockSpec((tc, LANES), row),
        pl.BlockSpec((None, 1, d), lambda i: (i // per_b, 0, 0)),
    ]
    args = [dest_flat, y_rows, x, meta, gm]
    if final:
        in_specs.append(pl.BlockSpec((1, d), lambda i: (0, 0)))
        args.append(final_g.reshape(1, d))
    return pl.pallas_call(
        functools.partial(_combine_kernel, tc=tc, final=final),
        out_shape=jax.ShapeDtypeStruct((n, d), F32),
        grid=(n // tc,),
        in_specs=in_specs,
        out_specs=pl.BlockSpec((tc, d), row),
        scratch_shapes=[
            pltpu.SMEM((2 * tc,), I32),
            pltpu.VMEM((2, tc, d), F32),
            pltpu.SemaphoreType.DMA,
            pltpu.SemaphoreType.DMA,
        ],
        compiler_params=_cparams(("arbitrary",)),
        name="moe_combine_final" if final else "moe_combine",
    )(*args)


def _moe(h, meta, meta_t, counts, x, gm, w_gate, w_up, w_down, final_g, seq):
    n, d = h.shape
    tc = min(512, seq)
    n_rows = n * TOP_K + N_EXPERTS * EXPERT_ROWS
    n_blk = n_rows // EXPERT_ROWS
    cnt = counts[0].astype(I32)
    padded = ((cnt + EXPERT_ROWS - 1) // EXPERT_ROWS) * EXPERT_ROWS
    seg_end = jnp.cumsum(padded)
    seg_start = seg_end - padded
    n_used = (seg_end[-1:] // EXPERT_ROWS).astype(I32)
    blk_start = jnp.arange(n_blk, dtype=I32) * EXPERT_ROWS
    blk_e = jnp.minimum(jnp.sum(seg_end[None, :] <= blk_start[:, None], axis=1), N_EXPERTS - 1).astype(I32)
    expert = meta_t[2:4].astype(I32)
    rank = meta_t[4:6].astype(I32)
    hot = expert[:, :, None] == jnp.arange(N_EXPERTS, dtype=I32)
    dest = rank + jnp.sum(jnp.where(hot, seg_start, 0), axis=-1)
    dest_flat = dest.reshape(TOP_K, n // tc, tc).transpose(1, 0, 2).reshape(-1)

    x_rows = _dispatch(dest_flat, h, n_rows, tc)
    y_rows = _experts(blk_e, n_used, x_rows, w_gate, w_up, w_down)
    return _combine(dest_flat, y_rows, x, meta, gm, final_g, seq, tc)


def kernel(x, c, positions, w_ada, b_ada, norm_g, w_in, w_out, diff_lambda, diff_subln_g, w_router, b_router,
           w_gate, w_up, w_down, final_norm_g):
    batch, seq, d = x.shape
    depth = w_ada.shape[0]
    n = batch * seq
    mod = _modulation(c, w_ada, b_ada)
    xf = x.reshape(n, d)
    for i in range(depth):
        sh_a, sc_a, g_a, sh_m, sc_m, g_m = [m.reshape(batch, 1, d) for m in jnp.split(mod[i], 6, axis=-1)]
        diff = i % 2 == 0
        proj = _norm_proj(xf, norm_g[i, 0], sc_a, sh_a, w_in[i].astype(BF16), positions, diff, seq)
        if diff:
            o = _diff_attention(proj, diff_lambda[i // 2], diff_subln_g[i // 2], batch, seq, d, i)
        else:
            o = _sb_attention(proj, batch, seq, d)
        xf, h, meta, meta_t, counts = _outproj_route(o, xf, w_out[i].astype(BF16), g_a, norm_g[i, 1], sc_m, sh_m,
                                                     w_router, b_router, seq)
        final_g = final_norm_g if i == depth - 1 else None
        xf = _moe(h, meta, meta_t, counts, xf, g_m, w_gate[i], w_up[i], w_down[i], final_g, seq)
    return xf.reshape(batch, seq, d)
```

```python
import functools
import math

import jax
import jax.numpy as jnp
from jax import lax
from jax.experimental import pallas as pl
from jax.experimental.pallas import tpu as pltpu

F32 = jnp.float32
BF16 = jnp.bfloat16
I32 = jnp.int32

LANES = 128
HEAD_DIM = 64
ROPE_DIMS = HEAD_DIM // 4
ROPE_THETA = 500000.0
N_EXPERTS = 32
N_GROUPS = 4
GROUP_SIZE = N_EXPERTS // N_GROUPS
TOP_K = 2
NORM_EPS = 1e-6
EXPERT_ROWS = 256
HEADS_PER_STEP = 2
SB_DEAD_LOG = -110.0
MASK_VALUE = -1e30
VMEM_LIMIT = 56 * 1024 * 1024


def _cparams(semantics, vmem=VMEM_LIMIT):
    return pltpu.CompilerParams(dimension_semantics=semantics, vmem_limit_bytes=vmem)


def _split_bf16(a):
    hi = a.astype(BF16)
    lo = (a - hi.astype(F32)).astype(BF16)
    return hi, lo


def _dot3(a, b):
    ah, al = _split_bf16(a)
    bh, bl = _split_bf16(b)
    d = functools.partial(jnp.dot, preferred_element_type=F32)
    return d(ah, bh) + d(ah, bl) + d(al, bh)


def _rms(x, g):
    ms = jnp.mean(x * x, axis=-1, keepdims=True)
    return x * lax.rsqrt(ms + NORM_EPS) * g


def _mod_kernel(c_ref, w_ref, b_ref, o_ref):
    c = c_ref[...]
    c_act = c / (1.0 + jnp.exp(-c))
    o_ref[...] = _dot3(c_act, w_ref[...]) + b_ref[...]


def _modulation(c, w_ada, b_ada):
    depth, d, six_d = w_ada.shape
    b = c.shape[0]
    rows = 16
    c_pad = jnp.zeros((rows, d), F32).at[:b].set(c)
    tn = 1536
    out = pl.pallas_call(
        _mod_kernel,
        out_shape=jax.ShapeDtypeStruct((depth, rows, six_d), F32),
        grid=(depth, six_d // tn),
        in_specs=[
            pl.BlockSpec((rows, d), lambda l, j: (0, 0)),
            pl.BlockSpec((None, d, tn), lambda l, j: (l, 0, j)),
            pl.BlockSpec((None, 1, tn), lambda l, j: (l, 0, j)),
        ],
        out_specs=pl.BlockSpec((None, rows, tn), lambda l, j: (l, 0, j)),
        compiler_params=_cparams(("arbitrary", "arbitrary")),
        name="adaln_mod",
    )(c_pad, w_ada, b_ada.reshape(depth, 1, six_d))
    return out[:, :b]


def _proj_kernel(*refs, d, rope, q_scale):
    if rope:
        x_ref, g_ref, sc_ref, sh_ref, w_ref, pos_ref, pat_ref, o_ref = refs
    else:
        x_ref, g_ref, sc_ref, sh_ref, w_ref, o_ref = refs
    h = _rms(x_ref[...], g_ref[...]) * (1.0 + sc_ref[...]) + sh_ref[...]
    p = jnp.dot(h.astype(BF16), w_ref[...], preferred_element_type=F32)
    n_qk = 2 * d // LANES
    if rope:
        ang = pos_ref[...].astype(F32) * pat_ref[...]
        cs, sn = jnp.cos(ang), jnp.sin(ang)
        lane = lax.broadcasted_iota(I32, ang.shape, 1) & (HEAD_DIM - 1)
        half = ROPE_DIMS // 2
        s_lo = jnp.where(lane < half, -sn, 0.0)
        s_hi = jnp.where((lane >= half) & (lane < ROPE_DIMS), sn, 0.0)
    for j in range(n_qk):
        blk = p[:, j * LANES:(j + 1) * LANES]
        if rope:
            blk = (blk * cs + pltpu.roll(blk, half, 1) * s_hi
                   + pltpu.roll(blk, LANES - half, 1) * s_lo)
        if j < d // LANES:
            blk = blk * q_scale
        o_ref[:, j * LANES:(j + 1) * LANES] = blk.astype(BF16)
    o_ref[:, 2 * d:] = p[:, 2 * d:].astype(BF16)


def _norm_proj(x, g, sc, sh, w_bf16, positions, rope, seq):
    n, d = x.shape
    tm = min(512, seq)
    per_b = seq // tm
    row = lambda i: (i, 0)
    bat = lambda i: (i // per_b, 0, 0)
    in_specs = [
        pl.BlockSpec((tm, d), row),
        pl.BlockSpec((1, d), lambda i: (0, 0)),
        pl.BlockSpec((None, 1, d), bat),
        pl.BlockSpec((None, 1, d), bat),
        pl.BlockSpec((d, 3 * d), lambda i: (0, 0)),
    ]
    args = [x, g.reshape(1, d), sc, sh, w_bf16]
    if rope:
        half = ROPE_DIMS // 2
        inv_freq = jnp.power(jnp.float32(ROPE_THETA), -jnp.arange(half, dtype=F32) * (2.0 / ROPE_DIMS))
        lane = jnp.arange(LANES) % HEAD_DIM
        pat = jnp.where(lane < ROPE_DIMS, inv_freq[lane % half], 0.0).astype(F32).reshape(1, LANES)
        in_specs += [pl.BlockSpec((tm, 1), row), pl.BlockSpec((1, LANES), lambda i: (0, 0))]
        args += [positions.reshape(n, 1), pat]
    q_scale = HEAD_DIM ** -0.5 * (math.log2(math.e) if rope else 1.0)
    return pl.pallas_call(
        functools.partial(_proj_kernel, d=d, rope=rope, q_scale=q_scale),
        out_shape=jax.ShapeDtypeStruct((n, 3 * d), BF16),
        grid=(n // tm,),
        in_specs=in_specs,
        out_specs=pl.BlockSpec((tm, 3 * d), row),
        compiler_params=_cparams(("arbitrary",)),
        name="norm_proj_rope" if rope else "norm_proj",
    )(*args)


def _stack_halves(q):
    lane = lax.broadcasted_iota(I32, q.shape, 1)
    zero = jnp.zeros_like(q)
    return jnp.concatenate([jnp.where(lane < HEAD_DIM, q, zero), jnp.where(lane >= HEAD_DIM, q, zero)], axis=0)


def _qk(qs, kb):
    return lax.dot_general(qs, kb, (((1,), (1,)), ((), ())), preferred_element_type=F32)


def _head_lanes(hh):
    return slice(hh * LANES, (hh + 1) * LANES)


def _lane_chunks(x):
    return [x[:, c * LANES:(c + 1) * LANES] for c in range(x.shape[1] // LANES)]


def _block_positions(shape, t, i, j):
    r = lax.broadcasted_iota(I32, shape, 0)
    qpos = i * t + jnp.where(r >= t, r - t, r)
    kpos = j * t + lax.broadcasted_iota(I32, shape, 1)
    return qpos, kpos


def _diff_attn_kernel(lam_ref, g_ref, q_ref, k_ref, v_ref, o_ref, m_ref, l_ref, acc_ref, *, t, lam_init):
    i = pl.program_id(2)
    qs = [_stack_halves(q_ref[:, _head_lanes(hh)]) for hh in range(HEADS_PER_STEP)]
    m_ref[...] = jnp.full(m_ref.shape, MASK_VALUE, F32)
    l_ref[...] = jnp.zeros(l_ref.shape, F32)
    acc_ref[...] = jnp.zeros(acc_ref.shape, F32)

    def scores(hh, j):
        start = pl.multiple_of(j * t, t)
        return _qk(qs[hh], k_ref[pl.ds(start, t), _head_lanes(hh)])

    def softmax_pv(hh, s, j, masked):
        start = pl.multiple_of(j * t, t)
        if masked:
            qpos, kpos = _block_positions(s.shape, t, i, j)
            s = jnp.where(kpos <= qpos, s, MASK_VALUE)
        chunks = _lane_chunks(s)
        m_old = m_ref[hh]
        m_new = jnp.maximum(m_old, jnp.max(functools.reduce(jnp.maximum, chunks), axis=1, keepdims=True))
        alpha = jnp.exp2(m_old - m_new)
        ps = [jnp.exp2(ch - m_new) for ch in chunks]
        l_ref[hh] = l_ref[hh] * alpha + functools.reduce(jnp.add, ps)
        p = jnp.concatenate([x.astype(BF16) for x in ps], axis=1)
        pv = jnp.dot(p, v_ref[pl.ds(start, t), _head_lanes(hh)], preferred_element_type=F32)
        acc_ref[hh] = acc_ref[hh] * alpha + pv
        m_ref[hh] = m_new

    def block(j, masked):
        s_all = [scores(hh, j) for hh in range(HEADS_PER_STEP)]
        for hh in range(HEADS_PER_STEP):
            softmax_pv(hh, s_all[hh], j, masked)

    def off_diag(j, carry):
        block(j, False)
        return carry

    lax.fori_loop(0, i, off_diag, 0)
    block(i, True)

    lam = lam_ref[...]
    lam_full = (jnp.exp(jnp.sum(lam[0:1] * lam[1:2], axis=1, keepdims=True))
                - jnp.exp(jnp.sum(lam[2:3] * lam[3:4], axis=1, keepdims=True)) + lam_init)
    for hh in range(HEADS_PER_STEP):
        o_all = acc_ref[hh] / jnp.sum(l_ref[hh], axis=1, keepdims=True)
        o = o_all[:t] - lam_full * o_all[t:]
        o_ref[:, _head_lanes(hh)] = (_rms(o, g_ref[...]) * (1.0 - lam_init)).astype(BF16)


def _attn_specs(t, seq, nq, groups):
    w = HEADS_PER_STEP * LANES
    q_spec = pl.BlockSpec((t, w), lambda b, h, i: (b * nq + i, h))
    k_spec = pl.BlockSpec((seq, w), lambda b, h, i: (b, groups + h))
    v_spec = pl.BlockSpec((seq, w), lambda b, h, i: (b, 2 * groups + h))
    return q_spec, k_spec, v_spec


def _diff_attention(proj, lam, subln_g, batch, seq, d, layer_idx):
    n = batch * seq
    groups = d // (HEADS_PER_STEP * LANES)
    t = min(512, seq)
    nq = seq // t
    lam_init = 0.8 - 0.6 * math.exp(-0.3 * layer_idx)
    q_spec, k_spec, v_spec = _attn_specs(t, seq, nq, groups)
    stat = pltpu.VMEM((HEADS_PER_STEP, 2 * t, LANES), F32)
    return pl.pallas_call(
        functools.partial(_diff_attn_kernel, t=t, lam_init=lam_init),
        out_shape=jax.ShapeDtypeStruct((n, d), BF16),
        grid=(batch, groups, nq),
        in_specs=[
            pl.BlockSpec((4, HEAD_DIM), lambda b, h, i: (0, 0)),
            pl.BlockSpec((1, LANES), lambda b, h, i: (0, 0)),
            q_spec, k_spec, v_spec,
        ],
        out_specs=q_spec,
        scratch_shapes=[stat, stat, stat],
        compiler_params=_cparams(("arbitrary", "arbitrary", "arbitrary")),
        name="diff_attention",
    )(lam, subln_g.reshape(1, LANES), proj, proj, proj)


def _sb_attn_kernel(q_ref, k_ref, v_ref, tri_ref, o_ref, c_ref, acc_ref, *, t):
    i = pl.program_id(2)
    qs = [_stack_halves(q_ref[:, _head_lanes(hh)]) for hh in range(HEADS_PER_STEP)]

    def gates(hh, j, diag, guard):
        start = pl.multiple_of(j * t, t)
        z = _qk(qs[hh], k_ref[pl.ds(start, t), _head_lanes(hh)])
        log_beta = jnp.minimum(z, 0.0) - jnp.log1p(jnp.exp(-jnp.abs(z)))
        log_rest = log_beta - z
        keep = None
        if diag:
            qpos, kpos = _block_positions(z.shape, t, i, j)
            keep = kpos < qpos
        if guard is not None:
            keep = lax.broadcasted_iota(I32, z.shape, 0) < guard
        if keep is not None:
            log_rest = jnp.where(keep, log_rest, 0.0)
        row_sum = jnp.sum(functools.reduce(jnp.add, _lane_chunks(log_rest)), axis=1, keepdims=True)
        return log_beta, log_rest, keep, row_sum

    def later_keys_sum(log_rest):
        hi, lo = _split_bf16(log_rest)
        tri = tri_ref[...]
        return jnp.dot(hi, tri, preferred_element_type=F32) + jnp.dot(lo, tri, preferred_element_type=F32)

    def weighted_values(hh, j, log_beta, within, keep, c):
        start = pl.multiple_of(j * t, t)
        arg = log_beta + within
        if c is not None:
            arg = jnp.concatenate([ch + c for ch in _lane_chunks(arg)], axis=1)
        a = jnp.exp(arg)
        if keep is not None:
            a = jnp.where(keep, a, 0.0)
        return jnp.dot(a.astype(BF16), v_ref[pl.ds(start, t), _head_lanes(hh)], preferred_element_type=F32)

    has_prev = jnp.where(i > 0, 2 * t, 0)
    j_prev = jnp.maximum(i - 1, 0)
    groups = range(HEADS_PER_STEP)
    g_diag = [gates(hh, i, True, None) for hh in groups]
    g_prev = [gates(hh, j_prev, False, has_prev) for hh in groups]
    w_diag = [later_keys_sum(g[1]) for g in g_diag]
    w_prev = [later_keys_sum(g[1]) for g in g_prev]
    c_max = None
    for hh in groups:
        c1 = jnp.broadcast_to(g_diag[hh][3], (2 * t, LANES))
        pv0 = weighted_values(hh, i, g_diag[hh][0], w_diag[hh], g_diag[hh][2], None)
        pv1 = weighted_values(hh, j_prev, g_prev[hh][0], w_prev[hh], g_prev[hh][2], c1)
        acc_ref[hh] = pv0 + pv1
        c2 = c1 + g_prev[hh][3]
        c_ref[hh] = c2
        c_max = jnp.max(c2) if c_max is None else jnp.maximum(c_max, jnp.max(c2))

    def cond(state):
        j, c_max = state
        return (j >= 0) & (c_max > SB_DEAD_LOG)

    def body(state):
        j, _ = state
        g_all = [gates(hh, j, False, None) for hh in groups]
        w_all = [later_keys_sum(g[1]) for g in g_all]
        c_max = None
        for hh in groups:
            c = c_ref[hh]
            acc_ref[hh] += weighted_values(hh, j, g_all[hh][0], w_all[hh], None, c)
            c_new = c + g_all[hh][3]
            c_ref[hh] = c_new
            c_max = jnp.max(c_new) if c_max is None else jnp.maximum(c_max, jnp.max(c_new))
        return j - 1, c_max

    lax.while_loop(cond, body, (i - 2, c_max))

    lane = lax.broadcasted_iota(I32, (t, LANES), 1)
    for hh in range(HEADS_PER_STEP):
        acc = acc_ref[hh]
        o_ref[:, _head_lanes(hh)] = jnp.where(lane < HEAD_DIM, acc[:t], acc[t:]).astype(BF16)


def _sb_attention(proj, batch, seq, d):
    n = batch * seq
    groups = d // (HEADS_PER_STEP * LANES)
    t = min(256, seq)
    nq = seq // t
    idx = jnp.arange(t)
    tri = (idx[:, None] > idx[None, :]).astype(BF16)
    q_spec, k_spec, v_spec = _attn_specs(t, seq, nq, groups)
    stat = pltpu.VMEM((HEADS_PER_STEP, 2 * t, LANES), F32)
    return pl.pallas_call(
        functools.partial(_sb_attn_kernel, t=t),
        out_shape=jax.ShapeDtypeStruct((n, d), BF16),
        grid=(batch, groups, nq),
        in_specs=[q_spec, k_spec, v_spec, pl.BlockSpec((t, t), lambda b, h, i: (0, 0))],
        out_specs=q_spec,
        scratch_shapes=[stat, stat],
        compiler_params=_cparams(("arbitrary", "arbitrary", "arbitrary")),
        name="sb_attention",
    )(proj, proj, proj, tri)


def _first_index_of(mask, lane, sentinel):
    return jnp.min(jnp.where(mask, lane, float(sentinel)), axis=1, keepdims=True)


def _top2(v, lane):
    neg = -jnp.inf
    m1 = jnp.max(v, axis=1, keepdims=True)
    i1 = _first_index_of(v == m1, lane, N_EXPERTS)
    v2 = jnp.where(lane == i1, neg, v)
    m2 = jnp.max(v2, axis=1, keepdims=True)
    i2 = _first_index_of(v2 == m2, lane, N_EXPERTS)
    return m1, i1, m2, i2


def _outproj_route_kernel(o_ref, x_ref, wo_ref, ga_ref, g_ref, sc_ref, sh_ref, wr_ref, br_ref, tril_ref,
                          xo_ref, h_ref, meta_ref, meta_t_ref, cnt_ref):
    @pl.when(pl.program_id(0) == 0)
    def _():
        cnt_ref[...] = jnp.zeros(cnt_ref.shape, F32)

    mix = jnp.dot(o_ref[...], wo_ref[...], preferred_element_type=F32)
    xn = x_ref[...] + ga_ref[...] * mix
    xo_ref[...] = xn
    h = _rms(xn, g_ref[...]) * (1.0 + sc_ref[...]) + sh_ref[...]
    h_ref[...] = h

    logits = _dot3(h, wr_ref[...])
    e = jnp.exp(logits - jnp.max(logits, axis=1, keepdims=True))
    scores = e / jnp.sum(e, axis=1, keepdims=True)
    sel = scores + br_ref[...]
    lane_i = lax.broadcasted_iota(I32, sel.shape, 1)
    lane = lane_i.astype(F32)
    grp = lax.shift_right_logical(lane_i, int(math.log2(GROUP_SIZE))).astype(F32)
    neg = -jnp.inf

    best = None
    g_best = None
    for g in range(N_GROUPS):
        m1, _, m2, _ = _top2(jnp.where(grp == float(g), sel, neg), lane)
        score = m1 + m2
        if g == 0:
            best, g_best = score, jnp.zeros(score.shape, F32)
        else:
            better = score > best
            g_best = jnp.where(better, float(g), g_best)
            best = jnp.where(better, score, best)
    _, e1, _, e2 = _top2(jnp.where(grp == g_best, sel, neg), lane)
    hot1, hot2 = lane == e1, lane == e2
    s1 = jnp.sum(jnp.where(hot1, scores, 0.0), axis=1, keepdims=True)
    s2 = jnp.sum(jnp.where(hot2, scores, 0.0), axis=1, keepdims=True)
    tot = s1 + s2

    tril = tril_ref[...]
    one1, one2 = jnp.where(hot1, 1.0, 0.0), jnp.where(hot2, 1.0, 0.0)
    c1 = jnp.dot(tril, one1.astype(BF16), preferred_element_type=F32)
    c2 = jnp.dot(tril, one2.astype(BF16), preferred_element_type=F32)
    tot1 = jnp.sum(one1, axis=0, keepdims=True)
    tot2 = jnp.sum(one2, axis=0, keepdims=True)
    base = cnt_ref[...]
    r1 = jnp.sum(jnp.where(hot1, base + c1, 0.0), axis=1, keepdims=True)
    r2 = jnp.sum(jnp.where(hot2, base + tot1 + c2, 0.0), axis=1, keepdims=True)
    cnt_ref[...] = base + tot1 + tot2

    ml = lax.broadcasted_iota(I32, meta_ref.shape, 1)
    meta = jnp.where(ml == 0, s1 / tot, 0.0)
    for k, col in enumerate((s2 / tot, e1, e2, r1, r2)):
        meta = jnp.where(ml == k + 1, col, meta)
    meta_ref[...] = meta
    meta_t_ref[...] = meta.T[:meta_t_ref.shape[0]]


def _outproj_route(o, x, wo_bf16, ga, g, sc, sh, w_router, b_router, seq):
    n, d = x.shape
    tm = min(512, seq)
    per_b = seq // tm
    row = lambda i: (i, 0)
    bat = lambda i: (i // per_b, 0, 0)
    const = lambda i: (0, 0)
    idx = jnp.arange(tm)
    tril = (idx[None, :] < idx[:, None]).astype(BF16)
    return pl.pallas_call(
        _outproj_route_kernel,
        out_shape=(
            jax.ShapeDtypeStruct((n, d), F32),
            jax.ShapeDtypeStruct((n, d), F32),
            jax.ShapeDtypeStruct((n, LANES), F32),
            jax.ShapeDtypeStruct((8, n), F32),
            jax.ShapeDtypeStruct((1, N_EXPERTS), F32),
        ),
        grid=(n // tm,),
        in_specs=[
            pl.BlockSpec((tm, d), row),
            pl.BlockSpec((tm, d), row),
            pl.BlockSpec((d, d), const),
            pl.BlockSpec((None, 1, d), bat),
            pl.BlockSpec((1, d), const),
            pl.BlockSpec((None, 1, d), bat),
            pl.BlockSpec((None, 1, d), bat),
            pl.BlockSpec((d, N_EXPERTS), const),
            pl.BlockSpec((1, N_EXPERTS), const),
            pl.BlockSpec((tm, tm), const),
        ],
        out_specs=(
            pl.BlockSpec((tm, d), row),
            pl.BlockSpec((tm, d), row),
            pl.BlockSpec((tm, LANES), row),
            pl.BlockSpec((8, tm), lambda i: (0, i)),
            pl.BlockSpec((1, N_EXPERTS), const),
        ),
        compiler_params=_cparams(("arbitrary",)),
        name="outproj_route",
    )(o, x, wo_bf16, ga, g.reshape(1, d), sc, sh, w_router, b_router.reshape(1, N_EXPERTS), tril)


def _row_copy(src_hbm, src_row, dst, dst_row, sem):
    return pltpu.make_async_copy(src_hbm.at[pl.ds(src_row, 1)], dst.at[pl.ds(dst_row, 1)], sem)


def _dispatch_kernel(dest_hbm, h_ref, init_hbm, rows_hbm, dest_smem, idx_sem, row_sem, *, tc):
    del init_hbm
    c = pl.program_id(0)
    idx_copy = pltpu.make_async_copy(dest_hbm.at[pl.ds(c * 2 * tc, 2 * tc)], dest_smem, idx_sem)
    idx_copy.start()
    idx_copy.wait()

    def issue(tk, carry):
        _row_copy(h_ref, tk, rows_hbm, dest_smem[tk], row_sem).start()
        _row_copy(h_ref, tk, rows_hbm, dest_smem[tc + tk], row_sem).start()
        return carry

    lax.fori_loop(0, tc, issue, 0, unroll=8)

    def drain(tk, carry):
        _row_copy(h_ref, 0, rows_hbm, 0, row_sem).wait()
        _row_copy(h_ref, 0, rows_hbm, 0, row_sem).wait()
        return carry

    lax.fori_loop(0, tc, drain, 0, unroll=8)


def _dispatch(dest_flat, h, n_rows, tc):
    n, d = h.shape
    any_spec = pl.BlockSpec(memory_space=pl.ANY)
    return pl.pallas_call(
        functools.partial(_dispatch_kernel, tc=tc),
        out_shape=jax.ShapeDtypeStruct((n_rows, d), F32),
        grid=(n // tc,),
        in_specs=[any_spec, pl.BlockSpec((tc, d), lambda i: (i, 0)), any_spec],
        out_specs=any_spec,
        scratch_shapes=[pltpu.SMEM((2 * tc,), I32), pltpu.SemaphoreType.DMA, pltpu.SemaphoreType.DMA],
        input_output_aliases={2: 0},
        compiler_params=_cparams(("arbitrary",)),
        name="moe_dispatch",
    )(dest_flat, h, jnp.zeros((n_rows, d), F32))


def _experts_kernel(blk_e_ref, n_used_ref, x_ref, wg_ref, wu_ref, wd_ref, y_ref, wg_s, wu_s, wd_s):
    i = pl.program_id(0)
    used = i < n_used_ref[0]
    prev = blk_e_ref[jnp.maximum(i - 1, 0)]
    new_expert = (i == 0) | (blk_e_ref[i] != prev)

    @pl.when(used & new_expert)
    def _():
        wg_s[...] = wg_ref[...].astype(BF16)
        wu_s[...] = wu_ref[...].astype(BF16)
        wd_s[...] = wd_ref[...].astype(BF16)

    @pl.when(used)
    def _():
        xb = x_ref[...].astype(BF16)
        gate = jnp.dot(xb, wg_s[...], preferred_element_type=F32)
        up = jnp.dot(xb, wu_s[...], preferred_element_type=F32)
        hid = gate / (1.0 + jnp.exp(-gate)) * up
        y_ref[...] = jnp.dot(hid.astype(BF16), wd_s[...], preferred_element_type=F32)

    @pl.when(jnp.logical_not(used))
    def _():
        y_ref[...] = jnp.zeros(y_ref.shape, F32)


def _experts(blk_e, n_used, x_rows, w_gate, w_up, w_down, layer):
    n_rows, d = x_rows.shape
    f = w_gate.shape[-1]
    n_blk = n_rows // EXPERT_ROWS
    last_used = lambda n_used_ref: jnp.maximum(n_used_ref[0] - 1, 0)
    x_map = lambda i, be, nu: (jnp.minimum(i, last_used(nu)), 0)
    w_map = lambda i, be, nu: (layer, be[jnp.minimum(i, last_used(nu))], 0, 0)
    grid_spec = pltpu.PrefetchScalarGridSpec(
        num_scalar_prefetch=2,
        grid=(n_blk,),
        in_specs=[
            pl.BlockSpec((EXPERT_ROWS, d), x_map),
            pl.BlockSpec((None, None, d, f), w_map),
            pl.BlockSpec((None, None, d, f), w_map),
            pl.BlockSpec((None, None, f, d), w_map),
        ],
        out_specs=pl.BlockSpec((EXPERT_ROWS, d), lambda i, be, nu: (i, 0)),
        scratch_shapes=[pltpu.VMEM((d, f), BF16), pltpu.VMEM((d, f), BF16), pltpu.VMEM((f, d), BF16)],
    )
    return pl.pallas_call(
        _experts_kernel,
        out_shape=jax.ShapeDtypeStruct((n_rows, d), F32),
        grid_spec=grid_spec,
        compiler_params=_cparams(("arbitrary",)),
        name="moe_experts",
    )(blk_e, n_used, x_rows, w_gate, w_up, w_down)


def _combine_kernel(*refs, tc, final):
    if final:
        dest_hbm, y_hbm, x_ref, meta_ref, gm_ref, fg_ref, o_ref, dest_smem, buf, idx_sem, row_sem = refs
    else:
        dest_hbm, y_hbm, x_ref, meta_ref, gm_ref, o_ref, dest_smem, buf, idx_sem, row_sem = refs
    c = pl.program_id(0)
    idx_copy = pltpu.make_async_copy(dest_hbm.at[pl.ds(c * 2 * tc, 2 * tc)], dest_smem, idx_sem)
    idx_copy.start()
    idx_copy.wait()

    def issue(tk, carry):
        _row_copy(y_hbm, dest_smem[tk], buf.at[0], tk, row_sem).start()
        _row_copy(y_hbm, dest_smem[tc + tk], buf.at[1], tk, row_sem).start()
        return carry

    lax.fori_loop(0, tc, issue, 0, unroll=8)

    def drain(tk, carry):
        _row_copy(y_hbm, 0, buf.at[0], 0, row_sem).wait()
        _row_copy(y_hbm, 0, buf.at[1], 0, row_sem).wait()
        return carry

    lax.fori_loop(0, tc, drain, 0, unroll=8)

    meta = meta_ref[...]
    y = meta[:, 0:1] * buf[0] + meta[:, 1:2] * buf[1]
    xn = x_ref[...] + gm_ref[...] * y
    if final:
        xn = _rms(xn, fg_ref[...])
    o_ref[...] = xn


def _combine(dest_flat, y_rows, x, meta, gm, final_g, seq, tc):
    n, d = x.shape
    per_b = seq // tc
    row = lambda i: (i, 0)
    any_spec = pl.BlockSpec(memory_space=pl.ANY)
    final = final_g is not None
    in_specs = [
        any_spec,
        any_spec,
        pl.BlockSpec((tc, d), row),
        pl.BlockSpec((tc, LANES), row),
        pl.BlockSpec((None, 1, d), lambda i: (i // per_b, 0, 0)),
    ]
    args = [dest_flat, y_rows, x, meta, gm]
    if final:
        in_specs.append(pl.BlockSpec((1, d), lambda i: (0, 0)))
        args.append(final_g.reshape(1, d))
    return pl.pallas_call(
        functools.partial(_combine_kernel, tc=tc, final=final),
        out_shape=jax.ShapeDtypeStruct((n, d), F32),
        grid=(n // tc,),
        in_specs=in_specs,
        out_specs=pl.BlockSpec((tc, d), row),
        scratch_shapes=[
            pltpu.SMEM((2 * tc,), I32),
            pltpu.VMEM((2, tc, d), F32),
            pltpu.SemaphoreType.DMA,
            pltpu.SemaphoreType.DMA,
        ],
        compiler_params=_cparams(("arbitrary",)),
        name="moe_combine_final" if final else "moe_combine",
    )(*args)


def _moe(h, meta, meta_t, counts, x, gm, w_gate, w_up, w_down, layer, final_g, seq):
    n, d = h.shape
    tc = min(512, seq)
    n_rows = n * TOP_K + N_EXPERTS * EXPERT_ROWS
    n_blk = n_rows // EXPERT_ROWS
    cnt = counts[0].astype(I32)
    padded = ((cnt + EXPERT_ROWS - 1) // EXPERT_ROWS) * EXPERT_ROWS
    seg_end = jnp.cumsum(padded)
    seg_start = seg_end - padded
    n_used = (seg_end[-1:] // EXPERT_ROWS).astype(I32)
    blk_start = jnp.arange(n_blk, dtype=I32) * EXPERT_ROWS
    blk_e = jnp.minimum(jnp.sum(seg_end[None, :] <= blk_start[:, None], axis=1), N_EXPERTS - 1).astype(I32)
    expert = meta_t[2:4].astype(I32)
    rank = meta_t[4:6].astype(I32)
    hot = expert[:, :, None] == jnp.arange(N_EXPERTS, dtype=I32)
    dest = rank + jnp.sum(jnp.where(hot, seg_start, 0), axis=-1)
    dest_flat = dest.reshape(TOP_K, n // tc, tc).transpose(1, 0, 2).reshape(-1)

    x_rows = _dispatch(dest_flat, h, n_rows, tc)
    y_rows = _experts(blk_e, n_used, x_rows, w_gate, w_up, w_down, layer)
    return _combine(dest_flat, y_rows, x, meta, gm, final_g, seq, tc)


def kernel(x, c, positions, w_ada, b_ada, norm_g, w_in, w_out, diff_lambda, diff_subln_g, w_router, b_router,
           w_gate, w_up, w_down, final_norm_g):
    batch, seq, d = x.shape
    depth = w_ada.shape[0]
    n = batch * seq
    mod = _modulation(c, w_ada, b_ada)
    xf = x.reshape(n, d)
    for i in range(depth):
        sh_a, sc_a, g_a, sh_m, sc_m, g_m = [m.reshape(batch, 1, d) for m in jnp.split(mod[i], 6, axis=-1)]
        diff = i % 2 == 0
        proj = _norm_proj(xf, norm_g[i, 0], sc_a, sh_a, w_in[i].astype(BF16), positions, diff, seq)
        if diff:
            o = _diff_attention(proj, diff_lambda[i // 2], diff_subln_g[i // 2], batch, seq, d, i)
        else:
            o = _sb_attention(proj, batch, seq, d)
        xf, h, meta, meta_t, counts = _outproj_route(o, xf, w_out[i].astype(BF16), g_a, norm_g[i, 1], sc_m, sh_m,
                                                     w_router, b_router, seq)
        final_g = final_norm_g if i == depth - 1 else None
        xf = _moe(h, meta, meta_t, counts, xf, g_m, w_gate, w_up, w_down, i, final_g, seq)
    return xf.reshape(batch, seq, d)
```

```python
import functools
import math

import jax
import jax.numpy as jnp
from jax import lax
from jax.experimental import pallas as pl
from jax.experimental.pallas import tpu as pltpu

F32 = jnp.float32
BF16 = jnp.bfloat16
I32 = jnp.int32

LANES = 128
HEAD_DIM = 64
ROPE_DIMS = HEAD_DIM // 4
ROPE_THETA = 500000.0
N_EXPERTS = 32
N_GROUPS = 4
GROUP_SIZE = N_EXPERTS // N_GROUPS
TOP_K = 2
NORM_EPS = 1e-6
EXPERT_ROWS = 256
HEADS_PER_STEP = 2
ROW_TILE = 8
SB_DEAD_LOG = -110.0
MASK_VALUE = -1e30
VMEM_LIMIT = 56 * 1024 * 1024


def _cparams(semantics, vmem=VMEM_LIMIT):
    return pltpu.CompilerParams(dimension_semantics=semantics, vmem_limit_bytes=vmem)


def _split_bf16(a):
    hi = a.astype(BF16)
    lo = (a - hi.astype(F32)).astype(BF16)
    return hi, lo


def _dot3(a, b):
    ah, al = _split_bf16(a)
    bh, bl = _split_bf16(b)
    d = functools.partial(jnp.dot, preferred_element_type=F32)
    return d(ah, bh) + d(ah, bl) + d(al, bh)


def _rms(x, g):
    ms = jnp.mean(x * x, axis=-1, keepdims=True)
    return x * lax.rsqrt(ms + NORM_EPS) * g


def _mod_kernel(c_ref, w_ref, b_ref, o_ref):
    c = c_ref[...]
    c_act = c / (1.0 + jnp.exp(-c))
    o_ref[...] = _dot3(c_act, w_ref[...]) + b_ref[...]


def _modulation(c, w_ada, b_ada):
    depth, d, six_d = w_ada.shape
    b = c.shape[0]
    rows = 16
    c_pad = jnp.zeros((rows, d), F32).at[:b].set(c)
    tn = 1536
    out = pl.pallas_call(
        _mod_kernel,
        out_shape=jax.ShapeDtypeStruct((depth, rows, six_d), F32),
        grid=(depth, six_d // tn),
        in_specs=[
            pl.BlockSpec((rows, d), lambda l, j: (0, 0)),
            pl.BlockSpec((None, d, tn), lambda l, j: (l, 0, j)),
            pl.BlockSpec((None, 1, tn), lambda l, j: (l, 0, j)),
        ],
        out_specs=pl.BlockSpec((None, rows, tn), lambda l, j: (l, 0, j)),
        compiler_params=_cparams(("arbitrary", "arbitrary")),
        name="adaln_mod",
    )(c_pad, w_ada, b_ada.reshape(depth, 1, six_d))
    return out[:, :b]


def _proj_kernel(*refs, d, rope, q_scale):
    if rope:
        x_ref, g_ref, sc_ref, sh_ref, w_ref, pos_ref, pat_ref, o_ref = refs
    else:
        x_ref, g_ref, sc_ref, sh_ref, w_ref, o_ref = refs
    h = _rms(x_ref[...], g_ref[...]) * (1.0 + sc_ref[...]) + sh_ref[...]
    p = jnp.dot(h.astype(BF16), w_ref[...], preferred_element_type=F32)
    n_qk = 2 * d // LANES
    if rope:
        ang = pos_ref[...].astype(F32) * pat_ref[...]
        cs, sn = jnp.cos(ang), jnp.sin(ang)
        lane = lax.broadcasted_iota(I32, ang.shape, 1) & (HEAD_DIM - 1)
        half = ROPE_DIMS // 2
        s_lo = jnp.where(lane < half, -sn, 0.0)
        s_hi = jnp.where((lane >= half) & (lane < ROPE_DIMS), sn, 0.0)
    for j in range(n_qk):
        blk = p[:, j * LANES:(j + 1) * LANES]
        if rope:
            blk = (blk * cs + pltpu.roll(blk, half, 1) * s_hi
                   + pltpu.roll(blk, LANES - half, 1) * s_lo)
        if j < d // LANES:
            blk = blk * q_scale
        o_ref[:, j * LANES:(j + 1) * LANES] = blk.astype(BF16)
    o_ref[:, 2 * d:] = p[:, 2 * d:].astype(BF16)


def _norm_proj(x, g, sc, sh, w_bf16, positions, rope, seq):
    n, d = x.shape
    tm = min(512, seq)
    per_b = seq // tm
    row = lambda i: (i, 0)
    bat = lambda i: (i // per_b, 0, 0)
    in_specs = [
        pl.BlockSpec((tm, d), row),
        pl.BlockSpec((1, d), lambda i: (0, 0)),
        pl.BlockSpec((None, 1, d), bat),
        pl.BlockSpec((None, 1, d), bat),
        pl.BlockSpec((d, 3 * d), lambda i: (0, 0)),
    ]
    args = [x, g.reshape(1, d), sc, sh, w_bf16]
    if rope:
        half = ROPE_DIMS // 2
        inv_freq = jnp.power(jnp.float32(ROPE_THETA), -jnp.arange(half, dtype=F32) * (2.0 / ROPE_DIMS))
        lane = jnp.arange(LANES) % HEAD_DIM
        pat = jnp.where(lane < ROPE_DIMS, inv_freq[lane % half], 0.0).astype(F32).reshape(1, LANES)
        in_specs += [pl.BlockSpec((tm, 1), row), pl.BlockSpec((1, LANES), lambda i: (0, 0))]
        args += [positions.reshape(n, 1), pat]
    q_scale = HEAD_DIM ** -0.5 * (math.log2(math.e) if rope else 1.0)
    return pl.pallas_call(
        functools.partial(_proj_kernel, d=d, rope=rope, q_scale=q_scale),
        out_shape=jax.ShapeDtypeStruct((n, 3 * d), BF16),
        grid=(n // tm,),
        in_specs=in_specs,
        out_specs=pl.BlockSpec((tm, 3 * d), row),
        compiler_params=_cparams(("arbitrary",)),
        name="norm_proj_rope" if rope else "norm_proj",
    )(*args)


def _stack_halves(q):
    lane = lax.broadcasted_iota(I32, q.shape, 1)
    zero = jnp.zeros_like(q)
    return jnp.concatenate([jnp.where(lane < HEAD_DIM, q, zero), jnp.where(lane >= HEAD_DIM, q, zero)], axis=0)


def _qk(qs, kb):
    return lax.dot_general(qs, kb, (((1,), (1,)), ((), ())), preferred_element_type=F32)


def _head_lanes(hh):
    return slice(hh * LANES, (hh + 1) * LANES)


def _lane_chunks(x):
    return [x[:, c * LANES:(c + 1) * LANES] for c in range(x.shape[1] // LANES)]


def _block_positions(shape, t, i, j):
    r = lax.broadcasted_iota(I32, shape, 0)
    qpos = i * t + jnp.where(r >= t, r - t, r)
    kpos = j * t + lax.broadcasted_iota(I32, shape, 1)
    return qpos, kpos


def _diff_attn_kernel(lam_ref, g_ref, q_ref, k_ref, v_ref, o_ref, m_ref, l_ref, acc_ref, *, t, lam_init):
    i = pl.program_id(2)
    qs = [_stack_halves(q_ref[:, _head_lanes(hh)]) for hh in range(HEADS_PER_STEP)]
    m_ref[...] = jnp.full(m_ref.shape, MASK_VALUE, F32)
    l_ref[...] = jnp.zeros(l_ref.shape, F32)
    acc_ref[...] = jnp.zeros(acc_ref.shape, F32)

    def scores(hh, j):
        start = pl.multiple_of(j * t, t)
        return _qk(qs[hh], k_ref[pl.ds(start, t), _head_lanes(hh)])

    def softmax_pv(hh, s, j, masked):
        start = pl.multiple_of(j * t, t)
        if masked:
            qpos, kpos = _block_positions(s.shape, t, i, j)
            s = jnp.where(kpos <= qpos, s, MASK_VALUE)
        chunks = _lane_chunks(s)
        m_old = m_ref[hh]
        m_new = jnp.maximum(m_old, jnp.max(functools.reduce(jnp.maximum, chunks), axis=1, keepdims=True))
        alpha = jnp.exp2(m_old - m_new)
        ps = [jnp.exp2(ch - m_new) for ch in chunks]
        l_ref[hh] = l_ref[hh] * alpha + functools.reduce(jnp.add, ps)
        p = jnp.concatenate([x.astype(BF16) for x in ps], axis=1)
        pv = jnp.dot(p, v_ref[pl.ds(start, t), _head_lanes(hh)], preferred_element_type=F32)
        acc_ref[hh] = acc_ref[hh] * alpha + pv
        m_ref[hh] = m_new

    def block(j, masked):
        s_all = [scores(hh, j) for hh in range(HEADS_PER_STEP)]
        for hh in range(HEADS_PER_STEP):
            softmax_pv(hh, s_all[hh], j, masked)

    def block_pair(j0, j1, masked1):
        s0 = [scores(hh, j0) for hh in range(HEADS_PER_STEP)]
        s1 = []
        for hh in range(HEADS_PER_STEP):
            softmax_pv(hh, s0[hh], j0, False)
            s1.append(scores(hh, j1))
        for hh in range(HEADS_PER_STEP):
            softmax_pv(hh, s1[hh], j1, masked1)

    def off_diag_pair(p, carry):
        block_pair(2 * p, 2 * p + 1, False)
        return carry

    lax.fori_loop(0, lax.shift_right_logical(i, 1), off_diag_pair, 0)

    @pl.when((i & 1) == 1)
    def _():
        block_pair(i - 1, i, True)

    @pl.when((i & 1) == 0)
    def _():
        block(i, True)

    lam = lam_ref[...]
    lam_full = (jnp.exp(jnp.sum(lam[0:1] * lam[1:2], axis=1, keepdims=True))
                - jnp.exp(jnp.sum(lam[2:3] * lam[3:4], axis=1, keepdims=True)) + lam_init)
    for hh in range(HEADS_PER_STEP):
        o_all = acc_ref[hh] / jnp.sum(l_ref[hh], axis=1, keepdims=True)
        o = o_all[:t] - lam_full * o_all[t:]
        o_ref[:, _head_lanes(hh)] = (_rms(o, g_ref[...]) * (1.0 - lam_init)).astype(BF16)


def _attn_specs(t, seq, nq, groups):
    w = HEADS_PER_STEP * LANES
    q_spec = pl.BlockSpec((t, w), lambda b, h, i: (b * nq + i, h))
    k_spec = pl.BlockSpec((seq, w), lambda b, h, i: (b, groups + h))
    v_spec = pl.BlockSpec((seq, w), lambda b, h, i: (b, 2 * groups + h))
    return q_spec, k_spec, v_spec


def _diff_attention(proj, lam, subln_g, batch, seq, d, layer_idx):
    n = batch * seq
    groups = d // (HEADS_PER_STEP * LANES)
    t = min(512, seq)
    nq = seq // t
    lam_init = 0.8 - 0.6 * math.exp(-0.3 * layer_idx)
    q_spec, k_spec, v_spec = _attn_specs(t, seq, nq, groups)
    stat = pltpu.VMEM((HEADS_PER_STEP, 2 * t, LANES), F32)
    return pl.pallas_call(
        functools.partial(_diff_attn_kernel, t=t, lam_init=lam_init),
        out_shape=jax.ShapeDtypeStruct((n, d), BF16),
        grid=(batch, groups, nq),
        in_specs=[
            pl.BlockSpec((4, HEAD_DIM), lambda b, h, i: (0, 0)),
            pl.BlockSpec((1, LANES), lambda b, h, i: (0, 0)),
            q_spec, k_spec, v_spec,
        ],
        out_specs=q_spec,
        scratch_shapes=[stat, stat, stat],
        compiler_params=_cparams(("arbitrary", "arbitrary", "arbitrary")),
        name="diff_attention",
    )(lam, subln_g.reshape(1, LANES), proj, proj, proj)


def _sb_attn_kernel(q_ref, k_ref, v_ref, tri_ref, o_ref, c_ref, acc_ref, *, t):
    i = pl.program_id(2)
    qs = [_stack_halves(q_ref[:, _head_lanes(hh)]) for hh in range(HEADS_PER_STEP)]

    def gates(hh, j, diag, guard):
        start = pl.multiple_of(j * t, t)
        z = _qk(qs[hh], k_ref[pl.ds(start, t), _head_lanes(hh)])
        log_beta = jnp.minimum(z, 0.0) - jnp.log(1.0 + jnp.exp(-jnp.abs(z)))
        log_rest = log_beta - z
        keep = None
        if diag:
            qpos, kpos = _block_positions(z.shape, t, i, j)
            keep = kpos < qpos
        if guard is not None:
            keep = lax.broadcasted_iota(I32, z.shape, 0) < guard
        if keep is not None:
            log_rest = jnp.where(keep, log_rest, 0.0)
        row_sum = jnp.sum(functools.reduce(jnp.add, _lane_chunks(log_rest)), axis=1, keepdims=True)
        return log_beta, log_rest, keep, row_sum

    def later_keys_sum(log_rest):
        hi, lo = _split_bf16(log_rest)
        tri = tri_ref[...]
        return jnp.dot(hi, tri, preferred_element_type=F32) + jnp.dot(lo, tri, preferred_element_type=F32)

    def weighted_values(hh, j, log_beta, within, keep, c):
        start = pl.multiple_of(j * t, t)
        arg = log_beta + within
        if c is not None:
            arg = jnp.concatenate([ch + c for ch in _lane_chunks(arg)], axis=1)
        a = jnp.exp(arg)
        if keep is not None:
            a = jnp.where(keep, a, 0.0)
        return jnp.dot(a.astype(BF16), v_ref[pl.ds(start, t), _head_lanes(hh)], preferred_element_type=F32)

    has_prev = jnp.where(i > 0, 2 * t, 0)
    j_prev = jnp.maximum(i - 1, 0)
    groups = range(HEADS_PER_STEP)
    g_diag = [gates(hh, i, True, None) for hh in groups]
    g_prev = [gates(hh, j_prev, False, has_prev) for hh in groups]
    w_diag = [later_keys_sum(g[1]) for g in g_diag]
    w_prev = [later_keys_sum(g[1]) for g in g_prev]
    c_max = None
    for hh in groups:
        c1 = jnp.broadcast_to(g_diag[hh][3], (2 * t, LANES))
        pv0 = weighted_values(hh, i, g_diag[hh][0], w_diag[hh], g_diag[hh][2], None)
        pv1 = weighted_values(hh, j_prev, g_prev[hh][0], w_prev[hh], g_prev[hh][2], c1)
        acc_ref[hh] = pv0 + pv1
        c2 = c1 + g_prev[hh][3]
        c_ref[hh] = c2
        c_max = jnp.max(c2) if c_max is None else jnp.maximum(c_max, jnp.max(c2))

    def cond(state):
        j, c_max = state
        return (j >= 0) & (c_max > SB_DEAD_LOG)

    def body(state):
        j, _ = state
        g_all = [gates(hh, j, False, None) for hh in groups]
        w_all = [later_keys_sum(g[1]) for g in g_all]
        c_max = None
        for hh in groups:
            c = c_ref[hh]
            acc_ref[hh] += weighted_values(hh, j, g_all[hh][0], w_all[hh], None, c)
            c_new = c + g_all[hh][3]
            c_ref[hh] = c_new
            c_max = jnp.max(c_new) if c_max is None else jnp.maximum(c_max, jnp.max(c_new))
        return j - 1, c_max

    lax.while_loop(cond, body, (i - 2, c_max))

    lane = lax.broadcasted_iota(I32, (t, LANES), 1)
    for hh in range(HEADS_PER_STEP):
        acc = acc_ref[hh]
        o_ref[:, _head_lanes(hh)] = jnp.where(lane < HEAD_DIM, acc[:t], acc[t:]).astype(BF16)


def _sb_attention(proj, batch, seq, d):
    n = batch * seq
    groups = d // (HEADS_PER_STEP * LANES)
    t = min(256, seq)
    nq = seq // t
    idx = jnp.arange(t)
    tri = (idx[:, None] > idx[None, :]).astype(BF16)
    q_spec, k_spec, v_spec = _attn_specs(t, seq, nq, groups)
    stat = pltpu.VMEM((HEADS_PER_STEP, 2 * t, LANES), F32)
    return pl.pallas_call(
        functools.partial(_sb_attn_kernel, t=t),
        out_shape=jax.ShapeDtypeStruct((n, d), BF16),
        grid=(batch, groups, nq),
        in_specs=[q_spec, k_spec, v_spec, pl.BlockSpec((t, t), lambda b, h, i: (0, 0))],
        out_specs=q_spec,
        scratch_shapes=[stat, stat],
        compiler_params=_cparams(("arbitrary", "arbitrary", "arbitrary")),
        name="sb_attention",
    )(proj, proj, proj, tri)


def _first_index_of(mask, lane, sentinel):
    return jnp.min(jnp.where(mask, lane, float(sentinel)), axis=1, keepdims=True)


def _top2(v, lane):
    neg = -jnp.inf
    m1 = jnp.max(v, axis=1, keepdims=True)
    i1 = _first_index_of(v == m1, lane, N_EXPERTS)
    v2 = jnp.where(lane == i1, neg, v)
    m2 = jnp.max(v2, axis=1, keepdims=True)
    i2 = _first_index_of(v2 == m2, lane, N_EXPERTS)
    return m1, i1, m2, i2


def _route(scores, sel):
    lane_i = lax.broadcasted_iota(I32, sel.shape, 1)
    lane = lane_i.astype(F32)
    grp = lax.shift_right_logical(lane_i, int(math.log2(GROUP_SIZE))).astype(F32)
    neg = -jnp.inf
    best = None
    g_best = None
    for g in range(N_GROUPS):
        m1, _, m2, _ = _top2(jnp.where(grp == float(g), sel, neg), lane)
        score = m1 + m2
        if g == 0:
            best, g_best = score, jnp.zeros(score.shape, F32)
        else:
            better = score > best
            g_best = jnp.where(better, float(g), g_best)
            best = jnp.where(better, score, best)
    _, e1, _, e2 = _top2(jnp.where(grp == g_best, sel, neg), lane)
    hot1, hot2 = lane == e1, lane == e2
    s1 = jnp.sum(jnp.where(hot1, scores, 0.0), axis=1, keepdims=True)
    s2 = jnp.sum(jnp.where(hot2, scores, 0.0), axis=1, keepdims=True)
    tot = s1 + s2
    return s1 / tot, s2 / tot, e1, e2, hot1, hot2


def _outproj_route_kernel(o_ref, x_ref, wo_ref, ga_ref, g_ref, sc_ref, sh_ref, wr_ref, br_ref, tril_ref,
                          xo_ref, h_ref, meta_ref, meta_t_ref, cnt_ref, *, ts, n_sub):
    @pl.when(pl.program_id(0) == 0)
    def _():
        cnt_ref[...] = jnp.zeros(cnt_ref.shape, F32)

    subs = range(n_sub)
    rows = [pl.ds(s * ts, ts) for s in subs]
    mix = [jnp.dot(o_ref[rows[s], :], wo_ref[...], preferred_element_type=F32) for s in subs]
    hs = []
    for s in subs:
        xn = x_ref[rows[s], :] + ga_ref[...] * mix[s]
        xo_ref[rows[s], :] = xn
        h = _rms(xn, g_ref[...]) * (1.0 + sc_ref[...]) + sh_ref[...]
        _store_rows(h_ref, s * ts, h)
        hs.append(h)

    logits = [_dot3(hs[s], wr_ref[...]) for s in subs]
    routed = []
    for s in subs:
        e = jnp.exp(logits[s] - jnp.max(logits[s], axis=1, keepdims=True))
        scores = e / jnp.sum(e, axis=1, keepdims=True)
        routed.append(_route(scores, scores + br_ref[...]))

    tril = tril_ref[...]
    ones = [(jnp.where(r[4], 1.0, 0.0), jnp.where(r[5], 1.0, 0.0)) for r in routed]
    prefix = [(jnp.dot(tril, o1.astype(BF16), preferred_element_type=F32),
               jnp.dot(tril, o2.astype(BF16), preferred_element_type=F32)) for o1, o2 in ones]
    base = cnt_ref[...]
    ml = lax.broadcasted_iota(I32, (ts, LANES), 1)
    for s in subs:
        g1, g2, e1, e2, hot1, hot2 = routed[s]
        tot1 = jnp.sum(ones[s][0], axis=0, keepdims=True)
        tot2 = jnp.sum(ones[s][1], axis=0, keepdims=True)
        r1 = jnp.sum(jnp.where(hot1, base + prefix[s][0], 0.0), axis=1, keepdims=True)
        r2 = jnp.sum(jnp.where(hot2, base + tot1 + prefix[s][1], 0.0), axis=1, keepdims=True)
        base = base + tot1 + tot2
        meta = jnp.where(ml == 0, g1, 0.0)
        for k, col in enumerate((g2, e1, e2, r1, r2)):
            meta = jnp.where(ml == k + 1, col, meta)
        meta_ref[rows[s], :] = meta
        meta_t_ref[:, rows[s]] = meta.T[:meta_t_ref.shape[0]]
    cnt_ref[...] = base


def _outproj_route(o, x, wo_bf16, ga, g, sc, sh, w_router, b_router, seq):
    n, d = x.shape
    tm = min(512, seq)
    ts = min(128, tm)
    per_b = seq // tm
    row = lambda i: (i, 0)
    bat = lambda i: (i // per_b, 0, 0)
    const = lambda i: (0, 0)
    idx = jnp.arange(ts)
    tril = (idx[None, :] < idx[:, None]).astype(BF16)
    return pl.pallas_call(
        functools.partial(_outproj_route_kernel, ts=ts, n_sub=tm // ts),
        out_shape=(
            jax.ShapeDtypeStruct((n, d), F32),
            jax.ShapeDtypeStruct((n * ROW_TILE, LANES), F32),
            jax.ShapeDtypeStruct((n, LANES), F32),
            jax.ShapeDtypeStruct((8, n), F32),
            jax.ShapeDtypeStruct((1, N_EXPERTS), F32),
        ),
        grid=(n // tm,),
        in_specs=[
            pl.BlockSpec((tm, d), row),
            pl.BlockSpec((tm, d), row),
            pl.BlockSpec((d, d), const),
            pl.BlockSpec((None, 1, d), bat),
            pl.BlockSpec((1, d), const),
            pl.BlockSpec((None, 1, d), bat),
            pl.BlockSpec((None, 1, d), bat),
            pl.BlockSpec((d, N_EXPERTS), const),
            pl.BlockSpec((1, N_EXPERTS), const),
            pl.BlockSpec((ts, ts), const),
        ],
        out_specs=(
            pl.BlockSpec((tm, d), row),
            pl.BlockSpec((tm * ROW_TILE, LANES), row),
            pl.BlockSpec((tm, LANES), row),
            pl.BlockSpec((8, tm), lambda i: (0, i)),
            pl.BlockSpec((1, N_EXPERTS), const),
        ),
        compiler_params=_cparams(("arbitrary",)),
        name="outproj_route",
    )(o, x, wo_bf16, ga, g.reshape(1, d), sc, sh, w_router, b_router.reshape(1, N_EXPERTS), tril)


def _row_copy(src, src_row, dst, dst_row, sem, rows=1):
    n = rows * ROW_TILE
    return pltpu.make_async_copy(src.at[pl.ds(pl.multiple_of(src_row * ROW_TILE, ROW_TILE), n)],
                                 dst.at[pl.ds(pl.multiple_of(dst_row * ROW_TILE, ROW_TILE), n)], sem)


def _load_rows(ref, start_row, rows):
    return jnp.concatenate([ref[pl.ds(start_row * ROW_TILE + k, rows, stride=ROW_TILE), :]
                            for k in range(ROW_TILE)], axis=1)


def _store_rows(ref, start_row, value):
    for k in range(ROW_TILE):
        ref[pl.ds(start_row * ROW_TILE + k, value.shape[0], stride=ROW_TILE), :] = value[:, k * LANES:(k + 1) * LANES]


def _dispatch_kernel(tail_ref, dest_hbm, h_ref, rows_hbm, dest_smem, zero_buf, idx_sem, row_sem, *, tc, n_blk):
    c = pl.program_id(0)

    @pl.when(c == 0)
    def _():
        zero_buf[...] = jnp.zeros(zero_buf.shape, F32)

        def fill(row):
            cp = _row_copy(zero_buf, 0, rows_hbm, row, row_sem, rows=EXPERT_ROWS)
            cp.start()
            cp.wait()

        def fill_tail(e, carry):
            fill(tail_ref[e])
            return carry

        def fill_unused_block(b, carry):
            fill(b * EXPERT_ROWS)
            return carry

        lax.fori_loop(0, N_EXPERTS, fill_tail, 0)
        lax.fori_loop(tail_ref[N_EXPERTS], n_blk, fill_unused_block, 0)

    idx_copy = pltpu.make_async_copy(dest_hbm.at[pl.ds(c * 2 * tc, 2 * tc)], dest_smem, idx_sem)
    idx_copy.start()
    idx_copy.wait()

    def issue(tk, carry):
        _row_copy(h_ref, tk, rows_hbm, dest_smem[tk], row_sem).start()
        _row_copy(h_ref, tk, rows_hbm, dest_smem[tc + tk], row_sem).start()
        return carry

    lax.fori_loop(0, tc, issue, 0, unroll=8)

    def drain(tk, carry):
        _row_copy(h_ref, 0, rows_hbm, 0, row_sem).wait()
        _row_copy(h_ref, 0, rows_hbm, 0, row_sem).wait()
        return carry

    lax.fori_loop(0, tc, drain, 0, unroll=8)


def _dispatch(tail, dest_flat, h_tiles, n_rows, tc):
    n = h_tiles.shape[0] // ROW_TILE
    any_spec = pl.BlockSpec(memory_space=pl.ANY)
    grid_spec = pltpu.PrefetchScalarGridSpec(
        num_scalar_prefetch=1,
        grid=(n // tc,),
        in_specs=[any_spec, pl.BlockSpec((tc * ROW_TILE, LANES), lambda i, tail: (i, 0))],
        out_specs=any_spec,
        scratch_shapes=[pltpu.SMEM((2 * tc,), I32), pltpu.VMEM((EXPERT_ROWS * ROW_TILE, LANES), F32),
                        pltpu.SemaphoreType.DMA, pltpu.SemaphoreType.DMA],
    )
    return pl.pallas_call(
        functools.partial(_dispatch_kernel, tc=tc, n_blk=n_rows // EXPERT_ROWS),
        out_shape=jax.ShapeDtypeStruct((n_rows * ROW_TILE, LANES), F32),
        grid_spec=grid_spec,
        compiler_params=_cparams(("arbitrary",)),
        name="moe_dispatch",
    )(tail, dest_flat, h_tiles)


def _experts_kernel(blk_e_ref, n_used_ref, x_ref, wg_ref, wu_ref, wd_ref, y_ref, wg_s, wu_s, wd_s):
    i = pl.program_id(0)
    used = i < n_used_ref[0]
    prev = blk_e_ref[jnp.maximum(i - 1, 0)]
    new_expert = (i == 0) | (blk_e_ref[i] != prev)

    @pl.when(used & new_expert)
    def _():
        wg_s[...] = wg_ref[...].astype(BF16)
        wu_s[...] = wu_ref[...].astype(BF16)
        wd_s[...] = wd_ref[...].astype(BF16)

    @pl.when(used)
    def _():
        xb = _load_rows(x_ref, 0, EXPERT_ROWS).astype(BF16)
        gate = jnp.dot(xb, wg_s[...], preferred_element_type=F32)
        up = jnp.dot(xb, wu_s[...], preferred_element_type=F32)
        hid = gate / (1.0 + jnp.exp(-gate)) * up
        _store_rows(y_ref, 0, jnp.dot(hid.astype(BF16), wd_s[...], preferred_element_type=F32))

    @pl.when(jnp.logical_not(used))
    def _():
        y_ref[...] = jnp.zeros(y_ref.shape, F32)


def _experts(blk_e, n_used, x_tiles, w_gate, w_up, w_down, layer):
    d, f = w_gate.shape[-2:]
    n_blk = x_tiles.shape[0] // (EXPERT_ROWS * ROW_TILE)
    blk = (EXPERT_ROWS * ROW_TILE, LANES)
    last_used = lambda n_used_ref: jnp.maximum(n_used_ref[0] - 1, 0)
    x_map = lambda i, be, nu: (jnp.minimum(i, last_used(nu)), 0)
    w_map = lambda i, be, nu: (layer, be[jnp.minimum(i, last_used(nu))], 0, 0)
    grid_spec = pltpu.PrefetchScalarGridSpec(
        num_scalar_prefetch=2,
        grid=(n_blk,),
        in_specs=[
            pl.BlockSpec(blk, x_map),
            pl.BlockSpec((None, None, d, f), w_map),
            pl.BlockSpec((None, None, d, f), w_map),
            pl.BlockSpec((None, None, f, d), w_map),
        ],
        out_specs=pl.BlockSpec(blk, lambda i, be, nu: (i, 0)),
        scratch_shapes=[pltpu.VMEM((d, f), BF16), pltpu.VMEM((d, f), BF16), pltpu.VMEM((f, d), BF16)],
    )
    return pl.pallas_call(
        _experts_kernel,
        out_shape=jax.ShapeDtypeStruct(x_tiles.shape, F32),
        grid_spec=grid_spec,
        compiler_params=_cparams(("arbitrary",)),
        name="moe_experts",
    )(blk_e, n_used, x_tiles, w_gate, w_up, w_down)


def _combine_kernel(*refs, tc, final):
    if final:
        dest_hbm, y_hbm, x_ref, meta_ref, gm_ref, fg_ref, o_ref, dest_smem, buf0, buf1, idx_sem, row_sem = refs
    else:
        dest_hbm, y_hbm, x_ref, meta_ref, gm_ref, o_ref, dest_smem, buf0, buf1, idx_sem, row_sem = refs
    c = pl.program_id(0)
    idx_copy = pltpu.make_async_copy(dest_hbm.at[pl.ds(c * 2 * tc, 2 * tc)], dest_smem, idx_sem)
    idx_copy.start()
    idx_copy.wait()

    def issue(tk, carry):
        _row_copy(y_hbm, dest_smem[tk], buf0, tk, row_sem).start()
        _row_copy(y_hbm, dest_smem[tc + tk], buf1, tk, row_sem).start()
        return carry

    lax.fori_loop(0, tc, issue, 0, unroll=8)

    def drain(tk, carry):
        _row_copy(y_hbm, 0, buf0, 0, row_sem).wait()
        _row_copy(y_hbm, 0, buf1, 0, row_sem).wait()
        return carry

    lax.fori_loop(0, tc, drain, 0, unroll=8)

    meta = meta_ref[...]
    y = meta[:, 0:1] * _load_rows(buf0, 0, tc) + meta[:, 1:2] * _load_rows(buf1, 0, tc)
    xn = x_ref[...] + gm_ref[...] * y
    if final:
        xn = _rms(xn, fg_ref[...])
    o_ref[...] = xn


def _combine(dest_flat, y_rows, x, meta, gm, final_g, seq, tc):
    n, d = x.shape
    per_b = seq // tc
    row = lambda i: (i, 0)
    any_spec = pl.BlockSpec(memory_space=pl.ANY)
    final = final_g is not None
    in_specs = [
        any_spec,
        any_spec,
        pl.BlockSpec((tc, d), row),
        pl.BlockSpec((tc, LANES), row),
        pl.BlockSpec((None, 1, d), lambda i: (i // per_b, 0, 0)),
    ]
    args = [dest_flat, y_rows, x, meta, gm]
    if final:
        in_specs.append(pl.BlockSpec((1, d), lambda i: (0, 0)))
        args.append(final_g.reshape(1, d))
    return pl.pallas_call(
        functools.partial(_combine_kernel, tc=tc, final=final),
        out_shape=jax.ShapeDtypeStruct((n, d), F32),
        grid=(n // tc,),
        in_specs=in_specs,
        out_specs=pl.BlockSpec((tc, d), row),
        scratch_shapes=[
            pltpu.SMEM((2 * tc,), I32),
            pltpu.VMEM((tc * ROW_TILE, LANES), F32),
            pltpu.VMEM((tc * ROW_TILE, LANES), F32),
            pltpu.SemaphoreType.DMA,
            pltpu.SemaphoreType.DMA,
        ],
        compiler_params=_cparams(("arbitrary",)),
        name="moe_combine_final" if final else "moe_combine",
    )(*args)


def _moe(h_tiles, meta, meta_t, counts, x, gm, w_gate, w_up, w_down, layer, final_g, seq):
    n, d = x.shape
    assert d == ROW_TILE * LANES
    tc = min(512, seq)
    n_rows = n * TOP_K + N_EXPERTS * EXPERT_ROWS
    n_blk = n_rows // EXPERT_ROWS
    cnt = counts[0].astype(I32)
    padded = ((cnt + EXPERT_ROWS - 1) // EXPERT_ROWS) * EXPERT_ROWS
    seg_end = jnp.cumsum(padded)
    seg_start = seg_end - padded
    n_used = (seg_end[-1:] // EXPERT_ROWS).astype(I32)
    blk_start = jnp.arange(n_blk, dtype=I32) * EXPERT_ROWS
    blk_e = jnp.minimum(jnp.sum(seg_end[None, :] <= blk_start[:, None], axis=1), N_EXPERTS - 1).astype(I32)
    expert = meta_t[2:4].astype(I32)
    rank = meta_t[4:6].astype(I32)
    hot = expert[:, :, None] == jnp.arange(N_EXPERTS, dtype=I32)
    dest = rank + jnp.sum(jnp.where(hot, seg_start, 0), axis=-1)
    dest_flat = dest.reshape(TOP_K, n // tc, tc).transpose(1, 0, 2).reshape(-1)

    tail = jnp.concatenate([jnp.minimum(seg_start + cnt, n_rows - EXPERT_ROWS), n_used]).astype(I32)

    x_rows = _dispatch(tail, dest_flat, h_tiles, n_rows, tc)
    y_rows = _experts(blk_e, n_used, x_rows, w_gate, w_up, w_down, layer)
    return _combine(dest_flat, y_rows, x, meta, gm, final_g, seq, tc)


def kernel(x, c, positions, w_ada, b_ada, norm_g, w_in, w_out, diff_lambda, diff_subln_g, w_router, b_router,
           w_gate, w_up, w_down, final_norm_g):
    batch, seq, d = x.shape
    depth = w_ada.shape[0]
    n = batch * seq
    mod = _modulation(c, w_ada, b_ada)
    xf = x.reshape(n, d)
    for i in range(depth):
        sh_a, sc_a, g_a, sh_m, sc_m, g_m = [m.reshape(batch, 1, d) for m in jnp.split(mod[i], 6, axis=-1)]
        diff = i % 2 == 0
        proj = _norm_proj(xf, norm_g[i, 0], sc_a, sh_a, w_in[i].astype(BF16), positions, diff, seq)
        if diff:
            o = _diff_attention(proj, diff_lambda[i // 2], diff_subln_g[i // 2], batch, seq, d, i)
        else:
            o = _sb_attention(proj, batch, seq, d)
        xf, h, meta, meta_t, counts = _outproj_route(o, xf, w_out[i].astype(BF16), g_a, norm_g[i, 1], sc_m, sh_m,
                                                     w_router, b_router, seq)
        final_g = final_norm_g if i == depth - 1 else None
        xf = _moe(h, meta, meta_t, counts, xf, g_m, w_gate, w_up, w_down, i, final_g, seq)
    return xf.reshape(batch, seq, d)
```

```python
import functools
import math

import jax
import jax.numpy as jnp
from jax import lax
from jax.experimental import pallas as pl
from jax.experimental.pallas import tpu as pltpu

F32 = jnp.float32
BF16 = jnp.bfloat16
I32 = jnp.int32

LANES = 128
HEAD_DIM = 64
ROPE_DIMS = HEAD_DIM // 4
ROPE_THETA = 500000.0
N_EXPERTS = 32
N_GROUPS = 4
GROUP_SIZE = N_EXPERTS // N_GROUPS
TOP_K = 2
NORM_EPS = 1e-6
EXPERT_ROWS = 256
HEADS_PER_STEP = 2
ROW_TILE = 8
SB_DEAD_LOG = -110.0
MASK_VALUE = -1e30
VMEM_LIMIT = 56 * 1024 * 1024


def _cparams(semantics, vmem=VMEM_LIMIT):
    return pltpu.CompilerParams(dimension_semantics=semantics, vmem_limit_bytes=vmem)


def _split_bf16(a):
    hi = a.astype(BF16)
    lo = (a - hi.astype(F32)).astype(BF16)
    return hi, lo


def _dot3(a, b):
    ah, al = _split_bf16(a)
    bh, bl = _split_bf16(b)
    d = functools.partial(jnp.dot, preferred_element_type=F32)
    return d(ah, bh) + d(ah, bl) + d(al, bh)


def _rms(x, g):
    ms = jnp.mean(x * x, axis=-1, keepdims=True)
    return x * lax.rsqrt(ms + NORM_EPS) * g


def _mod_kernel(c_ref, w_ref, b_ref, o_ref):
    c = c_ref[...]
    c_act = c / (1.0 + jnp.exp(-c))
    o_ref[...] = _dot3(c_act, w_ref[...]) + b_ref[...]


def _modulation(c, w_ada, b_ada):
    depth, d, six_d = w_ada.shape
    b = c.shape[0]
    rows = 16
    c_pad = jnp.zeros((rows, d), F32).at[:b].set(c)
    tn = 1536
    out = pl.pallas_call(
        _mod_kernel,
        out_shape=jax.ShapeDtypeStruct((depth, rows, six_d), F32),
        grid=(depth, six_d // tn),
        in_specs=[
            pl.BlockSpec((rows, d), lambda l, j: (0, 0)),
            pl.BlockSpec((None, d, tn), lambda l, j: (l, 0, j)),
            pl.BlockSpec((None, 1, tn), lambda l, j: (l, 0, j)),
        ],
        out_specs=pl.BlockSpec((None, rows, tn), lambda l, j: (l, 0, j)),
        compiler_params=_cparams(("arbitrary", "arbitrary")),
        name="adaln_mod",
    )(c_pad, w_ada, b_ada.reshape(depth, 1, six_d))
    return out[:, :b]


def _proj_kernel(*refs, d, rope, q_scale):
    if rope:
        x_ref, g_ref, sc_ref, sh_ref, w_ref, pos_ref, pat_ref, o_ref = refs
    else:
        x_ref, g_ref, sc_ref, sh_ref, w_ref, o_ref = refs
    h = _rms(x_ref[...], g_ref[...]) * (1.0 + sc_ref[...]) + sh_ref[...]
    p = jnp.dot(h.astype(BF16), w_ref[...], preferred_element_type=F32)
    n_qk = 2 * d // LANES
    if rope:
        ang = pos_ref[...].astype(F32) * pat_ref[...]
        cs, sn = jnp.cos(ang), jnp.sin(ang)
        lane = lax.broadcasted_iota(I32, ang.shape, 1) & (HEAD_DIM - 1)
        half = ROPE_DIMS // 2
        s_lo = jnp.where(lane < half, -sn, 0.0)
        s_hi = jnp.where((lane >= half) & (lane < ROPE_DIMS), sn, 0.0)
    for j in range(n_qk):
        blk = p[:, j * LANES:(j + 1) * LANES]
        if rope:
            blk = (blk * cs + pltpu.roll(blk, half, 1) * s_hi
                   + pltpu.roll(blk, LANES - half, 1) * s_lo)
        if j < d // LANES:
            blk = blk * q_scale
        o_ref[:, j * LANES:(j + 1) * LANES] = blk.astype(BF16)
    o_ref[:, 2 * d:] = p[:, 2 * d:].astype(BF16)


def _norm_proj(x, g, sc, sh, w_bf16, positions, rope, seq):
    n, d = x.shape
    tm = min(512, seq)
    per_b = seq // tm
    row = lambda i: (i, 0)
    bat = lambda i: (i // per_b, 0, 0)
    in_specs = [
        pl.BlockSpec((tm, d), row),
        pl.BlockSpec((1, d), lambda i: (0, 0)),
        pl.BlockSpec((None, 1, d), bat),
        pl.BlockSpec((None, 1, d), bat),
        pl.BlockSpec((d, 3 * d), lambda i: (0, 0)),
    ]
    args = [x, g.reshape(1, d), sc, sh, w_bf16]
    if rope:
        half = ROPE_DIMS // 2
        inv_freq = jnp.power(jnp.float32(ROPE_THETA), -jnp.arange(half, dtype=F32) * (2.0 / ROPE_DIMS))
        lane = jnp.arange(LANES) % HEAD_DIM
        pat = jnp.where(lane < ROPE_DIMS, inv_freq[lane % half], 0.0).astype(F32).reshape(1, LANES)
        in_specs += [pl.BlockSpec((tm, 1), row), pl.BlockSpec((1, LANES), lambda i: (0, 0))]
        args += [positions.reshape(n, 1), pat]
    q_scale = HEAD_DIM ** -0.5 * (math.log2(math.e) if rope else 1.0)
    return pl.pallas_call(
        functools.partial(_proj_kernel, d=d, rope=rope, q_scale=q_scale),
        out_shape=jax.ShapeDtypeStruct((n, 3 * d), BF16),
        grid=(n // tm,),
        in_specs=in_specs,
        out_specs=pl.BlockSpec((tm, 3 * d), row),
        compiler_params=_cparams(("arbitrary",)),
        name="norm_proj_rope" if rope else "norm_proj",
    )(*args)


def _stack_halves(q):
    lane = lax.broadcasted_iota(I32, q.shape, 1)
    zero = jnp.zeros_like(q)
    return jnp.concatenate([jnp.where(lane < HEAD_DIM, q, zero), jnp.where(lane >= HEAD_DIM, q, zero)], axis=0)


def _qk(qs, kb):
    return lax.dot_general(qs, kb, (((1,), (1,)), ((), ())), preferred_element_type=F32)


def _head_lanes(hh):
    return slice(hh * LANES, (hh + 1) * LANES)


def _lane_chunks(x):
    return [x[:, c * LANES:(c + 1) * LANES] for c in range(x.shape[1] // LANES)]


def _block_positions(shape, t, i, j):
    r = lax.broadcasted_iota(I32, shape, 0)
    qpos = i * t + jnp.where(r >= t, r - t, r)
    kpos = j * t + lax.broadcasted_iota(I32, shape, 1)
    return qpos, kpos


def _diff_attn_kernel(lam_ref, g_ref, q_ref, k_ref, v_ref, o_ref, m_ref, l_ref, acc_ref, *, t, lam_init):
    i = pl.program_id(2)
    qs = [_stack_halves(q_ref[:, _head_lanes(hh)]) for hh in range(HEADS_PER_STEP)]
    m_ref[...] = jnp.full(m_ref.shape, MASK_VALUE, F32)
    l_ref[...] = jnp.zeros(l_ref.shape, F32)
    acc_ref[...] = jnp.zeros(acc_ref.shape, F32)

    def scores(hh, j):
        start = pl.multiple_of(j * t, t)
        return _qk(qs[hh], k_ref[pl.ds(start, t), _head_lanes(hh)])

    def softmax_pv(hh, s, j, masked):
        start = pl.multiple_of(j * t, t)
        if masked:
            qpos, kpos = _block_positions(s.shape, t, i, j)
            s = jnp.where(kpos <= qpos, s, MASK_VALUE)
        chunks = _lane_chunks(s)
        m_old = m_ref[hh]
        m_new = jnp.maximum(m_old, jnp.max(functools.reduce(jnp.maximum, chunks), axis=1, keepdims=True))
        alpha = jnp.exp2(m_old - m_new)
        ps = [jnp.exp2(ch - m_new) for ch in chunks]
        l_ref[hh] = l_ref[hh] * alpha + functools.reduce(jnp.add, ps)
        p = jnp.concatenate([x.astype(BF16) for x in ps], axis=1)
        pv = jnp.dot(p, v_ref[pl.ds(start, t), _head_lanes(hh)], preferred_element_type=F32)
        acc_ref[hh] = acc_ref[hh] * alpha + pv
        m_ref[hh] = m_new

    def block(j, masked):
        s_all = [scores(hh, j) for hh in range(HEADS_PER_STEP)]
        for hh in range(HEADS_PER_STEP):
            softmax_pv(hh, s_all[hh], j, masked)

    def block_pair(j0, j1, masked1):
        s0 = [scores(hh, j0) for hh in range(HEADS_PER_STEP)]
        s1 = []
        for hh in range(HEADS_PER_STEP):
            softmax_pv(hh, s0[hh], j0, False)
            s1.append(scores(hh, j1))
        for hh in range(HEADS_PER_STEP):
            softmax_pv(hh, s1[hh], j1, masked1)

    def off_diag_pair(p, carry):
        block_pair(2 * p, 2 * p + 1, False)
        return carry

    lax.fori_loop(0, lax.shift_right_logical(i, 1), off_diag_pair, 0)

    @pl.when((i & 1) == 1)
    def _():
        block_pair(i - 1, i, True)

    @pl.when((i & 1) == 0)
    def _():
        block(i, True)

    lam = lam_ref[...]
    lam_full = (jnp.exp(jnp.sum(lam[0:1] * lam[1:2], axis=1, keepdims=True))
                - jnp.exp(jnp.sum(lam[2:3] * lam[3:4], axis=1, keepdims=True)) + lam_init)
    for hh in range(HEADS_PER_STEP):
        o_all = acc_ref[hh] / jnp.sum(l_ref[hh], axis=1, keepdims=True)
        o = o_all[:t] - lam_full * o_all[t:]
        o_ref[:, _head_lanes(hh)] = (_rms(o, g_ref[...]) * (1.0 - lam_init)).astype(BF16)


def _attn_specs(t, seq, nq, groups):
    w = HEADS_PER_STEP * LANES
    q_spec = pl.BlockSpec((t, w), lambda b, h, i: (b * nq + i, h))
    k_spec = pl.BlockSpec((seq, w), lambda b, h, i: (b, groups + h))
    v_spec = pl.BlockSpec((seq, w), lambda b, h, i: (b, 2 * groups + h))
    return q_spec, k_spec, v_spec


def _diff_attention(proj, lam, subln_g, batch, seq, d, layer_idx):
    n = batch * seq
    groups = d // (HEADS_PER_STEP * LANES)
    t = min(512, seq)
    nq = seq // t
    lam_init = 0.8 - 0.6 * math.exp(-0.3 * layer_idx)
    q_spec, k_spec, v_spec = _attn_specs(t, seq, nq, groups)
    stat = pltpu.VMEM((HEADS_PER_STEP, 2 * t, LANES), F32)
    return pl.pallas_call(
        functools.partial(_diff_attn_kernel, t=t, lam_init=lam_init),
        out_shape=jax.ShapeDtypeStruct((n, d), BF16),
        grid=(batch, groups, nq),
        in_specs=[
            pl.BlockSpec((4, HEAD_DIM), lambda b, h, i: (0, 0)),
            pl.BlockSpec((1, LANES), lambda b, h, i: (0, 0)),
            q_spec, k_spec, v_spec,
        ],
        out_specs=q_spec,
        scratch_shapes=[stat, stat, stat],
        compiler_params=_cparams(("arbitrary", "arbitrary", "arbitrary")),
        name="diff_attention",
    )(lam, subln_g.reshape(1, LANES), proj, proj, proj)


def _sb_attn_kernel(q_ref, k_ref, v_ref, tri_ref, o_ref, c_ref, acc_ref, *, t):
    i = pl.program_id(2)
    qs = [_stack_halves(q_ref[:, _head_lanes(hh)]) for hh in range(HEADS_PER_STEP)]

    def gates(hh, j, diag, guard):
        start = pl.multiple_of(j * t, t)
        z = _qk(qs[hh], k_ref[pl.ds(start, t), _head_lanes(hh)])
        log_beta = jnp.minimum(z, 0.0) - jnp.log(1.0 + jnp.exp(-jnp.abs(z)))
        log_rest = log_beta - z
        keep = None
        if diag:
            qpos, kpos = _block_positions(z.shape, t, i, j)
            keep = kpos < qpos
        if guard is not None:
            keep = lax.broadcasted_iota(I32, z.shape, 0) < guard
        if keep is not None:
            log_rest = jnp.where(keep, log_rest, 0.0)
        row_sum = jnp.sum(functools.reduce(jnp.add, _lane_chunks(log_rest)), axis=1, keepdims=True)
        return log_beta, log_rest, keep, row_sum

    def later_keys_sum(log_rest):
        hi, lo = _split_bf16(log_rest)
        tri = tri_ref[...]
        return jnp.dot(hi, tri, preferred_element_type=F32) + jnp.dot(lo, tri, preferred_element_type=F32)

    def weighted_values(hh, j, log_beta, within, keep, c):
        start = pl.multiple_of(j * t, t)
        arg = log_beta + within
        if c is not None:
            arg = jnp.concatenate([ch + c for ch in _lane_chunks(arg)], axis=1)
        a = jnp.exp(arg)
        if keep is not None:
            a = jnp.where(keep, a, 0.0)
        return jnp.dot(a.astype(BF16), v_ref[pl.ds(start, t), _head_lanes(hh)], preferred_element_type=F32)

    has_prev = jnp.where(i > 0, 2 * t, 0)
    j_prev = jnp.maximum(i - 1, 0)
    groups = range(HEADS_PER_STEP)
    g_diag = [gates(hh, i, True, None) for hh in groups]
    g_prev = [gates(hh, j_prev, False, has_prev) for hh in groups]
    w_diag = [later_keys_sum(g[1]) for g in g_diag]
    w_prev = [later_keys_sum(g[1]) for g in g_prev]
    c_max = None
    for hh in groups:
        c1 = jnp.broadcast_to(g_diag[hh][3], (2 * t, LANES))
        pv0 = weighted_values(hh, i, g_diag[hh][0], w_diag[hh], g_diag[hh][2], None)
        pv1 = weighted_values(hh, j_prev, g_prev[hh][0], w_prev[hh], g_prev[hh][2], c1)
        acc_ref[hh] = pv0 + pv1
        c2 = c1 + g_prev[hh][3]
        c_ref[hh] = c2
        c_max = jnp.max(c2) if c_max is None else jnp.maximum(c_max, jnp.max(c2))

    def cond(state):
        j, c_max = state
        return (j >= 0) & (c_max > SB_DEAD_LOG)

    def body(state):
        j, _ = state
        g_all = [gates(hh, j, False, None) for hh in groups]
        w_all = [later_keys_sum(g[1]) for g in g_all]
        c_max = None
        for hh in groups:
            c = c_ref[hh]
            acc_ref[hh] += weighted_values(hh, j, g_all[hh][0], w_all[hh], None, c)
            c_new = c + g_all[hh][3]
            c_ref[hh] = c_new
            c_max = jnp.max(c_new) if c_max is None else jnp.maximum(c_max, jnp.max(c_new))
        return j - 1, c_max

    lax.while_loop(cond, body, (i - 2, c_max))

    lane = lax.broadcasted_iota(I32, (t, LANES), 1)
    for hh in range(HEADS_PER_STEP):
        acc = acc_ref[hh]
        o_ref[:, _head_lanes(hh)] = jnp.where(lane < HEAD_DIM, acc[:t], acc[t:]).astype(BF16)


def _sb_attention(proj, batch, seq, d):
    n = batch * seq
    groups = d // (HEADS_PER_STEP * LANES)
    t = min(256, seq)
    nq = seq // t
    idx = jnp.arange(t)
    tri = (idx[:, None] > idx[None, :]).astype(BF16)
    q_spec, k_spec, v_spec = _attn_specs(t, seq, nq, groups)
    stat = pltpu.VMEM((HEADS_PER_STEP, 2 * t, LANES), F32)
    return pl.pallas_call(
        functools.partial(_sb_attn_kernel, t=t),
        out_shape=jax.ShapeDtypeStruct((n, d), BF16),
        grid=(batch, groups, nq),
        in_specs=[q_spec, k_spec, v_spec, pl.BlockSpec((t, t), lambda b, h, i: (0, 0))],
        out_specs=q_spec,
        scratch_shapes=[stat, stat],
        compiler_params=_cparams(("arbitrary", "arbitrary", "arbitrary")),
        name="sb_attention",
    )(proj, proj, proj, tri)


def _first_index_of(mask, lane, sentinel):
    return jnp.min(jnp.where(mask, lane, float(sentinel)), axis=1, keepdims=True)


def _top2(v, lane):
    neg = -jnp.inf
    m1 = jnp.max(v, axis=1, keepdims=True)
    i1 = _first_index_of(v == m1, lane, N_EXPERTS)
    v2 = jnp.where(lane == i1, neg, v)
    m2 = jnp.max(v2, axis=1, keepdims=True)
    i2 = _first_index_of(v2 == m2, lane, N_EXPERTS)
    return m1, i1, m2, i2


def _route(scores, sel):
    lane_i = lax.broadcasted_iota(I32, sel.shape, 1)
    lane = lane_i.astype(F32)
    grp = lax.shift_right_logical(lane_i, int(math.log2(GROUP_SIZE))).astype(F32)
    neg = -jnp.inf
    best = None
    g_best = None
    for g in range(N_GROUPS):
        m1, _, m2, _ = _top2(jnp.where(grp == float(g), sel, neg), lane)
        score = m1 + m2
        if g == 0:
            best, g_best = score, jnp.zeros(score.shape, F32)
        else:
            better = score > best
            g_best = jnp.where(better, float(g), g_best)
            best = jnp.where(better, score, best)
    _, e1, _, e2 = _top2(jnp.where(grp == g_best, sel, neg), lane)
    hot1, hot2 = lane == e1, lane == e2
    s1 = jnp.sum(jnp.where(hot1, scores, 0.0), axis=1, keepdims=True)
    s2 = jnp.sum(jnp.where(hot2, scores, 0.0), axis=1, keepdims=True)
    tot = s1 + s2
    return s1 / tot, s2 / tot, e1, e2, hot1, hot2


def _outproj_route_kernel(o_ref, x_ref, wo_ref, ga_ref, g_ref, sc_ref, sh_ref, wr_ref, br_ref, tril_ref,
                          xo_ref, h_ref, meta_ref, meta_t_ref, cnt_ref, *, ts, n_sub):
    @pl.when(pl.program_id(0) == 0)
    def _():
        cnt_ref[...] = jnp.zeros(cnt_ref.shape, F32)

    subs = range(n_sub)
    rows = [pl.ds(s * ts, ts) for s in subs]
    mix = [jnp.dot(o_ref[rows[s], :], wo_ref[...], preferred_element_type=F32) for s in subs]
    hs = []
    for s in subs:
        xn = x_ref[rows[s], :] + ga_ref[...] * mix[s]
        xo_ref[rows[s], :] = xn
        h = _rms(xn, g_ref[...]) * (1.0 + sc_ref[...]) + sh_ref[...]
        _store_rows(h_ref, s * ts, h)
        hs.append(h)

    logits = [_dot3(hs[s], wr_ref[...]) for s in subs]
    routed = []
    for s in subs:
        e = jnp.exp(logits[s] - jnp.max(logits[s], axis=1, keepdims=True))
        scores = e / jnp.sum(e, axis=1, keepdims=True)
        routed.append(_route(scores, scores + br_ref[...]))

    tril = tril_ref[...]
    ones = [(jnp.where(r[4], 1.0, 0.0), jnp.where(r[5], 1.0, 0.0)) for r in routed]
    prefix = [(jnp.dot(tril, o1.astype(BF16), preferred_element_type=F32),
               jnp.dot(tril, o2.astype(BF16), preferred_element_type=F32)) for o1, o2 in ones]
    base = cnt_ref[...]
    ml = lax.broadcasted_iota(I32, (ts, LANES), 1)
    for s in subs:
        g1, g2, e1, e2, hot1, hot2 = routed[s]
        tot1 = jnp.sum(ones[s][0], axis=0, keepdims=True)
        tot2 = jnp.sum(ones[s][1], axis=0, keepdims=True)
        r1 = jnp.sum(jnp.where(hot1, base + prefix[s][0], 0.0), axis=1, keepdims=True)
        r2 = jnp.sum(jnp.where(hot2, base + tot1 + prefix[s][1], 0.0), axis=1, keepdims=True)
        base = base + tot1 + tot2
        meta = jnp.where(ml == 0, g1, 0.0)
        for k, col in enumerate((g2, e1, e2, r1, r2)):
            meta = jnp.where(ml == k + 1, col, meta)
        meta_ref[rows[s], :] = meta
        meta_t_ref[:, rows[s]] = meta.T[:meta_t_ref.shape[0]]
    cnt_ref[...] = base


def _outproj_route(o, x, wo_bf16, ga, g, sc, sh, w_router, b_router, seq):
    n, d = x.shape
    tm = min(512, seq)
    ts = min(128, tm)
    per_b = seq // tm
    row = lambda i: (i, 0)
    bat = lambda i: (i // per_b, 0, 0)
    const = lambda i: (0, 0)
    idx = jnp.arange(ts)
    tril = (idx[None, :] < idx[:, None]).astype(BF16)
    return pl.pallas_call(
        functools.partial(_outproj_route_kernel, ts=ts, n_sub=tm // ts),
        out_shape=(
            jax.ShapeDtypeStruct((n, d), F32),
            jax.ShapeDtypeStruct((n * ROW_TILE, LANES), F32),
            jax.ShapeDtypeStruct((n, LANES), F32),
            jax.ShapeDtypeStruct((8, n), F32),
            jax.ShapeDtypeStruct((1, N_EXPERTS), F32),
        ),
        grid=(n // tm,),
        in_specs=[
            pl.BlockSpec((tm, d), row),
            pl.BlockSpec((tm, d), row),
            pl.BlockSpec((d, d), const),
            pl.BlockSpec((None, 1, d), bat),
            pl.BlockSpec((1, d), const),
            pl.BlockSpec((None, 1, d), bat),
            pl.BlockSpec((None, 1, d), bat),
            pl.BlockSpec((d, N_EXPERTS), const),
            pl.BlockSpec((1, N_EXPERTS), const),
            pl.BlockSpec((ts, ts), const),
        ],
        out_specs=(
            pl.BlockSpec((tm, d), row),
            pl.BlockSpec((tm * ROW_TILE, LANES), row),
            pl.BlockSpec((tm, LANES), row),
            pl.BlockSpec((8, tm), lambda i: (0, i)),
            pl.BlockSpec((1, N_EXPERTS), const),
        ),
        compiler_params=_cparams(("arbitrary",)),
        name="outproj_route",
    )(o, x, wo_bf16, ga, g.reshape(1, d), sc, sh, w_router, b_router.reshape(1, N_EXPERTS), tril)


def _row_copy(src, src_row, dst, dst_row, sem, rows=1):
    n = rows * ROW_TILE
    return pltpu.make_async_copy(src.at[pl.ds(pl.multiple_of(src_row * ROW_TILE, ROW_TILE), n)],
                                 dst.at[pl.ds(pl.multiple_of(dst_row * ROW_TILE, ROW_TILE), n)], sem)


def _load_rows(ref, start_row, rows):
    return jnp.concatenate([ref[pl.ds(start_row * ROW_TILE + k, rows, stride=ROW_TILE), :]
                            for k in range(ROW_TILE)], axis=1)


def _store_rows(ref, start_row, value):
    for k in range(ROW_TILE):
        ref[pl.ds(start_row * ROW_TILE + k, value.shape[0], stride=ROW_TILE), :] = value[:, k * LANES:(k + 1) * LANES]


def _dispatch_kernel(tail_ref, dest_hbm, h_ref, rows_hbm, dest_smem, stage, zero_buf, idx_sem, row_sems, *, tc, n_blk):
    c = pl.program_id(0)
    last = pl.num_programs(0) - 1

    @pl.when(c == 0)
    def _():
        zero_buf[...] = jnp.zeros(zero_buf.shape, F32)

        def fill(row):
            cp = _row_copy(zero_buf, 0, rows_hbm, row, row_sems.at[0], rows=EXPERT_ROWS)
            cp.start()
            cp.wait()

        def fill_tail(e, carry):
            fill(tail_ref[e])
            return carry

        def fill_unused_block(b, carry):
            fill(b * EXPERT_ROWS)
            return carry

        lax.fori_loop(0, N_EXPERTS, fill_tail, 0)
        lax.fori_loop(tail_ref[N_EXPERTS], n_blk, fill_unused_block, 0)

    idx_copy = pltpu.make_async_copy(dest_hbm.at[pl.ds(c * 2 * tc, 2 * tc)], dest_smem, idx_sem)
    idx_copy.start()
    idx_copy.wait()

    def drain(slot):
        def body(tk, carry):
            _row_copy(stage.at[slot], 0, rows_hbm, 0, row_sems.at[slot]).wait()
            _row_copy(stage.at[slot], 0, rows_hbm, 0, row_sems.at[slot]).wait()
            return carry

        lax.fori_loop(0, tc, body, 0, unroll=8)

    for slot in range(2):
        @pl.when((c & 1) == slot)
        def _(slot=slot):
            stage[slot] = h_ref[...]

            def issue(tk, carry):
                _row_copy(stage.at[slot], tk, rows_hbm, dest_smem[tk], row_sems.at[slot]).start(priority=0)
                _row_copy(stage.at[slot], tk, rows_hbm, dest_smem[tc + tk], row_sems.at[slot]).start(priority=1)
                return carry

            lax.fori_loop(0, tc, issue, 0, unroll=8)

            @pl.when(c > 0)
            def _():
                drain(1 - slot)

            @pl.when(c == last)
            def _():
                drain(slot)


def _dispatch(tail, dest_flat, h_tiles, n_rows, tc):
    n = h_tiles.shape[0] // ROW_TILE
    any_spec = pl.BlockSpec(memory_space=pl.ANY)
    grid_spec = pltpu.PrefetchScalarGridSpec(
        num_scalar_prefetch=1,
        grid=(n // tc,),
        in_specs=[any_spec, pl.BlockSpec((tc * ROW_TILE, LANES), lambda i, tail: (i, 0))],
        out_specs=any_spec,
        scratch_shapes=[pltpu.SMEM((2 * tc,), I32), pltpu.VMEM((2, tc * ROW_TILE, LANES), F32),
                        pltpu.VMEM((EXPERT_ROWS * ROW_TILE, LANES), F32),
                        pltpu.SemaphoreType.DMA, pltpu.SemaphoreType.DMA((2,))],
    )
    return pl.pallas_call(
        functools.partial(_dispatch_kernel, tc=tc, n_blk=n_rows // EXPERT_ROWS),
        out_shape=jax.ShapeDtypeStruct((n_rows * ROW_TILE, LANES), F32),
        grid_spec=grid_spec,
        compiler_params=_cparams(("arbitrary",)),
        name="moe_dispatch",
    )(tail, dest_flat, h_tiles)


def _experts_kernel(blk_e_ref, n_used_ref, x_ref, wg_ref, wu_ref, wd_ref, y_ref, wg_s, wu_s, wd_s):
    i = pl.program_id(0)
    used = i < n_used_ref[0]
    prev = blk_e_ref[jnp.maximum(i - 1, 0)]
    new_expert = (i == 0) | (blk_e_ref[i] != prev)

    @pl.when(used & new_expert)
    def _():
        wg_s[...] = wg_ref[...].astype(BF16)
        wu_s[...] = wu_ref[...].astype(BF16)
        wd_s[...] = wd_ref[...].astype(BF16)

    @pl.when(used)
    def _():
        xb = _load_rows(x_ref, 0, EXPERT_ROWS).astype(BF16)
        gate = jnp.dot(xb, wg_s[...], preferred_element_type=F32)
        up = jnp.dot(xb, wu_s[...], preferred_element_type=F32)
        hid = gate / (1.0 + jnp.exp(-gate)) * up
        _store_rows(y_ref, 0, jnp.dot(hid.astype(BF16), wd_s[...], preferred_element_type=F32))

    @pl.when(jnp.logical_not(used))
    def _():
        y_ref[...] = jnp.zeros(y_ref.shape, F32)


def _experts(blk_e, n_used, x_tiles, w_gate, w_up, w_down, layer):
    d, f = w_gate.shape[-2:]
    n_blk = x_tiles.shape[0] // (EXPERT_ROWS * ROW_TILE)
    blk = (EXPERT_ROWS * ROW_TILE, LANES)
    last_used = lambda n_used_ref: jnp.maximum(n_used_ref[0] - 1, 0)
    x_map = lambda i, be, nu: (jnp.minimum(i, last_used(nu)), 0)
    w_map = lambda i, be, nu: (layer, be[jnp.minimum(i, last_used(nu))], 0, 0)
    grid_spec = pltpu.PrefetchScalarGridSpec(
        num_scalar_prefetch=2,
        grid=(n_blk,),
        in_specs=[
            pl.BlockSpec(blk, x_map),
            pl.BlockSpec((None, None, d, f), w_map),
            pl.BlockSpec((None, None, d, f), w_map),
            pl.BlockSpec((None, None, f, d), w_map),
        ],
        out_specs=pl.BlockSpec(blk, lambda i, be, nu: (i, 0)),
        scratch_shapes=[pltpu.VMEM((d, f), BF16), pltpu.VMEM((d, f), BF16), pltpu.VMEM((f, d), BF16)],
    )
    return pl.pallas_call(
        _experts_kernel,
        out_shape=jax.ShapeDtypeStruct(x_tiles.shape, F32),
        grid_spec=grid_spec,
        compiler_params=_cparams(("arbitrary",)),
        name="moe_experts",
    )(blk_e, n_used, x_tiles, w_gate, w_up, w_down)


def _combine_kernel(*refs, tc, final):
    if final:
        dest_hbm, y_hbm, x_ref, meta_ref, gm_ref, fg_ref, o_ref = refs[:7]
    else:
        dest_hbm, y_hbm, x_ref, meta_ref, gm_ref, o_ref = refs[:6]
    dest_a, dest_b, a0, a1, b0, b1, idx_sem, row_sems = refs[-8:]
    dest_smem = (dest_a, dest_b)
    bufs = ((a0, a1), (b0, b1))
    c = pl.program_id(0)
    last = pl.num_programs(0) - 1

    def fetch(step, slot):
        idx_copy = pltpu.make_async_copy(dest_hbm.at[pl.ds(step * 2 * tc, 2 * tc)], dest_smem[slot], idx_sem)
        idx_copy.start()
        idx_copy.wait()

        def issue(tk, carry):
            _row_copy(y_hbm, dest_smem[slot][tk], bufs[slot][0], tk, row_sems.at[slot]).start(priority=0)
            _row_copy(y_hbm, dest_smem[slot][tc + tk], bufs[slot][1], tk, row_sems.at[slot]).start(priority=1)
            return carry

        lax.fori_loop(0, tc, issue, 0, unroll=8)

    def drain(slot):
        def body(tk, carry):
            _row_copy(y_hbm, 0, bufs[slot][0], 0, row_sems.at[slot]).wait()
            _row_copy(y_hbm, 0, bufs[slot][1], 0, row_sems.at[slot]).wait()
            return carry

        lax.fori_loop(0, tc, body, 0, unroll=8)

    @pl.when(c == 0)
    def _():
        fetch(0, 0)

    for slot in range(2):
        @pl.when((c & 1) == slot)
        def _(slot=slot):
            @pl.when(c < last)
            def _():
                fetch(c + 1, 1 - slot)

            drain(slot)
            meta = meta_ref[...]
            y = meta[:, 0:1] * _load_rows(bufs[slot][0], 0, tc) + meta[:, 1:2] * _load_rows(bufs[slot][1], 0, tc)
            xn = x_ref[...] + gm_ref[...] * y
            if final:
                xn = _rms(xn, fg_ref[...])
            o_ref[...] = xn


def _combine(dest_flat, y_rows, x, meta, gm, final_g, seq, tc):
    n, d = x.shape
    per_b = seq // tc
    row = lambda i: (i, 0)
    any_spec = pl.BlockSpec(memory_space=pl.ANY)
    final = final_g is not None
    in_specs = [
        any_spec,
        any_spec,
        pl.BlockSpec((tc, d), row),
        pl.BlockSpec((tc, LANES), row),
        pl.BlockSpec((None, 1, d), lambda i: (i // per_b, 0, 0)),
    ]
    args = [dest_flat, y_rows, x, meta, gm]
    if final:
        in_specs.append(pl.BlockSpec((1, d), lambda i: (0, 0)))
        args.append(final_g.reshape(1, d))
    return pl.pallas_call(
        functools.partial(_combine_kernel, tc=tc, final=final),
        out_shape=jax.ShapeDtypeStruct((n, d), F32),
        grid=(n // tc,),
        in_specs=in_specs,
        out_specs=pl.BlockSpec((tc, d), row),
        scratch_shapes=[pltpu.SMEM((2 * tc,), I32)] * 2
        + [pltpu.VMEM((tc * ROW_TILE, LANES), F32)] * 4
        + [pltpu.SemaphoreType.DMA, pltpu.SemaphoreType.DMA((2,))],
        compiler_params=_cparams(("arbitrary",)),
        name="moe_combine_final" if final else "moe_combine",
    )(*args)


def _moe(h_tiles, meta, meta_t, counts, x, gm, w_gate, w_up, w_down, layer, final_g, seq):
    n, d = x.shape
    assert d == ROW_TILE * LANES
    tc = min(512, seq)
    n_rows = n * TOP_K + N_EXPERTS * EXPERT_ROWS
    n_blk = n_rows // EXPERT_ROWS
    cnt = counts[0].astype(I32)
    padded = ((cnt + EXPERT_ROWS - 1) // EXPERT_ROWS) * EXPERT_ROWS
    seg_end = jnp.cumsum(padded)
    seg_start = seg_end - padded
    n_used = (seg_end[-1:] // EXPERT_ROWS).astype(I32)
    blk_start = jnp.arange(n_blk, dtype=I32) * EXPERT_ROWS
    blk_e = jnp.minimum(jnp.sum(seg_end[None, :] <= blk_start[:, None], axis=1), N_EXPERTS - 1).astype(I32)
    expert = meta_t[2:4].astype(I32)
    rank = meta_t[4:6].astype(I32)
    hot = expert[:, :, None] == jnp.arange(N_EXPERTS, dtype=I32)
    dest = rank + jnp.sum(jnp.where(hot, seg_start, 0), axis=-1)
    dest_flat = dest.reshape(TOP_K, n // tc, tc).transpose(1, 0, 2).reshape(-1)

    tail = jnp.concatenate([jnp.minimum(seg_start + cnt, n_rows - EXPERT_ROWS), n_used]).astype(I32)

    x_rows = _dispatch(tail, dest_flat, h_tiles, n_rows, tc)
    y_rows = _experts(blk_e, n_used, x_rows, w_gate, w_up, w_down, layer)
    return _combine(dest_flat, y_rows, x, meta, gm, final_g, seq, tc)


def kernel(x, c, positions, w_ada, b_ada, norm_g, w_in, w_out, diff_lambda, diff_subln_g, w_router, b_router,
           w_gate, w_up, w_down, final_norm_g):
    batch, seq, d = x.shape
    depth = w_ada.shape[0]
    n = batch * seq
    mod = _modulation(c, w_ada, b_ada)
    xf = x.reshape(n, d)
    for i in range(depth):
        sh_a, sc_a, g_a, sh_m, sc_m, g_m = [m.reshape(batch, 1, d) for m in jnp.split(mod[i], 6, axis=-1)]
        diff = i % 2 == 0
        proj = _norm_proj(xf, norm_g[i, 0], sc_a, sh_a, w_in[i].astype(BF16), positions, diff, seq)
        if diff:
            o = _diff_attention(proj, diff_lambda[i // 2], diff_subln_g[i // 2], batch, seq, d, i)
        else:
            o = _sb_attention(proj, batch, seq, d)
        xf, h, meta, meta_t, counts = _outproj_route(o, xf, w_out[i].astype(BF16), g_a, norm_g[i, 1], sc_m, sh_m,
                                                     w_router, b_router, seq)
        final_g = final_norm_g if i == depth - 1 else None
        xf = _moe(h, meta, meta_t, counts, xf, g_m, w_gate, w_up, w_down, i, final_g, seq)
    return xf.reshape(batch, seq, d)
```

```python
import functools
import math

import jax
import jax.numpy as jnp
from jax import lax
from jax.experimental import pallas as pl
from jax.experimental.pallas import tpu as pltpu

F32 = jnp.float32
BF16 = jnp.bfloat16
I32 = jnp.int32

LANES = 128
HEAD_DIM = 64
ROPE_DIMS = HEAD_DIM // 4
ROPE_THETA = 500000.0
N_EXPERTS = 32
N_GROUPS = 4
GROUP_SIZE = N_EXPERTS // N_GROUPS
TOP_K = 2
NORM_EPS = 1e-6
EXPERT_ROWS = 512
EXPERT_SUB_ROWS = 256
HEADS_PER_STEP = 2
ROW_TILE = 8
SB_DEAD_LOG = -110.0
MASK_VALUE = -1e30
VMEM_LIMIT = 56 * 1024 * 1024


def _cparams(semantics, vmem=VMEM_LIMIT):
    return pltpu.CompilerParams(dimension_semantics=semantics, vmem_limit_bytes=vmem)


def _split_bf16(a):
    hi = a.astype(BF16)
    lo = (a - hi.astype(F32)).astype(BF16)
    return hi, lo


def _dot3(a, b):
    ah, al = _split_bf16(a)
    bh, bl = _split_bf16(b)
    d = functools.partial(jnp.dot, preferred_element_type=F32)
    return d(ah, bh) + d(ah, bl) + d(al, bh)


def _rms(x, g):
    ms = jnp.mean(x * x, axis=-1, keepdims=True)
    return x * lax.rsqrt(ms + NORM_EPS) * g


def _mod_kernel(c_ref, w_ref, b_ref, o_ref):
    c = c_ref[...]
    c_act = c / (1.0 + jnp.exp(-c))
    o_ref[...] = _dot3(c_act, w_ref[...]) + b_ref[...]


def _modulation(c, w_ada, b_ada):
    depth, d, six_d = w_ada.shape
    b = c.shape[0]
    rows = 16
    c_pad = jnp.zeros((rows, d), F32).at[:b].set(c)
    tn = 1536
    out = pl.pallas_call(
        _mod_kernel,
        out_shape=jax.ShapeDtypeStruct((depth, rows, six_d), F32),
        grid=(depth, six_d // tn),
        in_specs=[
            pl.BlockSpec((rows, d), lambda l, j: (0, 0)),
            pl.BlockSpec((None, d, tn), lambda l, j: (l, 0, j)),
            pl.BlockSpec((None, 1, tn), lambda l, j: (l, 0, j)),
        ],
        out_specs=pl.BlockSpec((None, rows, tn), lambda l, j: (l, 0, j)),
        compiler_params=_cparams(("arbitrary", "arbitrary")),
        name="adaln_mod",
    )(c_pad, w_ada, b_ada.reshape(depth, 1, six_d))
    return out[:, :b]


def _proj_kernel(*refs, d, rope, q_scale):
    if rope:
        x_ref, g_ref, sc_ref, sh_ref, w_ref, pos_ref, pat_ref, o_ref = refs
    else:
        x_ref, g_ref, sc_ref, sh_ref, w_ref, o_ref = refs
    h = _rms(x_ref[...], g_ref[...]) * (1.0 + sc_ref[...]) + sh_ref[...]
    p = jnp.dot(h.astype(BF16), w_ref[...], preferred_element_type=F32)
    n_qk = 2 * d // LANES
    if rope:
        ang = pos_ref[...].astype(F32) * pat_ref[...]
        cs, sn = jnp.cos(ang), jnp.sin(ang)
        lane = lax.broadcasted_iota(I32, ang.shape, 1) & (HEAD_DIM - 1)
        half = ROPE_DIMS // 2
        s_lo = jnp.where(lane < half, -sn, 0.0)
        s_hi = jnp.where((lane >= half) & (lane < ROPE_DIMS), sn, 0.0)
    for j in range(n_qk):
        blk = p[:, j * LANES:(j + 1) * LANES]
        if rope:
            blk = (blk * cs + pltpu.roll(blk, half, 1) * s_hi
                   + pltpu.roll(blk, LANES - half, 1) * s_lo)
        if j < d // LANES:
            blk = blk * q_scale
        o_ref[:, j * LANES:(j + 1) * LANES] = blk.astype(BF16)
    o_ref[:, 2 * d:] = p[:, 2 * d:].astype(BF16)


def _norm_proj(x, g, sc, sh, w_bf16, positions, rope, seq):
    n, d = x.shape
    tm = min(512, seq)
    per_b = seq // tm
    row = lambda i: (i, 0)
    bat = lambda i: (i // per_b, 0, 0)
    in_specs = [
        pl.BlockSpec((tm, d), row),
        pl.BlockSpec((1, d), lambda i: (0, 0)),
        pl.BlockSpec((None, 1, d), bat),
        pl.BlockSpec((None, 1, d), bat),
        pl.BlockSpec((d, 3 * d), lambda i: (0, 0)),
    ]
    args = [x, g.reshape(1, d), sc, sh, w_bf16]
    if rope:
        half = ROPE_DIMS // 2
        inv_freq = jnp.power(jnp.float32(ROPE_THETA), -jnp.arange(half, dtype=F32) * (2.0 / ROPE_DIMS))
        lane = jnp.arange(LANES) % HEAD_DIM
        pat = jnp.where(lane < ROPE_DIMS, inv_freq[lane % half], 0.0).astype(F32).reshape(1, LANES)
        in_specs += [pl.BlockSpec((tm, 1), row), pl.BlockSpec((1, LANES), lambda i: (0, 0))]
        args += [positions.reshape(n, 1), pat]
    q_scale = HEAD_DIM ** -0.5 * (math.log2(math.e) if rope else 1.0)
    return pl.pallas_call(
        functools.partial(_proj_kernel, d=d, rope=rope, q_scale=q_scale),
        out_shape=jax.ShapeDtypeStruct((n, 3 * d), BF16),
        grid=(n // tm,),
        in_specs=in_specs,
        out_specs=pl.BlockSpec((tm, 3 * d), row),
        compiler_params=_cparams(("arbitrary",)),
        name="norm_proj_rope" if rope else "norm_proj",
    )(*args)


def _stack_halves(q):
    lane = lax.broadcasted_iota(I32, q.shape, 1)
    zero = jnp.zeros_like(q)
    return jnp.concatenate([jnp.where(lane < HEAD_DIM, q, zero), jnp.where(lane >= HEAD_DIM, q, zero)], axis=0)


def _qk(qs, kb):
    return lax.dot_general(qs, kb, (((1,), (1,)), ((), ())), preferred_element_type=F32)


def _head_lanes(hh):
    return slice(hh * LANES, (hh + 1) * LANES)


def _lane_chunks(x):
    return [x[:, c * LANES:(c + 1) * LANES] for c in range(x.shape[1] // LANES)]


def _block_positions(shape, t, i, j):
    r = lax.broadcasted_iota(I32, shape, 0)
    qpos = i * t + jnp.where(r >= t, r - t, r)
    kpos = j * t + lax.broadcasted_iota(I32, shape, 1)
    return qpos, kpos


def _diff_attn_kernel(lam_ref, g_ref, q_ref, k_ref, v_ref, o_ref, m_ref, l_ref, acc_ref, *, t, lam_init):
    i = pl.program_id(2)
    qs = [_stack_halves(q_ref[:, _head_lanes(hh)]) for hh in range(HEADS_PER_STEP)]
    m_ref[...] = jnp.full(m_ref.shape, MASK_VALUE, F32)
    l_ref[...] = jnp.zeros(l_ref.shape, F32)
    acc_ref[...] = jnp.zeros(acc_ref.shape, F32)

    def scores(hh, j):
        start = pl.multiple_of(j * t, t)
        return _qk(qs[hh], k_ref[pl.ds(start, t), _head_lanes(hh)])

    def softmax_pv(hh, s, j, masked):
        start = pl.multiple_of(j * t, t)
        if masked:
            qpos, kpos = _block_positions(s.shape, t, i, j)
            s = jnp.where(kpos <= qpos, s, MASK_VALUE)
        chunks = _lane_chunks(s)
        m_old = m_ref[hh]
        m_new = jnp.maximum(m_old, jnp.max(functools.reduce(jnp.maximum, chunks), axis=1, keepdims=True))
        alpha = jnp.exp2(m_old - m_new)
        ps = [jnp.exp2(ch - m_new) for ch in chunks]
        l_ref[hh] = l_ref[hh] * alpha + functools.reduce(jnp.add, ps)
        p = jnp.concatenate([x.astype(BF16) for x in ps], axis=1)
        pv = jnp.dot(p, v_ref[pl.ds(start, t), _head_lanes(hh)], preferred_element_type=F32)
        acc_ref[hh] = acc_ref[hh] * alpha + pv
        m_ref[hh] = m_new

    def block(j, masked):
        s_all = [scores(hh, j) for hh in range(HEADS_PER_STEP)]
        for hh in range(HEADS_PER_STEP):
            softmax_pv(hh, s_all[hh], j, masked)

    def block_pair(j0, j1, masked1):
        s0 = [scores(hh, j0) for hh in range(HEADS_PER_STEP)]
        s1 = []
        for hh in range(HEADS_PER_STEP):
            softmax_pv(hh, s0[hh], j0, False)
            s1.append(scores(hh, j1))
        for hh in range(HEADS_PER_STEP):
            softmax_pv(hh, s1[hh], j1, masked1)

    def off_diag_pair(p, carry):
        block_pair(2 * p, 2 * p + 1, False)
        return carry

    lax.fori_loop(0, lax.shift_right_logical(i, 1), off_diag_pair, 0)

    @pl.when((i & 1) == 1)
    def _():
        block_pair(i - 1, i, True)

    @pl.when((i & 1) == 0)
    def _():
        block(i, True)

    lam = lam_ref[...]
    lam_full = (jnp.exp(jnp.sum(lam[0:1] * lam[1:2], axis=1, keepdims=True))
                - jnp.exp(jnp.sum(lam[2:3] * lam[3:4], axis=1, keepdims=True)) + lam_init)
    for hh in range(HEADS_PER_STEP):
        o_all = acc_ref[hh] / jnp.sum(l_ref[hh], axis=1, keepdims=True)
        o = o_all[:t] - lam_full * o_all[t:]
        o_ref[:, _head_lanes(hh)] = (_rms(o, g_ref[...]) * (1.0 - lam_init)).astype(BF16)


def _attn_specs(t, seq, nq, groups):
    w = HEADS_PER_STEP * LANES
    q_spec = pl.BlockSpec((t, w), lambda b, h, i: (b * nq + i, h))
    k_spec = pl.BlockSpec((seq, w), lambda b, h, i: (b, groups + h))
    v_spec = pl.BlockSpec((seq, w), lambda b, h, i: (b, 2 * groups + h))
    return q_spec, k_spec, v_spec


def _diff_attention(proj, lam, subln_g, batch, seq, d, layer_idx):
    n = batch * seq
    groups = d // (HEADS_PER_STEP * LANES)
    t = min(512, seq)
    nq = seq // t
    lam_init = 0.8 - 0.6 * math.exp(-0.3 * layer_idx)
    q_spec, k_spec, v_spec = _attn_specs(t, seq, nq, groups)
    stat = pltpu.VMEM((HEADS_PER_STEP, 2 * t, LANES), F32)
    return pl.pallas_call(
        functools.partial(_diff_attn_kernel, t=t, lam_init=lam_init),
        out_shape=jax.ShapeDtypeStruct((n, d), BF16),
        grid=(batch, groups, nq),
        in_specs=[
            pl.BlockSpec((4, HEAD_DIM), lambda b, h, i: (0, 0)),
            pl.BlockSpec((1, LANES), lambda b, h, i: (0, 0)),
            q_spec, k_spec, v_spec,
        ],
        out_specs=q_spec,
        scratch_shapes=[stat, stat, stat],
        compiler_params=_cparams(("arbitrary", "arbitrary", "arbitrary")),
        name="diff_attention",
    )(lam, subln_g.reshape(1, LANES), proj, proj, proj)


def _sb_attn_kernel(q_ref, k_ref, v_ref, tri_ref, o_ref, c_ref, acc_ref, *, t):
    i = pl.program_id(2)
    qs = [_stack_halves(q_ref[:, _head_lanes(hh)]) for hh in range(HEADS_PER_STEP)]

    def gates(hh, j, diag, guard):
        start = pl.multiple_of(j * t, t)
        z = _qk(qs[hh], k_ref[pl.ds(start, t), _head_lanes(hh)])
        log_beta = jnp.minimum(z, 0.0) - jnp.log(1.0 + jnp.exp(-jnp.abs(z)))
        log_rest = log_beta - z
        keep = None
        if diag:
            qpos, kpos = _block_positions(z.shape, t, i, j)
            keep = kpos < qpos
        if guard is not None:
            keep = lax.broadcasted_iota(I32, z.shape, 0) < guard
        if keep is not None:
            log_rest = jnp.where(keep, log_rest, 0.0)
        row_sum = jnp.sum(functools.reduce(jnp.add, _lane_chunks(log_rest)), axis=1, keepdims=True)
        return log_beta, log_rest, keep, row_sum

    def later_keys_sum(log_rest):
        return jnp.dot(log_rest.astype(BF16), tri_ref[...], preferred_element_type=F32)

    def weighted_values(hh, j, log_beta, within, keep, c):
        start = pl.multiple_of(j * t, t)
        arg = log_beta + within
        if c is not None:
            arg = jnp.concatenate([ch + c for ch in _lane_chunks(arg)], axis=1)
        a = jnp.exp(arg)
        if keep is not None:
            a = jnp.where(keep, a, 0.0)
        return jnp.dot(a.astype(BF16), v_ref[pl.ds(start, t), _head_lanes(hh)], preferred_element_type=F32)

    has_prev = jnp.where(i > 0, 2 * t, 0)
    j_prev = jnp.maximum(i - 1, 0)
    groups = range(HEADS_PER_STEP)
    g_diag = [gates(hh, i, True, None) for hh in groups]
    g_prev = [gates(hh, j_prev, False, has_prev) for hh in groups]
    w_diag = [later_keys_sum(g[1]) for g in g_diag]
    w_prev = [later_keys_sum(g[1]) for g in g_prev]
    c_max = None
    for hh in groups:
        c1 = jnp.broadcast_to(g_diag[hh][3], (2 * t, LANES))
        pv0 = weighted_values(hh, i, g_diag[hh][0], w_diag[hh], g_diag[hh][2], None)
        pv1 = weighted_values(hh, j_prev, g_prev[hh][0], w_prev[hh], g_prev[hh][2], c1)
        acc_ref[hh] = pv0 + pv1
        c2 = c1 + g_prev[hh][3]
        c_ref[hh] = c2
        c_max = jnp.max(c2) if c_max is None else jnp.maximum(c_max, jnp.max(c2))

    def cond(state):
        j, c_max = state
        return (j >= 0) & (c_max > SB_DEAD_LOG)

    def body(state):
        j, _ = state
        g_all = [gates(hh, j, False, None) for hh in groups]
        w_all = [later_keys_sum(g[1]) for g in g_all]
        c_max = None
        for hh in groups:
            c = c_ref[hh]
            acc_ref[hh] += weighted_values(hh, j, g_all[hh][0], w_all[hh], None, c)
            c_new = c + g_all[hh][3]
            c_ref[hh] = c_new
            c_max = jnp.max(c_new) if c_max is None else jnp.maximum(c_max, jnp.max(c_new))
        return j - 1, c_max

    lax.while_loop(cond, body, (i - 2, c_max))

    lane = lax.broadcasted_iota(I32, (t, LANES), 1)
    for hh in range(HEADS_PER_STEP):
        acc = acc_ref[hh]
        o_ref[:, _head_lanes(hh)] = jnp.where(lane < HEAD_DIM, acc[:t], acc[t:]).astype(BF16)


def _sb_attention(proj, batch, seq, d):
    n = batch * seq
    groups = d // (HEADS_PER_STEP * LANES)
    t = min(256, seq)
    nq = seq // t
    idx = jnp.arange(t)
    tri = (idx[:, None] > idx[None, :]).astype(BF16)
    q_spec, k_spec, v_spec = _attn_specs(t, seq, nq, groups)
    stat = pltpu.VMEM((HEADS_PER_STEP, 2 * t, LANES), F32)
    return pl.pallas_call(
        functools.partial(_sb_attn_kernel, t=t),
        out_shape=jax.ShapeDtypeStruct((n, d), BF16),
        grid=(batch, groups, nq),
        in_specs=[q_spec, k_spec, v_spec, pl.BlockSpec((t, t), lambda b, h, i: (0, 0))],
        out_specs=q_spec,
        scratch_shapes=[stat, stat],
        compiler_params=_cparams(("arbitrary", "arbitrary", "arbitrary")),
        name="sb_attention",
    )(proj, proj, proj, tri)


def _first_index_of(mask, lane, sentinel):
    return jnp.min(jnp.where(mask, lane, float(sentinel)), axis=1, keepdims=True)


def _top2(v, lane):
    neg = -jnp.inf
    m1 = jnp.max(v, axis=1, keepdims=True)
    i1 = _first_index_of(v == m1, lane, N_EXPERTS)
    v2 = jnp.where(lane == i1, neg, v)
    m2 = jnp.max(v2, axis=1, keepdims=True)
    i2 = _first_index_of(v2 == m2, lane, N_EXPERTS)
    return m1, i1, m2, i2


def _route(scores, sel):
    lane_i = lax.broadcasted_iota(I32, sel.shape, 1)
    lane = lane_i.astype(F32)
    grp = lax.shift_right_logical(lane_i, int(math.log2(GROUP_SIZE))).astype(F32)
    neg = -jnp.inf
    best = None
    g_best = None
    for g in range(N_GROUPS):
        m1, _, m2, _ = _top2(jnp.where(grp == float(g), sel, neg), lane)
        score = m1 + m2
        if g == 0:
            best, g_best = score, jnp.zeros(score.shape, F32)
        else:
            better = score > best
            g_best = jnp.where(better, float(g), g_best)
            best = jnp.where(better, score, best)
    _, e1, _, e2 = _top2(jnp.where(grp == g_best, sel, neg), lane)
    hot1, hot2 = lane == e1, lane == e2
    s1 = jnp.sum(jnp.where(hot1, scores, 0.0), axis=1, keepdims=True)
    s2 = jnp.sum(jnp.where(hot2, scores, 0.0), axis=1, keepdims=True)
    tot = s1 + s2
    return s1 / tot, s2 / tot, e1, e2, hot1, hot2


def _outproj_route_kernel(o_ref, x_ref, wo_ref, ga_ref, g_ref, sc_ref, sh_ref, wr_ref, br_ref, tril_ref,
                          xo_ref, h_ref, meta_ref, meta_t_ref, cnt_ref, *, ts, n_sub):
    @pl.when(pl.program_id(0) == 0)
    def _():
        cnt_ref[...] = jnp.zeros(cnt_ref.shape, F32)

    subs = range(n_sub)
    rows = [pl.ds(s * ts, ts) for s in subs]
    mix = [jnp.dot(o_ref[rows[s], :], wo_ref[...], preferred_element_type=F32) for s in subs]
    hs = []
    for s in subs:
        xn = x_ref[rows[s], :] + ga_ref[...] * mix[s]
        xo_ref[rows[s], :] = xn
        h = _rms(xn, g_ref[...]) * (1.0 + sc_ref[...]) + sh_ref[...]
        _store_rows(h_ref, s * ts, h)
        hs.append(h)

    logits = [_dot3(hs[s], wr_ref[...]) for s in subs]
    routed = []
    for s in subs:
        e = jnp.exp(logits[s] - jnp.max(logits[s], axis=1, keepdims=True))
        scores = e / jnp.sum(e, axis=1, keepdims=True)
        routed.append(_route(scores, scores + br_ref[...]))

    tril = tril_ref[...]
    ones = [(jnp.where(r[4], 1.0, 0.0), jnp.where(r[5], 1.0, 0.0)) for r in routed]
    prefix = [(jnp.dot(tril, o1.astype(BF16), preferred_element_type=F32),
               jnp.dot(tril, o2.astype(BF16), preferred_element_type=F32)) for o1, o2 in ones]
    base = cnt_ref[...]
    ml = lax.broadcasted_iota(I32, (ts, LANES), 1)
    for s in subs:
        g1, g2, e1, e2, hot1, hot2 = routed[s]
        tot1 = jnp.sum(ones[s][0], axis=0, keepdims=True)
        tot2 = jnp.sum(ones[s][1], axis=0, keepdims=True)
        r1 = jnp.sum(jnp.where(hot1, base + prefix[s][0], 0.0), axis=1, keepdims=True)
        r2 = jnp.sum(jnp.where(hot2, base + tot1 + prefix[s][1], 0.0), axis=1, keepdims=True)
        base = base + tot1 + tot2
        meta = jnp.where(ml == 0, g1, 0.0)
        for k, col in enumerate((g2, e1, e2, r1, r2)):
            meta = jnp.where(ml == k + 1, col, meta)
        meta_ref[rows[s], :] = meta
        meta_t_ref[:, rows[s]] = meta.T[:meta_t_ref.shape[0]]
    cnt_ref[...] = base


def _outproj_route(o, x, wo_bf16, ga, g, sc, sh, w_router, b_router, seq):
    n, d = x.shape
    tm = min(512, seq)
    ts = min(128, tm)
    per_b = seq // tm
    row = lambda i: (i, 0)
    bat = lambda i: (i // per_b, 0, 0)
    const = lambda i: (0, 0)
    idx = jnp.arange(ts)
    tril = (idx[None, :] < idx[:, None]).astype(BF16)
    return pl.pallas_call(
        functools.partial(_outproj_route_kernel, ts=ts, n_sub=tm // ts),
        out_shape=(
            jax.ShapeDtypeStruct((n, d), F32),
            jax.ShapeDtypeStruct((n * ROW_TILE, LANES), F32),
            jax.ShapeDtypeStruct((n, LANES), F32),
            jax.ShapeDtypeStruct((8, n), F32),
            jax.ShapeDtypeStruct((1, N_EXPERTS), F32),
        ),
        grid=(n // tm,),
        in_specs=[
            pl.BlockSpec((tm, d), row),
            pl.BlockSpec((tm, d), row),
            pl.BlockSpec((d, d), const),
            pl.BlockSpec((None, 1, d), bat),
            pl.BlockSpec((1, d), const),
            pl.BlockSpec((None, 1, d), bat),
            pl.BlockSpec((None, 1, d), bat),
            pl.BlockSpec((d, N_EXPERTS), const),
            pl.BlockSpec((1, N_EXPERTS), const),
            pl.BlockSpec((ts, ts), const),
        ],
        out_specs=(
            pl.BlockSpec((tm, d), row),
            pl.BlockSpec((tm * ROW_TILE, LANES), row),
            pl.BlockSpec((tm, LANES), row),
            pl.BlockSpec((8, tm), lambda i: (0, i)),
            pl.BlockSpec((1, N_EXPERTS), const),
        ),
        compiler_params=_cparams(("arbitrary",)),
        name="outproj_route",
    )(o, x, wo_bf16, ga, g.reshape(1, d), sc, sh, w_router, b_router.reshape(1, N_EXPERTS), tril)


def _row_copy(src, src_row, dst, dst_row, sem, rows=1):
    n = rows * ROW_TILE
    return pltpu.make_async_copy(src.at[pl.ds(pl.multiple_of(src_row * ROW_TILE, ROW_TILE), n)],
                                 dst.at[pl.ds(pl.multiple_of(dst_row * ROW_TILE, ROW_TILE), n)], sem)


def _load_rows(ref, start_row, rows):
    return jnp.concatenate([ref[pl.ds(start_row * ROW_TILE + k, rows, stride=ROW_TILE), :]
                            for k in range(ROW_TILE)], axis=1)


def _store_rows(ref, start_row, value):
    for k in range(ROW_TILE):
        ref[pl.ds(start_row * ROW_TILE + k, value.shape[0], stride=ROW_TILE), :] = value[:, k * LANES:(k + 1) * LANES]


def _dispatch_kernel(tail_ref, dest_hbm, h_ref, rows_hbm, dest_smem, stage, zero_buf, idx_sem, row_sems, *, tc, n_blk):
    c = pl.program_id(0)
    last = pl.num_programs(0) - 1

    @pl.when(c == 0)
    def _():
        zero_buf[...] = jnp.zeros(zero_buf.shape, F32)

        def fill(row):
            cp = _row_copy(zero_buf, 0, rows_hbm, row, row_sems.at[0], rows=EXPERT_ROWS)
            cp.start()
            cp.wait()

        def fill_tail(e, carry):
            fill(tail_ref[e])
            return carry

        def fill_unused_block(b, carry):
            fill(b * EXPERT_ROWS)
            return carry

        lax.fori_loop(0, N_EXPERTS, fill_tail, 0)
        lax.fori_loop(tail_ref[N_EXPERTS], n_blk, fill_unused_block, 0)

    idx_copy = pltpu.make_async_copy(dest_hbm.at[pl.ds(c * 2 * tc, 2 * tc)], dest_smem, idx_sem)
    idx_copy.start()
    idx_copy.wait()

    def drain(slot):
        def body(tk, carry):
            _row_copy(stage.at[slot], 0, rows_hbm, 0, row_sems.at[slot]).wait()
            _row_copy(stage.at[slot], 0, rows_hbm, 0, row_sems.at[slot]).wait()
            return carry

        lax.fori_loop(0, tc, body, 0, unroll=8)

    for slot in range(2):
        @pl.when((c & 1) == slot)
        def _(slot=slot):
            stage[slot] = h_ref[...]

            def issue(tk, carry):
                _row_copy(stage.at[slot], tk, rows_hbm, dest_smem[tk], row_sems.at[slot]).start(priority=0)
                _row_copy(stage.at[slot], tk, rows_hbm, dest_smem[tc + tk], row_sems.at[slot]).start(priority=1)
                return carry

            lax.fori_loop(0, tc, issue, 0, unroll=8)

            @pl.when(c > 0)
            def _():
                drain(1 - slot)

            @pl.when(c == last)
            def _():
                drain(slot)


def _dispatch(tail, dest_flat, h_tiles, n_rows, tc):
    n = h_tiles.shape[0] // ROW_TILE
    any_spec = pl.BlockSpec(memory_space=pl.ANY)
    grid_spec = pltpu.PrefetchScalarGridSpec(
        num_scalar_prefetch=1,
        grid=(n // tc,),
        in_specs=[any_spec, pl.BlockSpec((tc * ROW_TILE, LANES), lambda i, tail: (i, 0))],
        out_specs=any_spec,
        scratch_shapes=[pltpu.SMEM((2 * tc,), I32), pltpu.VMEM((2, tc * ROW_TILE, LANES), F32),
                        pltpu.VMEM((EXPERT_ROWS * ROW_TILE, LANES), F32),
                        pltpu.SemaphoreType.DMA, pltpu.SemaphoreType.DMA((2,))],
    )
    return pl.pallas_call(
        functools.partial(_dispatch_kernel, tc=tc, n_blk=n_rows // EXPERT_ROWS),
        out_shape=jax.ShapeDtypeStruct((n_rows * ROW_TILE, LANES), F32),
        grid_spec=grid_spec,
        compiler_params=_cparams(("arbitrary",)),
        name="moe_dispatch",
    )(tail, dest_flat, h_tiles)


def _experts_kernel(blk_e_ref, n_used_ref, x_ref, wg_ref, wu_ref, wd_ref, y_ref, wg_s, wu_s, wd_s):
    i = pl.program_id(0)
    used = i < n_used_ref[0]
    prev = blk_e_ref[jnp.maximum(i - 1, 0)]
    new_expert = (i == 0) | (blk_e_ref[i] != prev)

    @pl.when(used & new_expert)
    def _():
        wg_s[...] = wg_ref[...].astype(BF16)
        wu_s[...] = wu_ref[...].astype(BF16)
        wd_s[...] = wd_ref[...].astype(BF16)

    @pl.when(used)
    def _():
        groups = range(EXPERT_ROWS // EXPERT_SUB_ROWS)
        dot = functools.partial(jnp.dot, preferred_element_type=F32)
        xb = [_load_rows(x_ref, s * EXPERT_SUB_ROWS, EXPERT_SUB_ROWS).astype(BF16) for s in groups]
        gate = [dot(xb[s], wg_s[...]) for s in groups]
        up = [dot(xb[s], wu_s[...]) for s in groups]
        hid = [(gate[s] / (1.0 + jnp.exp(-gate[s])) * up[s]).astype(BF16) for s in groups]
        for s in groups:
            _store_rows(y_ref, s * EXPERT_SUB_ROWS, dot(hid[s], wd_s[...]))

    @pl.when(jnp.logical_not(used))
    def _():
        y_ref[...] = jnp.zeros(y_ref.shape, F32)


def _experts(blk_e, n_used, x_tiles, w_gate, w_up, w_down, layer):
    d, f = w_gate.shape[-2:]
    n_blk = x_tiles.shape[0] // (EXPERT_ROWS * ROW_TILE)
    blk = (EXPERT_ROWS * ROW_TILE, LANES)
    last_used = lambda n_used_ref: jnp.maximum(n_used_ref[0] - 1, 0)
    x_map = lambda i, be, nu: (jnp.minimum(i, last_used(nu)), 0)
    w_map = lambda i, be, nu: (layer, be[jnp.minimum(i, last_used(nu))], 0, 0)
    grid_spec = pltpu.PrefetchScalarGridSpec(
        num_scalar_prefetch=2,
        grid=(n_blk,),
        in_specs=[
            pl.BlockSpec(blk, x_map),
            pl.BlockSpec((None, None, d, f), w_map),
            pl.BlockSpec((None, None, d, f), w_map),
            pl.BlockSpec((None, None, f, d), w_map),
        ],
        out_specs=pl.BlockSpec(blk, lambda i, be, nu: (i, 0)),
        scratch_shapes=[pltpu.VMEM((d, f), BF16), pltpu.VMEM((d, f), BF16), pltpu.VMEM((f, d), BF16)],
    )
    return pl.pallas_call(
        _experts_kernel,
        out_shape=jax.ShapeDtypeStruct(x_tiles.shape, F32),
        grid_spec=grid_spec,
        compiler_params=_cparams(("arbitrary",)),
        name="moe_experts",
    )(blk_e, n_used, x_tiles, w_gate, w_up, w_down)


def _combine_kernel(*refs, tc, final):
    if final:
        dest_hbm, y_hbm, x_ref, meta_ref, gm_ref, fg_ref, o_ref = refs[:7]
    else:
        dest_hbm, y_hbm, x_ref, meta_ref, gm_ref, o_ref = refs[:6]
    dest_a, dest_b, a0, a1, b0, b1, idx_sem, row_sems = refs[-8:]
    dest_smem = (dest_a, dest_b)
    bufs = ((a0, a1), (b0, b1))
    c = pl.program_id(0)
    last = pl.num_programs(0) - 1

    def fetch(step, slot):
        idx_copy = pltpu.make_async_copy(dest_hbm.at[pl.ds(step * 2 * tc, 2 * tc)], dest_smem[slot], idx_sem)
        idx_copy.start()
        idx_copy.wait()

        def issue(tk, carry):
            _row_copy(y_hbm, dest_smem[slot][tk], bufs[slot][0], tk, row_sems.at[slot]).start(priority=0)
            _row_copy(y_hbm, dest_smem[slot][tc + tk], bufs[slot][1], tk, row_sems.at[slot]).start(priority=1)
            return carry

        lax.fori_loop(0, tc, issue, 0, unroll=8)

    def drain(slot):
        def body(tk, carry):
            _row_copy(y_hbm, 0, bufs[slot][0], 0, row_sems.at[slot]).wait()
            _row_copy(y_hbm, 0, bufs[slot][1], 0, row_sems.at[slot]).wait()
            return carry

        lax.fori_loop(0, tc, body, 0, unroll=8)

    @pl.when(c == 0)
    def _():
        fetch(0, 0)

    for slot in range(2):
        @pl.when((c & 1) == slot)
        def _(slot=slot):
            @pl.when(c < last)
            def _():
                fetch(c + 1, 1 - slot)

            drain(slot)
            meta = meta_ref[...]
            y = meta[:, 0:1] * _load_rows(bufs[slot][0], 0, tc) + meta[:, 1:2] * _load_rows(bufs[slot][1], 0, tc)
            xn = x_ref[...] + gm_ref[...] * y
            if final:
                xn = _rms(xn, fg_ref[...])
            o_ref[...] = xn


def _combine(dest_flat, y_rows, x, meta, gm, final_g, seq, tc):
    n, d = x.shape
    per_b = seq // tc
    row = lambda i: (i, 0)
    any_spec = pl.BlockSpec(memory_space=pl.ANY)
    final = final_g is not None
    in_specs = [
        any_spec,
        any_spec,
        pl.BlockSpec((tc, d), row),
        pl.BlockSpec((tc, LANES), row),
        pl.BlockSpec((None, 1, d), lambda i: (i // per_b, 0, 0)),
    ]
    args = [dest_flat, y_rows, x, meta, gm]
    if final:
        in_specs.append(pl.BlockSpec((1, d), lambda i: (0, 0)))
        args.append(final_g.reshape(1, d))
    return pl.pallas_call(
        functools.partial(_combine_kernel, tc=tc, final=final),
        out_shape=jax.ShapeDtypeStruct((n, d), F32),
        grid=(n // tc,),
        in_specs=in_specs,
        out_specs=pl.BlockSpec((tc, d), row),
        scratch_shapes=[pltpu.SMEM((2 * tc,), I32)] * 2
        + [pltpu.VMEM((tc * ROW_TILE, LANES), F32)] * 4
        + [pltpu.SemaphoreType.DMA, pltpu.SemaphoreType.DMA((2,))],
        compiler_params=_cparams(("arbitrary",)),
        name="moe_combine_final" if final else "moe_combine",
    )(*args)


def _moe(h_tiles, meta, meta_t, counts, x, gm, w_gate, w_up, w_down, layer, final_g, seq):
    n, d = x.shape
    assert d == ROW_TILE * LANES
    tc = min(512, seq)
    n_rows = n * TOP_K + N_EXPERTS * EXPERT_ROWS
    n_blk = n_rows // EXPERT_ROWS
    cnt = counts[0].astype(I32)
    padded = ((cnt + EXPERT_ROWS - 1) // EXPERT_ROWS) * EXPERT_ROWS
    seg_end = jnp.cumsum(padded)
    seg_start = seg_end - padded
    n_used = (seg_end[-1:] // EXPERT_ROWS).astype(I32)
    blk_start = jnp.arange(n_blk, dtype=I32) * EXPERT_ROWS
    blk_e = jnp.minimum(jnp.sum(seg_end[None, :] <= blk_start[:, None], axis=1), N_EXPERTS - 1).astype(I32)
    expert = meta_t[2:4].astype(I32)
    rank = meta_t[4:6].astype(I32)
    hot = expert[:, :, None] == jnp.arange(N_EXPERTS, dtype=I32)
    dest = rank + jnp.sum(jnp.where(hot, seg_start, 0), axis=-1)
    dest_flat = dest.reshape(TOP_K, n // tc, tc).transpose(1, 0, 2).reshape(-1)

    tail = jnp.concatenate([jnp.minimum(seg_start + cnt, n_rows - EXPERT_ROWS), n_used]).astype(I32)

    x_rows = _dispatch(tail, dest_flat, h_tiles, n_rows, tc)
    y_rows = _experts(blk_e, n_used, x_rows, w_gate, w_up, w_down, layer)
    return _combine(dest_flat, y_rows, x, meta, gm, final_g, seq, tc)


def kernel(x, c, positions, w_ada, b_ada, norm_g, w_in, w_out, diff_lambda, diff_subln_g, w_router, b_router,
           w_gate, w_up, w_down, final_norm_g):
    batch, seq, d = x.shape
    depth = w_ada.shape[0]
    n = batch * seq
    mod = _modulation(c, w_ada, b_ada)
    xf = x.reshape(n, d)
    for i in range(depth):
        sh_a, sc_a, g_a, sh_m, sc_m, g_m = [m.reshape(batch, 1, d) for m in jnp.split(mod[i], 6, axis=-1)]
        diff = i % 2 == 0
        proj = _norm_proj(xf, norm_g[i, 0], sc_a, sh_a, w_in[i].astype(BF16), positions, diff, seq)
        if diff:
            o = _diff_attention(proj, diff_lambda[i // 2], diff_subln_g[i // 2], batch, seq, d, i)
        else:
            o = _sb_attention(proj, batch, seq, d)
        xf, h, meta, meta_t, counts = _outproj_route(o, xf, w_out[i].astype(BF16), g_a, norm_g[i, 1], sc_m, sh_m,
                                                     w_router, b_router, seq)
        final_g = final_norm_g if i == depth - 1 else None
        xf = _moe(h, meta, meta_t, counts, xf, g_m, w_gate, w_up, w_down, i, final_g, seq)
    return xf.reshape(batch, seq, d)
```

```python
import functools
import math

import jax
import jax.numpy as jnp
from jax import lax
from jax.experimental import pallas as pl
from jax.experimental.pallas import tpu as pltpu

F32 = jnp.float32
BF16 = jnp.bfloat16
I32 = jnp.int32

LANES = 128
HEAD_DIM = 64
ROPE_DIMS = HEAD_DIM // 4
ROPE_THETA = 500000.0
N_EXPERTS = 32
N_GROUPS = 4
GROUP_SIZE = N_EXPERTS // N_GROUPS
TOP_K = 2
NORM_EPS = 1e-6
EXPERT_ROWS = 512
EXPERT_SUB_ROWS = 256
HEADS_PER_STEP = 2
DIFF_BLOCKS_PER_TRIP = 2
ROW_TILE = 8
SB_DEAD_LOG2 = -110.0 * math.log2(math.e)
MASK_VALUE = -1e30
VMEM_LIMIT = 56 * 1024 * 1024


def _cparams(semantics, vmem=VMEM_LIMIT):
    return pltpu.CompilerParams(dimension_semantics=semantics, vmem_limit_bytes=vmem)


def _split_bf16(a):
    hi = a.astype(BF16)
    lo = (a - hi.astype(F32)).astype(BF16)
    return hi, lo


def _dot3(a, b):
    ah, al = _split_bf16(a)
    bh, bl = _split_bf16(b)
    d = functools.partial(jnp.dot, preferred_element_type=F32)
    return d(ah, bh) + d(ah, bl) + d(al, bh)


def _rms(x, g):
    ms = jnp.mean(x * x, axis=-1, keepdims=True)
    return x * lax.rsqrt(ms + NORM_EPS) * g


def _mod_kernel(c_ref, w_ref, b_ref, o_ref):
    c = c_ref[...]
    c_act = c / (1.0 + jnp.exp(-c))
    o_ref[...] = _dot3(c_act, w_ref[...]) + b_ref[...]


def _modulation(c, w_ada, b_ada):
    depth, d, six_d = w_ada.shape
    b = c.shape[0]
    rows = 16
    c_pad = jnp.zeros((rows, d), F32).at[:b].set(c)
    tn = 1536
    out = pl.pallas_call(
        _mod_kernel,
        out_shape=jax.ShapeDtypeStruct((depth, rows, six_d), F32),
        grid=(depth, six_d // tn),
        in_specs=[
            pl.BlockSpec((rows, d), lambda l, j: (0, 0)),
            pl.BlockSpec((None, d, tn), lambda l, j: (l, 0, j)),
            pl.BlockSpec((None, 1, tn), lambda l, j: (l, 0, j)),
        ],
        out_specs=pl.BlockSpec((None, rows, tn), lambda l, j: (l, 0, j)),
        compiler_params=_cparams(("arbitrary", "arbitrary")),
        name="adaln_mod",
    )(c_pad, w_ada, b_ada.reshape(depth, 1, six_d))
    return out[:, :b]


def _project(h, w_ref, o_ref, d, q_scale, pos_ref=None, pat_ref=None):
    p = jnp.dot(h.astype(BF16), w_ref[...], preferred_element_type=F32)
    rope = pos_ref is not None
    if rope:
        ang = pos_ref[...].astype(F32) * pat_ref[...]
        cs, sn = jnp.cos(ang), jnp.sin(ang)
        lane = lax.broadcasted_iota(I32, ang.shape, 1) & (HEAD_DIM - 1)
        half = ROPE_DIMS // 2
        s_lo = jnp.where(lane < half, -sn, 0.0)
        s_hi = jnp.where((lane >= half) & (lane < ROPE_DIMS), sn, 0.0)
    for j in range(2 * d // LANES):
        blk = p[:, j * LANES:(j + 1) * LANES]
        if rope:
            blk = (blk * cs + pltpu.roll(blk, half, 1) * s_hi
                   + pltpu.roll(blk, LANES - half, 1) * s_lo)
        if j < d // LANES:
            blk = blk * q_scale
        o_ref[:, j * LANES:(j + 1) * LANES] = blk.astype(BF16)
    o_ref[:, 2 * d:] = p[:, 2 * d:].astype(BF16)


Q_SCALE = HEAD_DIM ** -0.5 * math.log2(math.e)


def _proj_kernel(*refs, d, rope):
    if rope:
        x_ref, g_ref, sc_ref, sh_ref, w_ref, pos_ref, pat_ref, o_ref = refs
    else:
        x_ref, g_ref, sc_ref, sh_ref, w_ref, o_ref = refs
        pos_ref = pat_ref = None
    h = _rms(x_ref[...], g_ref[...]) * (1.0 + sc_ref[...]) + sh_ref[...]
    _project(h, w_ref, o_ref, d, Q_SCALE, pos_ref, pat_ref)


def _norm_proj(x, g, sc, sh, w_bf16, positions, rope, seq):
    n, d = x.shape
    tm = min(512, seq)
    per_b = seq // tm
    row = lambda i: (i, 0)
    bat = lambda i: (i // per_b, 0, 0)
    in_specs = [
        pl.BlockSpec((tm, d), row),
        pl.BlockSpec((1, d), lambda i: (0, 0)),
        pl.BlockSpec((None, 1, d), bat),
        pl.BlockSpec((None, 1, d), bat),
        pl.BlockSpec((d, 3 * d), lambda i: (0, 0)),
    ]
    args = [x, g.reshape(1, d), sc, sh, w_bf16]
    if rope:
        half = ROPE_DIMS // 2
        inv_freq = jnp.power(jnp.float32(ROPE_THETA), -jnp.arange(half, dtype=F32) * (2.0 / ROPE_DIMS))
        lane = jnp.arange(LANES) % HEAD_DIM
        pat = jnp.where(lane < ROPE_DIMS, inv_freq[lane % half], 0.0).astype(F32).reshape(1, LANES)
        in_specs += [pl.BlockSpec((tm, 1), row), pl.BlockSpec((1, LANES), lambda i: (0, 0))]
        args += [positions.reshape(n, 1), pat]
    return pl.pallas_call(
        functools.partial(_proj_kernel, d=d, rope=rope),
        out_shape=jax.ShapeDtypeStruct((n, 3 * d), BF16),
        grid=(n // tm,),
        in_specs=in_specs,
        out_specs=pl.BlockSpec((tm, 3 * d), row),
        compiler_params=_cparams(("arbitrary",)),
        name="norm_proj_rope" if rope else "norm_proj",
    )(*args)


def _stack_halves(q):
    lane = lax.broadcasted_iota(I32, q.shape, 1)
    zero = jnp.zeros_like(q)
    return jnp.concatenate([jnp.where(lane < HEAD_DIM, q, zero), jnp.where(lane >= HEAD_DIM, q, zero)], axis=0)


def _qk(qs, kb):
    return lax.dot_general(qs, kb, (((1,), (1,)), ((), ())), preferred_element_type=F32)


def _head_lanes(hh):
    return slice(hh * LANES, (hh + 1) * LANES)


def _lane_chunks(x):
    return [x[:, c * LANES:(c + 1) * LANES] for c in range(x.shape[1] // LANES)]


def _block_positions(shape, t, i, j):
    r = lax.broadcasted_iota(I32, shape, 0)
    qpos = i * t + jnp.where(r >= t, r - t, r)
    kpos = j * t + lax.broadcasted_iota(I32, shape, 1)
    return qpos, kpos


def _diff_attn_kernel(lam_ref, g_ref, q_ref, k_ref, v_ref, o_ref, m_ref, l_ref, acc_ref, *, t, lam_init):
    i = pl.program_id(2)
    qs = [_stack_halves(q_ref[:, _head_lanes(hh)]) for hh in range(HEADS_PER_STEP)]
    m_ref[...] = jnp.full(m_ref.shape, MASK_VALUE, F32)
    l_ref[...] = jnp.zeros(l_ref.shape, F32)
    acc_ref[...] = jnp.zeros(acc_ref.shape, F32)

    def scores(hh, j):
        start = pl.multiple_of(j * t, t)
        return _qk(qs[hh], k_ref[pl.ds(start, t), _head_lanes(hh)])

    def softmax_pv(hh, s, j, masked):
        start = pl.multiple_of(j * t, t)
        if masked:
            qpos, kpos = _block_positions(s.shape, t, i, j)
            s = jnp.where(kpos <= qpos, s, MASK_VALUE)
        chunks = _lane_chunks(s)
        m_old = m_ref[hh]
        m_new = jnp.maximum(m_old, jnp.max(functools.reduce(jnp.maximum, chunks), axis=1, keepdims=True))
        alpha = jnp.exp2(m_old - m_new)
        ps = [jnp.exp2(ch - m_new) for ch in chunks]
        l_ref[hh] = l_ref[hh] * alpha + functools.reduce(jnp.add, ps)
        p = jnp.concatenate([x.astype(BF16) for x in ps], axis=1)
        pv = jnp.dot(p, v_ref[pl.ds(start, t), _head_lanes(hh)], preferred_element_type=F32)
        acc_ref[hh] = acc_ref[hh] * alpha + pv
        m_ref[hh] = m_new

    def block_run(js, last_masked):
        work = [(hh, j, last_masked and n == len(js) - 1) for n, j in enumerate(js) for hh in range(HEADS_PER_STEP)]
        s_next = scores(work[0][0], work[0][1])
        for n, (hh, j, masked) in enumerate(work):
            s_cur = s_next
            if n + 1 < len(work):
                s_next = scores(work[n + 1][0], work[n + 1][1])
            softmax_pv(hh, s_cur, j, masked)

    run = DIFF_BLOCKS_PER_TRIP
    shift = run.bit_length() - 1

    def off_diag_run(p, carry):
        block_run([run * p + n for n in range(run)], False)
        return carry

    lax.fori_loop(0, lax.shift_right_logical(i, shift), off_diag_run, 0)

    rem = i & (run - 1)
    for r in range(run):
        @pl.when(rem == r)
        def _(r=r):
            block_run([i - r + n for n in range(r + 1)], True)

    lam = lam_ref[...]
    lam_full = (jnp.exp(jnp.sum(lam[0:1] * lam[1:2], axis=1, keepdims=True))
                - jnp.exp(jnp.sum(lam[2:3] * lam[3:4], axis=1, keepdims=True)) + lam_init)
    for hh in range(HEADS_PER_STEP):
        o_all = acc_ref[hh] / jnp.sum(l_ref[hh], axis=1, keepdims=True)
        o = o_all[:t] - lam_full * o_all[t:]
        o_ref[:, _head_lanes(hh)] = (_rms(o, g_ref[...]) * (1.0 - lam_init)).astype(BF16)


def _attn_specs(t, seq, nq, groups):
    w = HEADS_PER_STEP * LANES
    q_spec = pl.BlockSpec((t, w), lambda b, h, i: (b * nq + i, h))
    k_spec = pl.BlockSpec((seq, w), lambda b, h, i: (b, groups + h))
    v_spec = pl.BlockSpec((seq, w), lambda b, h, i: (b, 2 * groups + h))
    return q_spec, k_spec, v_spec


def _diff_attention(proj, lam, subln_g, batch, seq, d, layer_idx):
    n = batch * seq
    groups = d // (HEADS_PER_STEP * LANES)
    t = min(512, seq)
    nq = seq // t
    lam_init = 0.8 - 0.6 * math.exp(-0.3 * layer_idx)
    q_spec, k_spec, v_spec = _attn_specs(t, seq, nq, groups)
    stat = pltpu.VMEM((HEADS_PER_STEP, 2 * t, LANES), F32)
    return pl.pallas_call(
        functools.partial(_diff_attn_kernel, t=t, lam_init=lam_init),
        out_shape=jax.ShapeDtypeStruct((n, d), BF16),
        grid=(batch, groups, nq),
        in_specs=[
            pl.BlockSpec((4, HEAD_DIM), lambda b, h, i: (0, 0)),
            pl.BlockSpec((1, LANES), lambda b, h, i: (0, 0)),
            q_spec, k_spec, v_spec,
        ],
        out_specs=q_spec,
        scratch_shapes=[stat, stat, stat],
        compiler_params=_cparams(("arbitrary", "arbitrary", "arbitrary")),
        name="diff_attention",
    )(lam, subln_g.reshape(1, LANES), proj, proj, proj)


def _sb_attn_kernel(q_ref, k_ref, v_ref, tri_ref, o_ref, c_ref, acc_ref, *, t):
    i = pl.program_id(2)
    units = list(range(HEADS_PER_STEP))
    qs = [_stack_halves(q_ref[:, _head_lanes(u)]) for u in units]

    def gates(u, j, diag, guard):
        start = pl.multiple_of(j * t, t)
        z = _qk(qs[u], k_ref[pl.ds(start, t), _head_lanes(u)])
        log_beta = jnp.minimum(z, 0.0) - jnp.log2(1.0 + jnp.exp2(-jnp.abs(z)))
        log_rest = log_beta - z
        keep = None
        if diag:
            qpos, kpos = _block_positions(z.shape, t, i, j)
            keep = kpos < qpos
        if guard is not None:
            keep = lax.broadcasted_iota(I32, z.shape, 0) < guard
        if keep is not None:
            log_rest = jnp.where(keep, log_rest, 0.0)
        row_sum = jnp.sum(functools.reduce(jnp.add, _lane_chunks(log_rest)), axis=1, keepdims=True)
        return log_beta, log_rest, keep, row_sum

    def later_keys_sum(log_rest):
        return jnp.dot(log_rest.astype(BF16), tri_ref[...], preferred_element_type=F32)

    def weighted_values(u, j, log_beta, within, keep, c):
        start = pl.multiple_of(j * t, t)
        arg = log_beta + within
        if c is not None:
            arg = jnp.concatenate([ch + c for ch in _lane_chunks(arg)], axis=1)
        a = jnp.exp2(arg)
        if keep is not None:
            a = jnp.where(keep, a, 0.0)
        return jnp.dot(a.astype(BF16), v_ref[pl.ds(start, t), _head_lanes(u)], preferred_element_type=F32)

    def sweep(work, carried):
        gated, within, pv = {}, {}, {}
        for step in range(len(work) + 2):
            if step < len(work):
                gated[step] = gates(*work[step])
            if 1 <= step <= len(work):
                within[step - 1] = later_keys_sum(gated[step - 1][1])
            if step >= 2:
                n = step - 2
                u, j, _, _ = work[n]
                pv[n] = weighted_values(u, j, gated[n][0], within[n], gated[n][2], carried(n, gated))
        return gated, pv

    has_prev = jnp.where(i > 0, 2 * t, 0)
    j_prev = jnp.maximum(i - 1, 0)
    n_u = len(units)
    work = [(u, i, True, None) for u in units] + [(u, j_prev, False, has_prev) for u in units]
    gated, pv = sweep(work, lambda n, g: None if n < n_u else jnp.broadcast_to(g[n - n_u][3], (2 * t, LANES)))
    c_max = None
    for u in units:
        acc_ref[u] = pv[u] + pv[n_u + u]
        c2 = jnp.broadcast_to(gated[u][3], (2 * t, LANES)) + gated[n_u + u][3]
        c_ref[u] = c2
        c_max = jnp.max(c2) if c_max is None else jnp.maximum(c_max, jnp.max(c2))

    def cond(state):
        j, c_max = state
        return (j >= 0) & (c_max > SB_DEAD_LOG2)

    def body(state):
        j, _ = state
        cs = [c_ref[u] for u in units]
        gated, pv = sweep([(u, j, False, None) for u in units], lambda n, g: cs[n])
        c_max = None
        for u in units:
            acc_ref[u] += pv[u]
            c_new = cs[u] + gated[u][3]
            c_ref[u] = c_new
            c_max = jnp.max(c_new) if c_max is None else jnp.maximum(c_max, jnp.max(c_new))
        return j - 1, c_max

    lax.while_loop(cond, body, (i - 2, c_max))

    lane = lax.broadcasted_iota(I32, (t, LANES), 1)
    for hh in range(HEADS_PER_STEP):
        acc = acc_ref[hh]
        o_ref[:, _head_lanes(hh)] = jnp.where(lane < HEAD_DIM, acc[:t], acc[t:]).astype(BF16)


def _sb_attention(proj, batch, seq, d):
    n = batch * seq
    groups = d // (HEADS_PER_STEP * LANES)
    t = min(256, seq)
    nq = seq // t
    idx = jnp.arange(t)
    tri = (idx[:, None] > idx[None, :]).astype(BF16)
    q_spec, k_spec, v_spec = _attn_specs(t, seq, nq, groups)
    stat = pltpu.VMEM((HEADS_PER_STEP, 2 * t, LANES), F32)
    return pl.pallas_call(
        functools.partial(_sb_attn_kernel, t=t),
        out_shape=jax.ShapeDtypeStruct((n, d), BF16),
        grid=(batch, groups, nq),
        in_specs=[q_spec, k_spec, v_spec, pl.BlockSpec((t, t), lambda b, h, i: (0, 0))],
        out_specs=q_spec,
        scratch_shapes=[stat, stat],
        compiler_params=_cparams(("arbitrary", "arbitrary", "arbitrary")),
        name="sb_attention",
    )(proj, proj, proj, tri)


def _first_index_of(mask, lane, sentinel):
    return jnp.min(jnp.where(mask, lane, float(sentinel)), axis=1, keepdims=True)


def _top2(v, lane):
    neg = -jnp.inf
    m1 = jnp.max(v, axis=1, keepdims=True)
    i1 = _first_index_of(v == m1, lane, N_EXPERTS)
    v2 = jnp.where(lane == i1, neg, v)
    m2 = jnp.max(v2, axis=1, keepdims=True)
    i2 = _first_index_of(v2 == m2, lane, N_EXPERTS)
    return m1, i1, m2, i2


def _route(scores, sel):
    lane_i = lax.broadcasted_iota(I32, sel.shape, 1)
    lane = lane_i.astype(F32)
    grp = lax.shift_right_logical(lane_i, int(math.log2(GROUP_SIZE))).astype(F32)
    neg = -jnp.inf
    best = None
    g_best = None
    for g in range(N_GROUPS):
        m1, _, m2, _ = _top2(jnp.where(grp == float(g), sel, neg), lane)
        score = m1 + m2
        if g == 0:
            best, g_best = score, jnp.zeros(score.shape, F32)
        else:
            better = score > best
            g_best = jnp.where(better, float(g), g_best)
            best = jnp.where(better, score, best)
    _, e1, _, e2 = _top2(jnp.where(grp == g_best, sel, neg), lane)
    hot1, hot2 = lane == e1, lane == e2
    s1 = jnp.sum(jnp.where(hot1, scores, 0.0), axis=1, keepdims=True)
    s2 = jnp.sum(jnp.where(hot2, scores, 0.0), axis=1, keepdims=True)
    tot = s1 + s2
    return s1 / tot, s2 / tot, e1, e2, hot1, hot2


def _outproj_route_kernel(o_ref, x_ref, wo_ref, ga_ref, g_ref, sc_ref, sh_ref, wr_ref, br_ref, tril_ref,
                          xo_ref, h_ref, meta_ref, meta_t_ref, cnt_ref, *, ts, n_sub):
    @pl.when(pl.program_id(0) == 0)
    def _():
        cnt_ref[...] = jnp.zeros(cnt_ref.shape, F32)

    subs = range(n_sub)
    rows = [pl.ds(s * ts, ts) for s in subs]
    mix = [jnp.dot(o_ref[rows[s], :], wo_ref[...], preferred_element_type=F32) for s in subs]
    hs = []
    for s in subs:
        xn = x_ref[rows[s], :] + ga_ref[...] * mix[s]
        xo_ref[rows[s], :] = xn
        h = _rms(xn, g_ref[...]) * (1.0 + sc_ref[...]) + sh_ref[...]
        _store_rows(h_ref, s * ts, h)
        hs.append(h)

    logits = [_dot3(hs[s], wr_ref[...]) for s in subs]
    routed = []
    for s in subs:
        e = jnp.exp(logits[s] - jnp.max(logits[s], axis=1, keepdims=True))
        scores = e / jnp.sum(e, axis=1, keepdims=True)
        routed.append(_route(scores, scores + br_ref[...]))

    tril = tril_ref[...]
    ones = [(jnp.where(r[4], 1.0, 0.0), jnp.where(r[5], 1.0, 0.0)) for r in routed]
    prefix = [(jnp.dot(tril, o1.astype(BF16), preferred_element_type=F32),
               jnp.dot(tril, o2.astype(BF16), preferred_element_type=F32)) for o1, o2 in ones]
    base = cnt_ref[...]
    ml = lax.broadcasted_iota(I32, (ts, LANES), 1)
    for s in subs:
        g1, g2, e1, e2, hot1, hot2 = routed[s]
        tot1 = jnp.sum(ones[s][0], axis=0, keepdims=True)
        tot2 = jnp.sum(ones[s][1], axis=0, keepdims=True)
        r1 = jnp.sum(jnp.where(hot1, base + prefix[s][0], 0.0), axis=1, keepdims=True)
        r2 = jnp.sum(jnp.where(hot2, base + tot1 + prefix[s][1], 0.0), axis=1, keepdims=True)
        base = base + tot1 + tot2
        meta = jnp.where(ml == 0, g1, 0.0)
        for k, col in enumerate((g2, e1, e2, r1, r2)):
            meta = jnp.where(ml == k + 1, col, meta)
        meta_ref[rows[s], :] = meta
        meta_t_ref[:, rows[s]] = meta.T[:meta_t_ref.shape[0]]
    cnt_ref[...] = base


def _outproj_route(o, x, wo_bf16, ga, g, sc, sh, w_router, b_router, seq):
    n, d = x.shape
    tm = min(512, seq)
    ts = min(128, tm)
    per_b = seq // tm
    row = lambda i: (i, 0)
    bat = lambda i: (i // per_b, 0, 0)
    const = lambda i: (0, 0)
    idx = jnp.arange(ts)
    tril = (idx[None, :] < idx[:, None]).astype(BF16)
    return pl.pallas_call(
        functools.partial(_outproj_route_kernel, ts=ts, n_sub=tm // ts),
        out_shape=(
            jax.ShapeDtypeStruct((n, d), F32),
            jax.ShapeDtypeStruct((n * ROW_TILE, LANES), F32),
            jax.ShapeDtypeStruct((n, LANES), F32),
            jax.ShapeDtypeStruct((8, n), F32),
            jax.ShapeDtypeStruct((1, N_EXPERTS), F32),
        ),
        grid=(n // tm,),
        in_specs=[
            pl.BlockSpec((tm, d), row),
            pl.BlockSpec((tm, d), row),
            pl.BlockSpec((d, d), const),
            pl.BlockSpec((None, 1, d), bat),
            pl.BlockSpec((1, d), const),
            pl.BlockSpec((None, 1, d), bat),
            pl.BlockSpec((None, 1, d), bat),
            pl.BlockSpec((d, N_EXPERTS), const),
            pl.BlockSpec((1, N_EXPERTS), const),
            pl.BlockSpec((ts, ts), const),
        ],
        out_specs=(
            pl.BlockSpec((tm, d), row),
            pl.BlockSpec((tm * ROW_TILE, LANES), row),
            pl.BlockSpec((tm, LANES), row),
            pl.BlockSpec((8, tm), lambda i: (0, i)),
            pl.BlockSpec((1, N_EXPERTS), const),
        ),
        compiler_params=_cparams(("arbitrary",)),
        name="outproj_route",
    )(o, x, wo_bf16, ga, g.reshape(1, d), sc, sh, w_router, b_router.reshape(1, N_EXPERTS), tril)


def _row_copy(src, src_row, dst, dst_row, sem, rows=1):
    n = rows * ROW_TILE
    return pltpu.make_async_copy(src.at[pl.ds(pl.multiple_of(src_row * ROW_TILE, ROW_TILE), n)],
                                 dst.at[pl.ds(pl.multiple_of(dst_row * ROW_TILE, ROW_TILE), n)], sem)


def _load_rows(ref, start_row, rows):
    return jnp.concatenate([ref[pl.ds(start_row * ROW_TILE + k, rows, stride=ROW_TILE), :]
                            for k in range(ROW_TILE)], axis=1)


def _store_rows(ref, start_row, value):
    for k in range(ROW_TILE):
        ref[pl.ds(start_row * ROW_TILE + k, value.shape[0], stride=ROW_TILE), :] = value[:, k * LANES:(k + 1) * LANES]


def _dispatch_kernel(tail_ref, dest_hbm, h_ref, rows_hbm, dest_smem, stage, zero_buf, idx_sem, row_sems, *, tc, n_blk):
    c = pl.program_id(0)
    last = pl.num_programs(0) - 1

    @pl.when(c == 0)
    def _():
        zero_buf[...] = jnp.zeros(zero_buf.shape, F32)

        def fill(row):
            cp = _row_copy(zero_buf, 0, rows_hbm, row, row_sems.at[0], rows=EXPERT_ROWS)
            cp.start()
            cp.wait()

        def fill_tail(e, carry):
            fill(tail_ref[e])
            return carry

        def fill_unused_block(b, carry):
            fill(b * EXPERT_ROWS)
            return carry

        lax.fori_loop(0, N_EXPERTS, fill_tail, 0)
        lax.fori_loop(tail_ref[N_EXPERTS], n_blk, fill_unused_block, 0)

    idx_copy = pltpu.make_async_copy(dest_hbm.at[pl.ds(c * 2 * tc, 2 * tc)], dest_smem, idx_sem)
    idx_copy.start()
    idx_copy.wait()

    def drain(slot):
        def body(tk, carry):
            _row_copy(stage.at[slot], 0, rows_hbm, 0, row_sems.at[slot]).wait()
            _row_copy(stage.at[slot], 0, rows_hbm, 0, row_sems.at[slot]).wait()
            return carry

        lax.fori_loop(0, tc, body, 0, unroll=8)

    for slot in range(2):
        @pl.when((c & 1) == slot)
        def _(slot=slot):
            stage[slot] = h_ref[...]

            def issue(tk, carry):
                _row_copy(stage.at[slot], tk, rows_hbm, dest_smem[tk], row_sems.at[slot]).start(priority=0)
                _row_copy(stage.at[slot], tk, rows_hbm, dest_smem[tc + tk], row_sems.at[slot]).start(priority=1)
                return carry

            lax.fori_loop(0, tc, issue, 0, unroll=8)

            @pl.when(c > 0)
            def _():
                drain(1 - slot)

            @pl.when(c == last)
            def _():
                drain(slot)


def _dispatch(tail, dest_flat, h_tiles, n_rows, tc):
    n = h_tiles.shape[0] // ROW_TILE
    any_spec = pl.BlockSpec(memory_space=pl.ANY)
    grid_spec = pltpu.PrefetchScalarGridSpec(
        num_scalar_prefetch=1,
        grid=(n // tc,),
        in_specs=[any_spec, pl.BlockSpec((tc * ROW_TILE, LANES), lambda i, tail: (i, 0))],
        out_specs=any_spec,
        scratch_shapes=[pltpu.SMEM((2 * tc,), I32), pltpu.VMEM((2, tc * ROW_TILE, LANES), F32),
                        pltpu.VMEM((EXPERT_ROWS * ROW_TILE, LANES), F32),
                        pltpu.SemaphoreType.DMA, pltpu.SemaphoreType.DMA((2,))],
    )
    return pl.pallas_call(
        functools.partial(_dispatch_kernel, tc=tc, n_blk=n_rows // EXPERT_ROWS),
        out_shape=jax.ShapeDtypeStruct((n_rows * ROW_TILE, LANES), F32),
        grid_spec=grid_spec,
        compiler_params=_cparams(("arbitrary",)),
        name="moe_dispatch",
    )(tail, dest_flat, h_tiles)


def _experts_kernel(blk_e_ref, n_used_ref, x_ref, wg_ref, wu_ref, wd_ref, y_ref, wg_s, wu_s, wd_s):
    i = pl.program_id(0)
    used = i < n_used_ref[0]
    prev = blk_e_ref[jnp.maximum(i - 1, 0)]
    new_expert = (i == 0) | (blk_e_ref[i] != prev)

    @pl.when(used & new_expert)
    def _():
        wg_s[...] = wg_ref[...].astype(BF16)
        wu_s[...] = wu_ref[...].astype(BF16)
        wd_s[...] = wd_ref[...].astype(BF16)

    @pl.when(used)
    def _():
        groups = range(EXPERT_ROWS // EXPERT_SUB_ROWS)
        dot = functools.partial(jnp.dot, preferred_element_type=F32)
        xb = [_load_rows(x_ref, s * EXPERT_SUB_ROWS, EXPERT_SUB_ROWS).astype(BF16) for s in groups]
        gate = [dot(xb[s], wg_s[...]) for s in groups]
        up = [dot(xb[s], wu_s[...]) for s in groups]
        hid = [(gate[s] / (1.0 + jnp.exp(-gate[s])) * up[s]).astype(BF16) for s in groups]
        for s in groups:
            _store_rows(y_ref, s * EXPERT_SUB_ROWS, dot(hid[s], wd_s[...]))

    @pl.when(jnp.logical_not(used))
    def _():
        y_ref[...] = jnp.zeros(y_ref.shape, F32)


def _experts(blk_e, n_used, x_tiles, w_gate, w_up, w_down, layer):
    d, f = w_gate.shape[-2:]
    n_blk = x_tiles.shape[0] // (EXPERT_ROWS * ROW_TILE)
    blk = (EXPERT_ROWS * ROW_TILE, LANES)
    last_used = lambda n_used_ref: jnp.maximum(n_used_ref[0] - 1, 0)
    x_map = lambda i, be, nu: (jnp.minimum(i, last_used(nu)), 0)
    w_map = lambda i, be, nu: (layer, be[jnp.minimum(i, last_used(nu))], 0, 0)
    grid_spec = pltpu.PrefetchScalarGridSpec(
        num_scalar_prefetch=2,
        grid=(n_blk,),
        in_specs=[
            pl.BlockSpec(blk, x_map),
            pl.BlockSpec((None, None, d, f), w_map),
            pl.BlockSpec((None, None, d, f), w_map),
            pl.BlockSpec((None, None, f, d), w_map),
        ],
        out_specs=pl.BlockSpec(blk, lambda i, be, nu: (i, 0)),
        scratch_shapes=[pltpu.VMEM((d, f), BF16), pltpu.VMEM((d, f), BF16), pltpu.VMEM((f, d), BF16)],
    )
    return pl.pallas_call(
        _experts_kernel,
        out_shape=jax.ShapeDtypeStruct(x_tiles.shape, F32),
        grid_spec=grid_spec,
        compiler_params=_cparams(("arbitrary",)),
        name="moe_experts",
    )(blk_e, n_used, x_tiles, w_gate, w_up, w_down)


_COMBINE_EXTRA_INPUTS = {"plain": 0, "final": 1, "proj": 4}


def _combine_kernel(*refs, tc, mode):
    dest_hbm, y_hbm, x_ref, meta_ref, gm_ref = refs[:5]
    n_extra = _COMBINE_EXTRA_INPUTS[mode]
    extra = refs[5:5 + n_extra]
    outs = refs[5 + n_extra:-8]
    o_ref = outs[0]
    dest_a, dest_b, a0, a1, b0, b1, idx_sem, row_sems = refs[-8:]
    dest_smem = (dest_a, dest_b)
    bufs = ((a0, a1), (b0, b1))
    c = pl.program_id(0)
    last = pl.num_programs(0) - 1

    def fetch(step, slot):
        idx_copy = pltpu.make_async_copy(dest_hbm.at[pl.ds(step * 2 * tc, 2 * tc)], dest_smem[slot], idx_sem)
        idx_copy.start()
        idx_copy.wait()

        def issue(tk, carry):
            _row_copy(y_hbm, dest_smem[slot][tk], bufs[slot][0], tk, row_sems.at[slot]).start(priority=0)
            _row_copy(y_hbm, dest_smem[slot][tc + tk], bufs[slot][1], tk, row_sems.at[slot]).start(priority=1)
            return carry

        lax.fori_loop(0, tc, issue, 0, unroll=8)

    def drain(slot):
        def body(tk, carry):
            _row_copy(y_hbm, 0, bufs[slot][0], 0, row_sems.at[slot]).wait()
            _row_copy(y_hbm, 0, bufs[slot][1], 0, row_sems.at[slot]).wait()
            return carry

        lax.fori_loop(0, tc, body, 0, unroll=8)

    @pl.when(c == 0)
    def _():
        fetch(0, 0)

    for slot in range(2):
        @pl.when((c & 1) == slot)
        def _(slot=slot):
            @pl.when(c < last)
            def _():
                fetch(c + 1, 1 - slot)

            drain(slot)
            meta = meta_ref[...]
            y = meta[:, 0:1] * _load_rows(bufs[slot][0], 0, tc) + meta[:, 1:2] * _load_rows(bufs[slot][1], 0, tc)
            xn = x_ref[...] + gm_ref[...] * y
            if mode == "final":
                xn = _rms(xn, extra[0][...])
            o_ref[...] = xn
            if mode == "proj":
                g_ref, sc_ref, sh_ref, w_ref = extra
                h = _rms(xn, g_ref[...]) * (1.0 + sc_ref[...]) + sh_ref[...]
                _project(h, w_ref, outs[1], xn.shape[1], Q_SCALE)


def _combine(dest_flat, y_rows, x, meta, gm, seq, tc, mode, extra):
    n, d = x.shape
    per_b = seq // tc
    row = lambda i: (i, 0)
    bat = lambda i: (i // per_b, 0, 0)
    const = lambda i: (0, 0)
    any_spec = pl.BlockSpec(memory_space=pl.ANY)
    in_specs = [any_spec, any_spec, pl.BlockSpec((tc, d), row), pl.BlockSpec((tc, LANES), row),
                pl.BlockSpec((None, 1, d), bat)]
    args = [dest_flat, y_rows, x, meta, gm]
    out_shape = [jax.ShapeDtypeStruct((n, d), F32)]
    out_specs = [pl.BlockSpec((tc, d), row)]
    if mode == "final":
        in_specs.append(pl.BlockSpec((1, d), const))
        args.append(extra[0].reshape(1, d))
    elif mode == "proj":
        g, sc, sh, w_bf16 = extra
        in_specs += [pl.BlockSpec((1, d), const), pl.BlockSpec((None, 1, d), bat), pl.BlockSpec((None, 1, d), bat),
                     pl.BlockSpec((d, 3 * d), const)]
        args += [g.reshape(1, d), sc, sh, w_bf16]
        out_shape.append(jax.ShapeDtypeStruct((n, 3 * d), BF16))
        out_specs.append(pl.BlockSpec((tc, 3 * d), row))
    out = pl.pallas_call(
        functools.partial(_combine_kernel, tc=tc, mode=mode),
        out_shape=tuple(out_shape),
        grid=(n // tc,),
        in_specs=in_specs,
        out_specs=tuple(out_specs),
        scratch_shapes=[pltpu.SMEM((2 * tc,), I32)] * 2
        + [pltpu.VMEM((tc * ROW_TILE, LANES), F32)] * 4
        + [pltpu.SemaphoreType.DMA, pltpu.SemaphoreType.DMA((2,))],
        compiler_params=_cparams(("arbitrary",)),
        name="moe_combine_" + mode,
    )(*args)
    return out if mode == "proj" else out[0]


def _moe(h_tiles, meta, meta_t, counts, x, gm, w_gate, w_up, w_down, layer, seq, mode, extra):
    n, d = x.shape
    assert d == ROW_TILE * LANES
    tc = min(512, seq)
    n_rows = n * TOP_K + N_EXPERTS * EXPERT_ROWS
    n_blk = n_rows // EXPERT_ROWS
    cnt = counts[0].astype(I32)
    padded = ((cnt + EXPERT_ROWS - 1) // EXPERT_ROWS) * EXPERT_ROWS
    seg_end = jnp.cumsum(padded)
    seg_start = seg_end - padded
    n_used = (seg_end[-1:] // EXPERT_ROWS).astype(I32)
    blk_start = jnp.arange(n_blk, dtype=I32) * EXPERT_ROWS
    blk_e = jnp.minimum(jnp.sum(seg_end[None, :] <= blk_start[:, None], axis=1), N_EXPERTS - 1).astype(I32)
    expert = meta_t[2:4].astype(I32)
    rank = meta_t[4:6].astype(I32)
    hot = expert[:, :, None] == jnp.arange(N_EXPERTS, dtype=I32)
    dest = rank + jnp.sum(jnp.where(hot, seg_start, 0), axis=-1)
    dest_flat = dest.reshape(TOP_K, n // tc, tc).transpose(1, 0, 2).reshape(-1)

    tail = jnp.concatenate([jnp.minimum(seg_start + cnt, n_rows - EXPERT_ROWS), n_used]).astype(I32)

    x_rows = _dispatch(tail, dest_flat, h_tiles, n_rows, tc)
    y_rows = _experts(blk_e, n_used, x_rows, w_gate, w_up, w_down, layer)
    return _combine(dest_flat, y_rows, x, meta, gm, seq, tc, mode, extra)


def kernel(x, c, positions, w_ada, b_ada, norm_g, w_in, w_out, diff_lambda, diff_subln_g, w_router, b_router,
           w_gate, w_up, w_down, final_norm_g):
    batch, seq, d = x.shape
    depth = w_ada.shape[0]
    n = batch * seq
    mod = _modulation(c, w_ada, b_ada)
    xf = x.reshape(n, d)
    mods = [[m.reshape(batch, 1, d) for m in jnp.split(mod[i], 6, axis=-1)] for i in range(depth)]
    is_diff = lambda i: i % 2 == 0
    proj = None
    for i in range(depth):
        sh_a, sc_a, g_a, sh_m, sc_m, g_m = mods[i]
        if proj is None:
            proj = _norm_proj(xf, norm_g[i, 0], sc_a, sh_a, w_in[i].astype(BF16), positions, is_diff(i), seq)
        if is_diff(i):
            o = _diff_attention(proj, diff_lambda[i // 2], diff_subln_g[i // 2], batch, seq, d, i)
        else:
            o = _sb_attention(proj, batch, seq, d)
        xf, h, meta, meta_t, counts = _outproj_route(o, xf, w_out[i].astype(BF16), g_a, norm_g[i, 1], sc_m, sh_m,
                                                     w_router, b_router, seq)
        proj = None
        if i == depth - 1:
            mode, extra = "final", (final_norm_g,)
        elif not is_diff(i + 1):
            mode, extra = "proj", (norm_g[i + 1, 0], mods[i + 1][1], mods[i + 1][0], w_in[i + 1].astype(BF16))
        else:
            mode, extra = "plain", ()
        out = _moe(h, meta, meta_t, counts, xf, g_m, w_gate, w_up, w_down, i, seq, mode, extra)
        xf, proj = out if mode == "proj" else (out, None)
    return xf.reshape(batch, seq, d)
```

```python
import functools
import math

import jax
import jax.numpy as jnp
from jax import lax
from jax.experimental import pallas as pl
from jax.experimental.pallas import tpu as pltpu

F32 = jnp.float32
BF16 = jnp.bfloat16
I32 = jnp.int32

LANES = 128
HEAD_DIM = 64
ROPE_DIMS = HEAD_DIM // 4
ROPE_THETA = 500000.0
N_EXPERTS = 32
N_GROUPS = 4
GROUP_SIZE = N_EXPERTS // N_GROUPS
TOP_K = 2
NORM_EPS = 1e-6
EXPERT_ROWS = 512
EXPERT_SUB_ROWS = 256
HEADS_PER_STEP = 2
DIFF_BLOCKS_PER_TRIP = 2
ROW_TILE = 8
SB_DEAD_LOG2 = -110.0 * math.log2(math.e)
MASK_VALUE = -1e30
VMEM_LIMIT = 56 * 1024 * 1024


def _cparams(semantics, vmem=VMEM_LIMIT):
    return pltpu.CompilerParams(dimension_semantics=semantics, vmem_limit_bytes=vmem)


def _split_bf16(a):
    hi = a.astype(BF16)
    lo = (a - hi.astype(F32)).astype(BF16)
    return hi, lo


def _dot3(a, b):
    ah, al = _split_bf16(a)
    bh, bl = _split_bf16(b)
    d = functools.partial(jnp.dot, preferred_element_type=F32)
    return d(ah, bh) + d(ah, bl) + d(al, bh)


def _rms(x, g):
    ms = jnp.mean(x * x, axis=-1, keepdims=True)
    return x * lax.rsqrt(ms + NORM_EPS) * g


def _mod_kernel(c_ref, w_ref, b_ref, o_ref):
    c = c_ref[...]
    c_act = c / (1.0 + jnp.exp(-c))
    o_ref[...] = _dot3(c_act, w_ref[...]) + b_ref[...]


def _modulation(c, w_ada, b_ada):
    depth, d, six_d = w_ada.shape
    b = c.shape[0]
    rows = 16
    c_pad = jnp.zeros((rows, d), F32).at[:b].set(c)
    tn = 1536
    out = pl.pallas_call(
        _mod_kernel,
        out_shape=jax.ShapeDtypeStruct((depth, rows, six_d), F32),
        grid=(depth, six_d // tn),
        in_specs=[
            pl.BlockSpec((rows, d), lambda l, j: (0, 0)),
            pl.BlockSpec((None, d, tn), lambda l, j: (l, 0, j)),
            pl.BlockSpec((None, 1, tn), lambda l, j: (l, 0, j)),
        ],
        out_specs=pl.BlockSpec((None, rows, tn), lambda l, j: (l, 0, j)),
        compiler_params=_cparams(("arbitrary", "arbitrary")),
        name="adaln_mod",
    )(c_pad, w_ada, b_ada.reshape(depth, 1, six_d))
    return out[:, :b]


Q_SCALE = HEAD_DIM ** -0.5 * math.log2(math.e)


def _proj_kernel(*refs, d, rope):
    if rope:
        x_ref, g_ref, sc_ref, sh_ref, w_ref, pos_ref, pat_ref, o_ref = refs
    else:
        x_ref, g_ref, sc_ref, sh_ref, w_ref, o_ref = refs
    h = _rms(x_ref[...], g_ref[...]) * (1.0 + sc_ref[...]) + sh_ref[...]
    p = jnp.dot(h.astype(BF16), w_ref[...], preferred_element_type=F32)
    if rope:
        ang = pos_ref[...].astype(F32) * pat_ref[...]
        cs, sn = jnp.cos(ang), jnp.sin(ang)
        lane = lax.broadcasted_iota(I32, ang.shape, 1) & (HEAD_DIM - 1)
        half = ROPE_DIMS // 2
        s_lo = jnp.where(lane < half, -sn, 0.0)
        s_hi = jnp.where((lane >= half) & (lane < ROPE_DIMS), sn, 0.0)
    for j in range(2 * d // LANES):
        blk = p[:, j * LANES:(j + 1) * LANES]
        if rope:
            blk = (blk * cs + pltpu.roll(blk, half, 1) * s_hi
                   + pltpu.roll(blk, LANES - half, 1) * s_lo)
        if j < d // LANES:
            blk = blk * Q_SCALE
        o_ref[:, j * LANES:(j + 1) * LANES] = blk.astype(BF16)
    o_ref[:, 2 * d:] = p[:, 2 * d:].astype(BF16)


def _norm_proj(x, g, sc, sh, w_bf16, positions, rope, seq):
    n, d = x.shape
    tm = min(512, seq)
    per_b = seq // tm
    row = lambda i: (i, 0)
    bat = lambda i: (i // per_b, 0, 0)
    in_specs = [
        pl.BlockSpec((tm, d), row),
        pl.BlockSpec((1, d), lambda i: (0, 0)),
        pl.BlockSpec((None, 1, d), bat),
        pl.BlockSpec((None, 1, d), bat),
        pl.BlockSpec((d, 3 * d), lambda i: (0, 0)),
    ]
    args = [x, g.reshape(1, d), sc, sh, w_bf16]
    if rope:
        half = ROPE_DIMS // 2
        inv_freq = jnp.power(jnp.float32(ROPE_THETA), -jnp.arange(half, dtype=F32) * (2.0 / ROPE_DIMS))
        lane = jnp.arange(LANES) % HEAD_DIM
        pat = jnp.where(lane < ROPE_DIMS, inv_freq[lane % half], 0.0).astype(F32).reshape(1, LANES)
        in_specs += [pl.BlockSpec((tm, 1), row), pl.BlockSpec((1, LANES), lambda i: (0, 0))]
        args += [positions.reshape(n, 1), pat]
    return pl.pallas_call(
        functools.partial(_proj_kernel, d=d, rope=rope),
        out_shape=jax.ShapeDtypeStruct((n, 3 * d), BF16),
        grid=(n // tm,),
        in_specs=in_specs,
        out_specs=pl.BlockSpec((tm, 3 * d), row),
        compiler_params=_cparams(("arbitrary",)),
        name="norm_proj_rope" if rope else "norm_proj",
    )(*args)


def _stack_halves(q):
    lane = lax.broadcasted_iota(I32, q.shape, 1)
    zero = jnp.zeros_like(q)
    return jnp.concatenate([jnp.where(lane < HEAD_DIM, q, zero), jnp.where(lane >= HEAD_DIM, q, zero)], axis=0)


def _qk(qs, kb):
    return lax.dot_general(qs, kb, (((1,), (1,)), ((), ())), preferred_element_type=F32)


def _head_lanes(hh):
    return slice(hh * LANES, (hh + 1) * LANES)


def _lane_chunks(x):
    return [x[:, c * LANES:(c + 1) * LANES] for c in range(x.shape[1] // LANES)]


def _block_positions(shape, t, i, j):
    r = lax.broadcasted_iota(I32, shape, 0)
    qpos = i * t + jnp.where(r >= t, r - t, r)
    kpos = j * t + lax.broadcasted_iota(I32, shape, 1)
    return qpos, kpos


def _diff_attn_kernel(lam_ref, g_ref, q_ref, k_ref, v_ref, o_ref, m_ref, l_ref, acc_ref, *, t, lam_init):
    i = pl.program_id(2)
    qs = [_stack_halves(q_ref[:, _head_lanes(hh)]) for hh in range(HEADS_PER_STEP)]
    m_ref[...] = jnp.full(m_ref.shape, MASK_VALUE, F32)
    l_ref[...] = jnp.zeros(l_ref.shape, F32)
    acc_ref[...] = jnp.zeros(acc_ref.shape, F32)

    def scores(hh, j):
        start = pl.multiple_of(j * t, t)
        return _qk(qs[hh], k_ref[pl.ds(start, t), _head_lanes(hh)])

    def softmax_pv(hh, s, j, masked):
        start = pl.multiple_of(j * t, t)
        if masked:
            qpos, kpos = _block_positions(s.shape, t, i, j)
            s = jnp.where(kpos <= qpos, s, MASK_VALUE)
        chunks = _lane_chunks(s)
        m_old = m_ref[hh]
        m_new = jnp.maximum(m_old, jnp.max(functools.reduce(jnp.maximum, chunks), axis=1, keepdims=True))
        alpha = jnp.exp2(m_old - m_new)
        ps = [jnp.exp2(ch - m_new) for ch in chunks]
        l_ref[hh] = l_ref[hh] * alpha + functools.reduce(jnp.add, ps)
        p = jnp.concatenate([x.astype(BF16) for x in ps], axis=1)
        pv = jnp.dot(p, v_ref[pl.ds(start, t), _head_lanes(hh)], preferred_element_type=F32)
        acc_ref[hh] = acc_ref[hh] * alpha + pv
        m_ref[hh] = m_new

    def block_run(js, last_masked):
        work = [(hh, j, last_masked and n == len(js) - 1) for n, j in enumerate(js) for hh in range(HEADS_PER_STEP)]
        s_next = scores(work[0][0], work[0][1])
        for n, (hh, j, masked) in enumerate(work):
            s_cur = s_next
            if n + 1 < len(work):
                s_next = scores(work[n + 1][0], work[n + 1][1])
            softmax_pv(hh, s_cur, j, masked)

    run = DIFF_BLOCKS_PER_TRIP
    shift = run.bit_length() - 1

    def off_diag_run(p, carry):
        block_run([run * p + n for n in range(run)], False)
        return carry

    lax.fori_loop(0, lax.shift_right_logical(i, shift), off_diag_run, 0)

    rem = i & (run - 1)
    for r in range(run):
        @pl.when(rem == r)
        def _(r=r):
            block_run([i - r + n for n in range(r + 1)], True)

    lam = lam_ref[...]
    lam_full = (jnp.exp(jnp.sum(lam[0:1] * lam[1:2], axis=1, keepdims=True))
                - jnp.exp(jnp.sum(lam[2:3] * lam[3:4], axis=1, keepdims=True)) + lam_init)
    for hh in range(HEADS_PER_STEP):
        o_all = acc_ref[hh] / jnp.sum(l_ref[hh], axis=1, keepdims=True)
        o = o_all[:t] - lam_full * o_all[t:]
        o_ref[:, _head_lanes(hh)] = (_rms(o, g_ref[...]) * (1.0 - lam_init)).astype(BF16)


def _attn_specs(t, seq, nq, groups):
    w = HEADS_PER_STEP * LANES
    q_spec = pl.BlockSpec((t, w), lambda b, h, i: (b * nq + i, h))
    k_spec = pl.BlockSpec((seq, w), lambda b, h, i: (b, groups + h))
    v_spec = pl.BlockSpec((seq, w), lambda b, h, i: (b, 2 * groups + h))
    return q_spec, k_spec, v_spec


def _diff_attention(proj, lam, subln_g, batch, seq, d, layer_idx):
    n = batch * seq
    groups = d // (HEADS_PER_STEP * LANES)
    t = min(512, seq)
    nq = seq // t
    lam_init = 0.8 - 0.6 * math.exp(-0.3 * layer_idx)
    q_spec, k_spec, v_spec = _attn_specs(t, seq, nq, groups)
    stat = pltpu.VMEM((HEADS_PER_STEP, 2 * t, LANES), F32)
    return pl.pallas_call(
        functools.partial(_diff_attn_kernel, t=t, lam_init=lam_init),
        out_shape=jax.ShapeDtypeStruct((n, d), BF16),
        grid=(batch, groups, nq),
        in_specs=[
            pl.BlockSpec((4, HEAD_DIM), lambda b, h, i: (0, 0)),
            pl.BlockSpec((1, LANES), lambda b, h, i: (0, 0)),
            q_spec, k_spec, v_spec,
        ],
        out_specs=q_spec,
        scratch_shapes=[stat, stat, stat],
        compiler_params=_cparams(("arbitrary", "arbitrary", "arbitrary")),
        name="diff_attention",
    )(lam, subln_g.reshape(1, LANES), proj, proj, proj)


def _sb_attn_kernel(q_ref, k_ref, v_ref, tri_ref, o_ref, c_ref, acc_ref, *, t):
    i = pl.program_id(2)
    units = list(range(HEADS_PER_STEP))
    qs = [_stack_halves(q_ref[:, _head_lanes(u)]) for u in units]

    def gates(u, j, diag, guard):
        start = pl.multiple_of(j * t, t)
        z = _qk(qs[u], k_ref[pl.ds(start, t), _head_lanes(u)])
        log_beta = jnp.minimum(z, 0.0) - jnp.log2(1.0 + jnp.exp2(-jnp.abs(z)))
        log_rest = log_beta - z
        keep = None
        if diag:
            qpos, kpos = _block_positions(z.shape, t, i, j)
            keep = kpos < qpos
        if guard is not None:
            keep = lax.broadcasted_iota(I32, z.shape, 0) < guard
        if keep is not None:
            log_rest = jnp.where(keep, log_rest, 0.0)
        row_sum = jnp.sum(functools.reduce(jnp.add, _lane_chunks(log_rest)), axis=1, keepdims=True)
        return log_beta, log_rest, keep, row_sum

    def later_keys_sum(log_rest):
        return jnp.dot(log_rest.astype(BF16), tri_ref[...], preferred_element_type=F32)

    def weighted_values(u, j, log_beta, within, keep, c):
        start = pl.multiple_of(j * t, t)
        arg = log_beta + within
        if c is not None:
            arg = jnp.concatenate([ch + c for ch in _lane_chunks(arg)], axis=1)
        a = jnp.exp2(arg)
        if keep is not None:
            a = jnp.where(keep, a, 0.0)
        return jnp.dot(a.astype(BF16), v_ref[pl.ds(start, t), _head_lanes(u)], preferred_element_type=F32)

    def sweep(work, carried):
        gated, within, pv = {}, {}, {}
        for step in range(len(work) + 2):
            if step < len(work):
                gated[step] = gates(*work[step])
            if 1 <= step <= len(work):
                within[step - 1] = later_keys_sum(gated[step - 1][1])
            if step >= 2:
                n = step - 2
                u, j, _, _ = work[n]
                pv[n] = weighted_values(u, j, gated[n][0], within[n], gated[n][2], carried(n, gated))
        return gated, pv

    has_prev = jnp.where(i > 0, 2 * t, 0)
    j_prev = jnp.maximum(i - 1, 0)
    n_u = len(units)
    work = [(u, i, True, None) for u in units] + [(u, j_prev, False, has_prev) for u in units]
    gated, pv = sweep(work, lambda n, g: None if n < n_u else jnp.broadcast_to(g[n - n_u][3], (2 * t, LANES)))
    c_max = None
    for u in units:
        acc_ref[u] = pv[u] + pv[n_u + u]
        c2 = jnp.broadcast_to(gated[u][3], (2 * t, LANES)) + gated[n_u + u][3]
        c_ref[u] = c2
        c_max = jnp.max(c2) if c_max is None else jnp.maximum(c_max, jnp.max(c2))

    def cond(state):
        j, c_max = state
        return (j >= 0) & (c_max > SB_DEAD_LOG2)

    def body(state):
        j, _ = state
        cs = [c_ref[u] for u in units]
        gated, pv = sweep([(u, j, False, None) for u in units], lambda n, g: cs[n])
        c_max = None
        for u in units:
            acc_ref[u] += pv[u]
            c_new = cs[u] + gated[u][3]
            c_ref[u] = c_new
            c_max = jnp.max(c_new) if c_max is None else jnp.maximum(c_max, jnp.max(c_new))
        return j - 1, c_max

    lax.while_loop(cond, body, (i - 2, c_max))

    lane = lax.broadcasted_iota(I32, (t, LANES), 1)
    for hh in range(HEADS_PER_STEP):
        acc = acc_ref[hh]
        o_ref[:, _head_lanes(hh)] = jnp.where(lane < HEAD_DIM, acc[:t], acc[t:]).astype(BF16)


def _sb_attention(proj, batch, seq, d):
    n = batch * seq
    groups = d // (HEADS_PER_STEP * LANES)
    t = min(256, seq)
    nq = seq // t
    idx = jnp.arange(t)
    tri = (idx[:, None] > idx[None, :]).astype(BF16)
    q_spec, k_spec, v_spec = _attn_specs(t, seq, nq, groups)
    stat = pltpu.VMEM((HEADS_PER_STEP, 2 * t, LANES), F32)
    return pl.pallas_call(
        functools.partial(_sb_attn_kernel, t=t),
        out_shape=jax.ShapeDtypeStruct((n, d), BF16),
        grid=(batch, groups, nq),
        in_specs=[q_spec, k_spec, v_spec, pl.BlockSpec((t, t), lambda b, h, i: (0, 0))],
        out_specs=q_spec,
        scratch_shapes=[stat, stat],
        compiler_params=_cparams(("arbitrary", "arbitrary", "arbitrary")),
        name="sb_attention",
    )(proj, proj, proj, tri)


def _first_index_of(mask, lane, sentinel):
    return jnp.min(jnp.where(mask, lane, float(sentinel)), axis=1, keepdims=True)


def _top2(v, lane):
    neg = -jnp.inf
    m1 = jnp.max(v, axis=1, keepdims=True)
    i1 = _first_index_of(v == m1, lane, N_EXPERTS)
    v2 = jnp.where(lane == i1, neg, v)
    m2 = jnp.max(v2, axis=1, keepdims=True)
    i2 = _first_index_of(v2 == m2, lane, N_EXPERTS)
    return m1, i1, m2, i2


def _route(scores, sel):
    lane_i = lax.broadcasted_iota(I32, sel.shape, 1)
    lane = lane_i.astype(F32)
    grp = lax.shift_right_logical(lane_i, int(math.log2(GROUP_SIZE))).astype(F32)
    neg = -jnp.inf
    best = None
    g_best = None
    for g in range(N_GROUPS):
        m1, _, m2, _ = _top2(jnp.where(grp == float(g), sel, neg), lane)
        score = m1 + m2
        if g == 0:
            best, g_best = score, jnp.zeros(score.shape, F32)
        else:
            better = score > best
            g_best = jnp.where(better, float(g), g_best)
            best = jnp.where(better, score, best)
    _, e1, _, e2 = _top2(jnp.where(grp == g_best, sel, neg), lane)
    hot1, hot2 = lane == e1, lane == e2
    s1 = jnp.sum(jnp.where(hot1, scores, 0.0), axis=1, keepdims=True)
    s2 = jnp.sum(jnp.where(hot2, scores, 0.0), axis=1, keepdims=True)
    tot = s1 + s2
    return s1 / tot, s2 / tot, e1, e2, hot1, hot2


def _outproj_route_kernel(o_ref, x_ref, wo_ref, ga_ref, g_ref, sc_ref, sh_ref, wr_ref, br_ref, tril_ref,
                          xo_ref, h_ref, meta_ref, meta_t_ref, cnt_ref, *, ts, n_sub):
    @pl.when(pl.program_id(0) == 0)
    def _():
        cnt_ref[...] = jnp.zeros(cnt_ref.shape, F32)

    subs = range(n_sub)
    rows = [pl.ds(s * ts, ts) for s in subs]
    mix = [jnp.dot(o_ref[rows[s], :], wo_ref[...], preferred_element_type=F32) for s in subs]
    hs = []
    for s in subs:
        xn = x_ref[rows[s], :] + ga_ref[...] * mix[s]
        xo_ref[rows[s], :] = xn
        h = _rms(xn, g_ref[...]) * (1.0 + sc_ref[...]) + sh_ref[...]
        _store_rows(h_ref, s * ts, h)
        hs.append(h)

    logits = [_dot3(hs[s], wr_ref[...]) for s in subs]
    routed = []
    for s in subs:
        e = jnp.exp(logits[s] - jnp.max(logits[s], axis=1, keepdims=True))
        scores = e / jnp.sum(e, axis=1, keepdims=True)
        routed.append(_route(scores, scores + br_ref[...]))

    tril = tril_ref[...]
    ones = [(jnp.where(r[4], 1.0, 0.0), jnp.where(r[5], 1.0, 0.0)) for r in routed]
    prefix = [(jnp.dot(tril, o1.astype(BF16), preferred_element_type=F32),
               jnp.dot(tril, o2.astype(BF16), preferred_element_type=F32)) for o1, o2 in ones]
    base = cnt_ref[...]
    ml = lax.broadcasted_iota(I32, (ts, LANES), 1)
    for s in subs:
        g1, g2, e1, e2, hot1, hot2 = routed[s]
        tot1 = jnp.sum(ones[s][0], axis=0, keepdims=True)
        tot2 = jnp.sum(ones[s][1], axis=0, keepdims=True)
        r1 = jnp.sum(jnp.where(hot1, base + prefix[s][0], 0.0), axis=1, keepdims=True)
        r2 = jnp.sum(jnp.where(hot2, base + tot1 + prefix[s][1], 0.0), axis=1, keepdims=True)
        base = base + tot1 + tot2
        meta = jnp.where(ml == 0, g1, 0.0)
        for k, col in enumerate((g2, e1, e2, r1, r2)):
            meta = jnp.where(ml == k + 1, col, meta)
        meta_ref[rows[s], :] = meta
        meta_t_ref[:, rows[s]] = meta.T[:meta_t_ref.shape[0]]
    cnt_ref[...] = base


def _outproj_route(o, x, wo_bf16, ga, g, sc, sh, w_router, b_router, seq):
    n, d = x.shape
    tm = min(512, seq)
    ts = min(128, tm)
    per_b = seq // tm
    row = lambda i: (i, 0)
    bat = lambda i: (i // per_b, 0, 0)
    const = lambda i: (0, 0)
    idx = jnp.arange(ts)
    tril = (idx[None, :] < idx[:, None]).astype(BF16)
    return pl.pallas_call(
        functools.partial(_outproj_route_kernel, ts=ts, n_sub=tm // ts),
        out_shape=(
            jax.ShapeDtypeStruct((n, d), F32),
            jax.ShapeDtypeStruct((n * ROW_TILE, LANES), F32),
            jax.ShapeDtypeStruct((n, LANES), F32),
            jax.ShapeDtypeStruct((8, n), F32),
            jax.ShapeDtypeStruct((1, N_EXPERTS), F32),
        ),
        grid=(n // tm,),
        in_specs=[
            pl.BlockSpec((tm, d), row),
            pl.BlockSpec((tm, d), row),
            pl.BlockSpec((d, d), const),
            pl.BlockSpec((None, 1, d), bat),
            pl.BlockSpec((1, d), const),
            pl.BlockSpec((None, 1, d), bat),
            pl.BlockSpec((None, 1, d), bat),
            pl.BlockSpec((d, N_EXPERTS), const),
            pl.BlockSpec((1, N_EXPERTS), const),
            pl.BlockSpec((ts, ts), const),
        ],
        out_specs=(
            pl.BlockSpec((tm, d), row),
            pl.BlockSpec((tm * ROW_TILE, LANES), row),
            pl.BlockSpec((tm, LANES), row),
            pl.BlockSpec((8, tm), lambda i: (0, i)),
            pl.BlockSpec((1, N_EXPERTS), const),
        ),
        compiler_params=_cparams(("arbitrary",)),
        name="outproj_route",
    )(o, x, wo_bf16, ga, g.reshape(1, d), sc, sh, w_router, b_router.reshape(1, N_EXPERTS), tril)


def _row_copy(src, src_row, dst, dst_row, sem, rows=1):
    n = rows * ROW_TILE
    return pltpu.make_async_copy(src.at[pl.ds(pl.multiple_of(src_row * ROW_TILE, ROW_TILE), n)],
                                 dst.at[pl.ds(pl.multiple_of(dst_row * ROW_TILE, ROW_TILE), n)], sem)


def _load_rows(ref, start_row, rows):
    return jnp.concatenate([ref[pl.ds(start_row * ROW_TILE + k, rows, stride=ROW_TILE), :]
                            for k in range(ROW_TILE)], axis=1)


def _store_rows(ref, start_row, value):
    for k in range(ROW_TILE):
        ref[pl.ds(start_row * ROW_TILE + k, value.shape[0], stride=ROW_TILE), :] = value[:, k * LANES:(k + 1) * LANES]


def _dispatch_kernel(tail_ref, dest_hbm, h_ref, rows_hbm, dest_smem, stage, zero_buf, idx_sem, row_sems, *, tc, n_blk):
    c = pl.program_id(0)
    last = pl.num_programs(0) - 1

    @pl.when(c == 0)
    def _():
        zero_buf[...] = jnp.zeros(zero_buf.shape, F32)

        def fill(row):
            return _row_copy(zero_buf, 0, rows_hbm, row, row_sems.at[0], rows=EXPERT_ROWS)

        def fill_tail(e, carry):
            cp = fill(tail_ref[e])
            cp.start()
            cp.wait()
            return carry

        def start_unused_block(b, carry):
            fill(b * EXPERT_ROWS).start()
            return carry

        def wait_unused_block(b, carry):
            fill(b * EXPERT_ROWS).wait()
            return carry

        lax.fori_loop(0, N_EXPERTS, fill_tail, 0)
        lax.fori_loop(tail_ref[N_EXPERTS], n_blk, start_unused_block, 0)
        lax.fori_loop(tail_ref[N_EXPERTS], n_blk, wait_unused_block, 0)

    idx_copy = pltpu.make_async_copy(dest_hbm.at[pl.ds(c * 2 * tc, 2 * tc)], dest_smem, idx_sem)
    idx_copy.start()
    idx_copy.wait()

    def drain(slot):
        def body(tk, carry):
            _row_copy(stage.at[slot], 0, rows_hbm, 0, row_sems.at[slot]).wait()
            _row_copy(stage.at[slot], 0, rows_hbm, 0, row_sems.at[slot]).wait()
            return carry

        lax.fori_loop(0, tc, body, 0, unroll=8)

    for slot in range(2):
        @pl.when((c & 1) == slot)
        def _(slot=slot):
            stage[slot] = h_ref[...]

            def issue(tk, carry):
                _row_copy(stage.at[slot], tk, rows_hbm, dest_smem[tk], row_sems.at[slot]).start(priority=0)
                _row_copy(stage.at[slot], tk, rows_hbm, dest_smem[tc + tk], row_sems.at[slot]).start(priority=1)
                return carry

            lax.fori_loop(0, tc, issue, 0, unroll=8)

            @pl.when(c > 0)
            def _():
                drain(1 - slot)

            @pl.when(c == last)
            def _():
                drain(slot)


def _dispatch(tail, dest_flat, h_tiles, n_rows, tc):
    n = h_tiles.shape[0] // ROW_TILE
    any_spec = pl.BlockSpec(memory_space=pl.ANY)
    grid_spec = pltpu.PrefetchScalarGridSpec(
        num_scalar_prefetch=1,
        grid=(n // tc,),
        in_specs=[any_spec, pl.BlockSpec((tc * ROW_TILE, LANES), lambda i, tail: (i, 0))],
        out_specs=any_spec,
        scratch_shapes=[pltpu.SMEM((2 * tc,), I32), pltpu.VMEM((2, tc * ROW_TILE, LANES), F32),
                        pltpu.VMEM((EXPERT_ROWS * ROW_TILE, LANES), F32),
                        pltpu.SemaphoreType.DMA, pltpu.SemaphoreType.DMA((2,))],
    )
    return pl.pallas_call(
        functools.partial(_dispatch_kernel, tc=tc, n_blk=n_rows // EXPERT_ROWS),
        out_shape=jax.ShapeDtypeStruct((n_rows * ROW_TILE, LANES), F32),
        grid_spec=grid_spec,
        compiler_params=_cparams(("arbitrary",)),
        name="moe_dispatch",
    )(tail, dest_flat, h_tiles)


def _experts_kernel(blk_e_ref, n_used_ref, x_ref, wg_ref, wu_ref, wd_ref, y_ref, wg_s, wu_s, wd_s):
    i = pl.program_id(0)
    used = i < n_used_ref[0]
    prev = blk_e_ref[jnp.maximum(i - 1, 0)]
    new_expert = (i == 0) | (blk_e_ref[i] != prev)

    @pl.when(used & new_expert)
    def _():
        wg_s[...] = wg_ref[...].astype(BF16)
        wu_s[...] = wu_ref[...].astype(BF16)
        wd_s[...] = wd_ref[...].astype(BF16)

    @pl.when(used)
    def _():
        groups = range(EXPERT_ROWS // EXPERT_SUB_ROWS)
        dot = functools.partial(jnp.dot, preferred_element_type=F32)
        xb = [_load_rows(x_ref, s * EXPERT_SUB_ROWS, EXPERT_SUB_ROWS).astype(BF16) for s in groups]
        gate = [dot(xb[s], wg_s[...]) for s in groups]
        up = [dot(xb[s], wu_s[...]) for s in groups]
        hid = [(gate[s] / (1.0 + jnp.exp(-gate[s])) * up[s]).astype(BF16) for s in groups]
        for s in groups:
            _store_rows(y_ref, s * EXPERT_SUB_ROWS, dot(hid[s], wd_s[...]))

    @pl.when(jnp.logical_not(used))
    def _():
        y_ref[...] = jnp.zeros(y_ref.shape, F32)


def _experts(blk_e, n_used, x_tiles, w_gate, w_up, w_down, layer):
    d, f = w_gate.shape[-2:]
    n_blk = x_tiles.shape[0] // (EXPERT_ROWS * ROW_TILE)
    blk = (EXPERT_ROWS * ROW_TILE, LANES)
    last_used = lambda n_used_ref: jnp.maximum(n_used_ref[0] - 1, 0)
    x_map = lambda i, be, nu: (jnp.minimum(i, last_used(nu)), 0)
    w_map = lambda i, be, nu: (layer, be[jnp.minimum(i, last_used(nu))], 0, 0)
    grid_spec = pltpu.PrefetchScalarGridSpec(
        num_scalar_prefetch=2,
        grid=(n_blk,),
        in_specs=[
            pl.BlockSpec(blk, x_map),
            pl.BlockSpec((None, None, d, f), w_map),
            pl.BlockSpec((None, None, d, f), w_map),
            pl.BlockSpec((None, None, f, d), w_map),
        ],
        out_specs=pl.BlockSpec(blk, lambda i, be, nu: (i, 0)),
        scratch_shapes=[pltpu.VMEM((d, f), BF16), pltpu.VMEM((d, f), BF16), pltpu.VMEM((f, d), BF16)],
    )
    return pl.pallas_call(
        _experts_kernel,
        out_shape=jax.ShapeDtypeStruct(x_tiles.shape, F32),
        grid_spec=grid_spec,
        compiler_params=_cparams(("arbitrary",)),
        name="moe_experts",
    )(blk_e, n_used, x_tiles, w_gate, w_up, w_down)


def _combine_kernel(*refs, tc, final):
    if final:
        dest_hbm, y_hbm, x_ref, meta_ref, gm_ref, fg_ref, o_ref = refs[:7]
    else:
        dest_hbm, y_hbm, x_ref, meta_ref, gm_ref, o_ref = refs[:6]
    dest_a, dest_b, a0, a1, b0, b1, idx_sem, row_sems = refs[-8:]
    dest_smem = (dest_a, dest_b)
    bufs = ((a0, a1), (b0, b1))
    c = pl.program_id(0)
    last = pl.num_programs(0) - 1

    def fetch(step, slot):
        idx_copy = pltpu.make_async_copy(dest_hbm.at[pl.ds(step * 2 * tc, 2 * tc)], dest_smem[slot], idx_sem)
        idx_copy.start()
        idx_copy.wait()

        def issue(tk, carry):
            _row_copy(y_hbm, dest_smem[slot][tk], bufs[slot][0], tk, row_sems.at[slot]).start(priority=0)
            _row_copy(y_hbm, dest_smem[slot][tc + tk], bufs[slot][1], tk, row_sems.at[slot]).start(priority=1)
            return carry

        lax.fori_loop(0, tc, issue, 0, unroll=8)

    def drain(slot):
        def body(tk, carry):
            _row_copy(y_hbm, 0, bufs[slot][0], 0, row_sems.at[slot]).wait()
            _row_copy(y_hbm, 0, bufs[slot][1], 0, row_sems.at[slot]).wait()
            return carry

        lax.fori_loop(0, tc, body, 0, unroll=8)

    @pl.when(c == 0)
    def _():
        fetch(0, 0)

    for slot in range(2):
        @pl.when((c & 1) == slot)
        def _(slot=slot):
            @pl.when(c < last)
            def _():
                fetch(c + 1, 1 - slot)

            drain(slot)
            meta = meta_ref[...]
            y = meta[:, 0:1] * _load_rows(bufs[slot][0], 0, tc) + meta[:, 1:2] * _load_rows(bufs[slot][1], 0, tc)
            xn = x_ref[...] + gm_ref[...] * y
            if final:
                xn = _rms(xn, fg_ref[...])
            o_ref[...] = xn


def _combine(dest_flat, y_rows, x, meta, gm, final_g, seq, tc):
    n, d = x.shape
    per_b = seq // tc
    row = lambda i: (i, 0)
    any_spec = pl.BlockSpec(memory_space=pl.ANY)
    final = final_g is not None
    in_specs = [any_spec, any_spec, pl.BlockSpec((tc, d), row), pl.BlockSpec((tc, LANES), row),
                pl.BlockSpec((None, 1, d), lambda i: (i // per_b, 0, 0))]
    args = [dest_flat, y_rows, x, meta, gm]
    if final:
        in_specs.append(pl.BlockSpec((1, d), lambda i: (0, 0)))
        args.append(final_g.reshape(1, d))
    return pl.pallas_call(
        functools.partial(_combine_kernel, tc=tc, final=final),
        out_shape=jax.ShapeDtypeStruct((n, d), F32),
        grid=(n // tc,),
        in_specs=in_specs,
        out_specs=pl.BlockSpec((tc, d), row),
        scratch_shapes=[pltpu.SMEM((2 * tc,), I32)] * 2
        + [pltpu.VMEM((tc * ROW_TILE, LANES), F32)] * 4
        + [pltpu.SemaphoreType.DMA, pltpu.SemaphoreType.DMA((2,))],
        compiler_params=_cparams(("arbitrary",)),
        name="moe_combine_final" if final else "moe_combine",
    )(*args)


def _moe(h_tiles, meta, meta_t, counts, x, gm, w_gate, w_up, w_down, layer, final_g, seq):
    n, d = x.shape
    assert d == ROW_TILE * LANES
    tc = min(512, seq)
    n_rows = n * TOP_K + N_EXPERTS * EXPERT_ROWS
    n_blk = n_rows // EXPERT_ROWS
    cnt = counts[0].astype(I32)
    padded = ((cnt + EXPERT_ROWS - 1) // EXPERT_ROWS) * EXPERT_ROWS
    seg_end = jnp.cumsum(padded)
    seg_start = seg_end - padded
    n_used = (seg_end[-1:] // EXPERT_ROWS).astype(I32)
    blk_start = jnp.arange(n_blk, dtype=I32) * EXPERT_ROWS
    blk_e = jnp.minimum(jnp.sum(seg_end[None, :] <= blk_start[:, None], axis=1), N_EXPERTS - 1).astype(I32)
    expert = meta_t[2:4].astype(I32)
    rank = meta_t[4:6].astype(I32)
    hot = expert[:, :, None] == jnp.arange(N_EXPERTS, dtype=I32)
    dest = rank + jnp.sum(jnp.where(hot, seg_start, 0), axis=-1)
    dest_flat = dest.reshape(TOP_K, n // tc, tc).transpose(1, 0, 2).reshape(-1)

    tail = jnp.concatenate([jnp.minimum(seg_start + cnt, n_rows - EXPERT_ROWS), n_used]).astype(I32)

    x_rows = _dispatch(tail, dest_flat, h_tiles, n_rows, tc)
    y_rows = _experts(blk_e, n_used, x_rows, w_gate, w_up, w_down, layer)
    return _combine(dest_flat, y_rows, x, meta, gm, final_g, seq, tc)


def kernel(x, c, positions, w_ada, b_ada, norm_g, w_in, w_out, diff_lambda, diff_subln_g, w_router, b_router,
           w_gate, w_up, w_down, final_norm_g):
    batch, seq, d = x.shape
    depth = w_ada.shape[0]
    n = batch * seq
    mod = _modulation(c, w_ada, b_ada)
    xf = x.reshape(n, d)
    for i in range(depth):
        sh_a, sc_a, g_a, sh_m, sc_m, g_m = [m.reshape(batch, 1, d) for m in jnp.split(mod[i], 6, axis=-1)]
        diff = i % 2 == 0
        proj = _norm_proj(xf, norm_g[i, 0], sc_a, sh_a, w_in[i].astype(BF16), positions, diff, seq)
        if diff:
            o = _diff_attention(proj, diff_lambda[i // 2], diff_subln_g[i // 2], batch, seq, d, i)
        else:
            o = _sb_attention(proj, batch, seq, d)
        xf, h, meta, meta_t, counts = _outproj_route(o, xf, w_out[i].astype(BF16), g_a, norm_g[i, 1], sc_m, sh_m,
                                                     w_router, b_router, seq)
        final_g = final_norm_g if i == depth - 1 else None
        xf = _moe(h, meta, meta_t, counts, xf, g_m, w_gate, w_up, w_down, i, final_g, seq)
    return xf.reshape(batch, seq, d)
```

```python
import functools
import math

import jax
import jax.numpy as jnp
from jax import lax
from jax.experimental import pallas as pl
from jax.experimental.pallas import tpu as pltpu

F32 = jnp.float32
BF16 = jnp.bfloat16
I32 = jnp.int32

LANES = 128
HEAD_DIM = 64
ROPE_DIMS = HEAD_DIM // 4
ROPE_THETA = 500000.0
N_EXPERTS = 32
N_GROUPS = 4
GROUP_SIZE = N_EXPERTS // N_GROUPS
TOP_K = 2
NORM_EPS = 1e-6
EXPERT_ROWS = 512
EXPERT_SUB_ROWS = 256
HEADS_PER_STEP = 2
SB_GROUPS_PER_STEP = 4
DIFF_BLOCKS_PER_TRIP = 2
ROW_TILE = 8
SB_DEAD_LOG2 = -110.0 * math.log2(math.e)
MASK_VALUE = -1e30
VMEM_LIMIT = 56 * 1024 * 1024


def _cparams(semantics, vmem=VMEM_LIMIT):
    return pltpu.CompilerParams(dimension_semantics=semantics, vmem_limit_bytes=vmem)


def _split_bf16(a):
    hi = a.astype(BF16)
    lo = (a - hi.astype(F32)).astype(BF16)
    return hi, lo


def _dot3(a, b):
    ah, al = _split_bf16(a)
    bh, bl = _split_bf16(b)
    d = functools.partial(jnp.dot, preferred_element_type=F32)
    return d(ah, bh) + d(ah, bl) + d(al, bh)


def _rms(x, g):
    ms = jnp.mean(x * x, axis=-1, keepdims=True)
    return x * lax.rsqrt(ms + NORM_EPS) * g


def _mod_kernel(c_ref, w_ref, b_ref, o_ref):
    c = c_ref[...]
    c_act = c / (1.0 + jnp.exp(-c))
    o_ref[...] = _dot3(c_act, w_ref[...]) + b_ref[...]


def _modulation(c, w_ada, b_ada):
    depth, d, six_d = w_ada.shape
    b = c.shape[0]
    rows = 16
    c_pad = jnp.zeros((rows, d), F32).at[:b].set(c)
    tn = 1536
    out = pl.pallas_call(
        _mod_kernel,
        out_shape=jax.ShapeDtypeStruct((depth, rows, six_d), F32),
        grid=(depth, six_d // tn),
        in_specs=[
            pl.BlockSpec((rows, d), lambda l, j: (0, 0)),
            pl.BlockSpec((None, d, tn), lambda l, j: (l, 0, j)),
            pl.BlockSpec((None, 1, tn), lambda l, j: (l, 0, j)),
        ],
        out_specs=pl.BlockSpec((None, rows, tn), lambda l, j: (l, 0, j)),
        compiler_params=_cparams(("arbitrary", "arbitrary")),
        name="adaln_mod",
    )(c_pad, w_ada, b_ada.reshape(depth, 1, six_d))
    return out[:, :b]


Q_SCALE = HEAD_DIM ** -0.5 * math.log2(math.e)


def _proj_kernel(*refs, d, rope):
    if rope:
        x_ref, g_ref, sc_ref, sh_ref, w_ref, pos_ref, pat_ref, o_ref = refs
    else:
        x_ref, g_ref, sc_ref, sh_ref, w_ref, o_ref = refs
    h = _rms(x_ref[...], g_ref[...]) * (1.0 + sc_ref[...]) + sh_ref[...]
    p = jnp.dot(h.astype(BF16), w_ref[...], preferred_element_type=F32)
    if rope:
        ang = pos_ref[...].astype(F32) * pat_ref[...]
        cs, sn = jnp.cos(ang), jnp.sin(ang)
        lane = lax.broadcasted_iota(I32, ang.shape, 1) & (HEAD_DIM - 1)
        half = ROPE_DIMS // 2
        s_lo = jnp.where(lane < half, -sn, 0.0)
        s_hi = jnp.where((lane >= half) & (lane < ROPE_DIMS), sn, 0.0)
    for j in range(2 * d // LANES):
        blk = p[:, j * LANES:(j + 1) * LANES]
        if rope:
            blk = (blk * cs + pltpu.roll(blk, half, 1) * s_hi
                   + pltpu.roll(blk, LANES - half, 1) * s_lo)
        if j < d // LANES:
            blk = blk * Q_SCALE
        o_ref[:, j * LANES:(j + 1) * LANES] = blk.astype(BF16)
    o_ref[:, 2 * d:] = p[:, 2 * d:].astype(BF16)


def _norm_proj(x, g, sc, sh, w_bf16, positions, rope, seq):
    n, d = x.shape
    tm = min(512, seq)
    per_b = seq // tm
    row = lambda i: (i, 0)
    bat = lambda i: (i // per_b, 0, 0)
    in_specs = [
        pl.BlockSpec((tm, d), row),
        pl.BlockSpec((1, d), lambda i: (0, 0)),
        pl.BlockSpec((None, 1, d), bat),
        pl.BlockSpec((None, 1, d), bat),
        pl.BlockSpec((d, 3 * d), lambda i: (0, 0)),
    ]
    args = [x, g.reshape(1, d), sc, sh, w_bf16]
    if rope:
        half = ROPE_DIMS // 2
        inv_freq = jnp.power(jnp.float32(ROPE_THETA), -jnp.arange(half, dtype=F32) * (2.0 / ROPE_DIMS))
        lane = jnp.arange(LANES) % HEAD_DIM
        pat = jnp.where(lane < ROPE_DIMS, inv_freq[lane % half], 0.0).astype(F32).reshape(1, LANES)
        in_specs += [pl.BlockSpec((tm, 1), row), pl.BlockSpec((1, LANES), lambda i: (0, 0))]
        args += [positions.reshape(n, 1), pat]
    return pl.pallas_call(
        functools.partial(_proj_kernel, d=d, rope=rope),
        out_shape=jax.ShapeDtypeStruct((n, 3 * d), BF16),
        grid=(n // tm,),
        in_specs=in_specs,
        out_specs=pl.BlockSpec((tm, 3 * d), row),
        compiler_params=_cparams(("arbitrary",)),
        name="norm_proj_rope" if rope else "norm_proj",
    )(*args)


def _stack_halves(q):
    lane = lax.broadcasted_iota(I32, q.shape, 1)
    zero = jnp.zeros_like(q)
    return jnp.concatenate([jnp.where(lane < HEAD_DIM, q, zero), jnp.where(lane >= HEAD_DIM, q, zero)], axis=0)


def _qk(qs, kb):
    return lax.dot_general(qs, kb, (((1,), (1,)), ((), ())), preferred_element_type=F32)


def _head_lanes(hh):
    return slice(hh * LANES, (hh + 1) * LANES)


def _lane_chunks(x):
    return [x[:, c * LANES:(c + 1) * LANES] for c in range(x.shape[1] // LANES)]


def _block_positions(shape, t, i, j):
    r = lax.broadcasted_iota(I32, shape, 0)
    qpos = i * t + jnp.where(r >= t, r - t, r)
    kpos = j * t + lax.broadcasted_iota(I32, shape, 1)
    return qpos, kpos


def _diff_attn_kernel(lam_ref, g_ref, q_ref, k_ref, v_ref, o_ref, m_ref, l_ref, acc_ref, *, t, lam_init):
    i = pl.program_id(2)
    qs = [_stack_halves(q_ref[:, _head_lanes(hh)]) for hh in range(HEADS_PER_STEP)]
    m_ref[...] = jnp.full(m_ref.shape, MASK_VALUE, F32)
    l_ref[...] = jnp.zeros(l_ref.shape, F32)
    acc_ref[...] = jnp.zeros(acc_ref.shape, F32)

    def scores(hh, j):
        start = pl.multiple_of(j * t, t)
        return _qk(qs[hh], k_ref[pl.ds(start, t), _head_lanes(hh)])

    def softmax_pv(hh, s, j, masked):
        start = pl.multiple_of(j * t, t)
        if masked:
            qpos, kpos = _block_positions(s.shape, t, i, j)
            s = jnp.where(kpos <= qpos, s, MASK_VALUE)
        chunks = _lane_chunks(s)
        m_old = m_ref[hh]
        m_new = jnp.maximum(m_old, jnp.max(functools.reduce(jnp.maximum, chunks), axis=1, keepdims=True))
        alpha = jnp.exp2(m_old - m_new)
        ps = [jnp.exp2(ch - m_new) for ch in chunks]
        l_ref[hh] = l_ref[hh] * alpha + functools.reduce(jnp.add, ps)
        p = jnp.concatenate([x.astype(BF16) for x in ps], axis=1)
        pv = jnp.dot(p, v_ref[pl.ds(start, t), _head_lanes(hh)], preferred_element_type=F32)
        acc_ref[hh] = acc_ref[hh] * alpha + pv
        m_ref[hh] = m_new

    def block_run(js, last_masked):
        work = [(hh, j, last_masked and n == len(js) - 1) for n, j in enumerate(js) for hh in range(HEADS_PER_STEP)]
        s_next = scores(work[0][0], work[0][1])
        for n, (hh, j, masked) in enumerate(work):
            s_cur = s_next
            if n + 1 < len(work):
                s_next = scores(work[n + 1][0], work[n + 1][1])
            softmax_pv(hh, s_cur, j, masked)

    run = DIFF_BLOCKS_PER_TRIP
    shift = run.bit_length() - 1

    def off_diag_run(p, carry):
        block_run([run * p + n for n in range(run)], False)
        return carry

    lax.fori_loop(0, lax.shift_right_logical(i, shift), off_diag_run, 0)

    rem = i & (run - 1)
    for r in range(run):
        @pl.when(rem == r)
        def _(r=r):
            block_run([i - r + n for n in range(r + 1)], True)

    lam = lam_ref[...]
    lam_full = (jnp.exp(jnp.sum(lam[0:1] * lam[1:2], axis=1, keepdims=True))
                - jnp.exp(jnp.sum(lam[2:3] * lam[3:4], axis=1, keepdims=True)) + lam_init)
    for hh in range(HEADS_PER_STEP):
        o_all = acc_ref[hh] / jnp.sum(l_ref[hh], axis=1, keepdims=True)
        o = o_all[:t] - lam_full * o_all[t:]
        o_ref[:, _head_lanes(hh)] = (_rms(o, g_ref[...]) * (1.0 - lam_init)).astype(BF16)


def _attn_specs(t, seq, nq, groups, per_step):
    w = per_step * LANES
    q_spec = pl.BlockSpec((t, w), lambda b, h, i: (b * nq + i, h))
    k_spec = pl.BlockSpec((seq, w), lambda b, h, i: (b, groups + h))
    v_spec = pl.BlockSpec((seq, w), lambda b, h, i: (b, 2 * groups + h))
    return q_spec, k_spec, v_spec


def _diff_attention(proj, lam, subln_g, batch, seq, d, layer_idx):
    n = batch * seq
    groups = d // (HEADS_PER_STEP * LANES)
    t = min(512, seq)
    nq = seq // t
    lam_init = 0.8 - 0.6 * math.exp(-0.3 * layer_idx)
    q_spec, k_spec, v_spec = _attn_specs(t, seq, nq, groups, HEADS_PER_STEP)
    stat = pltpu.VMEM((HEADS_PER_STEP, 2 * t, LANES), F32)
    return pl.pallas_call(
        functools.partial(_diff_attn_kernel, t=t, lam_init=lam_init),
        out_shape=jax.ShapeDtypeStruct((n, d), BF16),
        grid=(batch, groups, nq),
        in_specs=[
            pl.BlockSpec((4, HEAD_DIM), lambda b, h, i: (0, 0)),
            pl.BlockSpec((1, LANES), lambda b, h, i: (0, 0)),
            q_spec, k_spec, v_spec,
        ],
        out_specs=q_spec,
        scratch_shapes=[stat, stat, stat],
        compiler_params=_cparams(("arbitrary", "arbitrary", "arbitrary")),
        name="diff_attention",
    )(lam, subln_g.reshape(1, LANES), proj, proj, proj)


def _sb_attn_kernel(q_ref, k_ref, v_ref, tri_ref, o_ref, c_ref, acc_ref, *, t):
    i = pl.program_id(2)
    units = list(range(SB_GROUPS_PER_STEP))
    qs = [_stack_halves(q_ref[:, _head_lanes(u)]) for u in units]

    def gates(u, j, diag, guard):
        start = pl.multiple_of(j * t, t)
        z = _qk(qs[u], k_ref[pl.ds(start, t), _head_lanes(u)])
        log_beta = jnp.minimum(z, 0.0) - jnp.log2(1.0 + jnp.exp2(-jnp.abs(z)))
        log_rest = log_beta - z
        keep = None
        if diag:
            qpos, kpos = _block_positions(z.shape, t, i, j)
            keep = kpos < qpos
        if guard is not None:
            keep = lax.broadcasted_iota(I32, z.shape, 0) < guard
        if keep is not None:
            log_rest = jnp.where(keep, log_rest, 0.0)
        row_sum = jnp.sum(functools.reduce(jnp.add, _lane_chunks(log_rest)), axis=1, keepdims=True)
        return log_beta, log_rest, keep, row_sum

    def later_keys_sum(log_rest):
        return jnp.dot(log_rest.astype(BF16), tri_ref[...], preferred_element_type=F32)

    def weighted_values(u, j, log_beta, within, keep, c):
        start = pl.multiple_of(j * t, t)
        arg = log_beta + within
        if c is not None:
            arg = jnp.concatenate([ch + c for ch in _lane_chunks(arg)], axis=1)
        a = jnp.exp2(arg)
        if keep is not None:
            a = jnp.where(keep, a, 0.0)
        return jnp.dot(a.astype(BF16), v_ref[pl.ds(start, t), _head_lanes(u)], preferred_element_type=F32)

    def sweep(work, carried):
        gated, within, pv = {}, {}, {}
        for step in range(len(work) + 2):
            if step < len(work):
                gated[step] = gates(*work[step])
            if 1 <= step <= len(work):
                within[step - 1] = later_keys_sum(gated[step - 1][1])
            if step >= 2:
                n = step - 2
                u, j, _, _ = work[n]
                pv[n] = weighted_values(u, j, gated[n][0], within[n], gated[n][2], carried(n, gated))
        return gated, pv

    has_prev = jnp.where(i > 0, 2 * t, 0)
    j_prev = jnp.maximum(i - 1, 0)
    n_u = len(units)
    work = [(u, i, True, None) for u in units] + [(u, j_prev, False, has_prev) for u in units]
    gated, pv = sweep(work, lambda n, g: None if n < n_u else jnp.broadcast_to(g[n - n_u][3], (2 * t, LANES)))
    c_max = None
    for u in units:
        acc_ref[u] = pv[u] + pv[n_u + u]
        c2 = jnp.broadcast_to(gated[u][3], (2 * t, LANES)) + gated[n_u + u][3]
        c_ref[u] = c2
        c_max = jnp.max(c2) if c_max is None else jnp.maximum(c_max, jnp.max(c2))

    def cond(state):
        j, c_max = state
        return (j >= 0) & (c_max > SB_DEAD_LOG2)

    def body(state):
        j, _ = state
        cs = [c_ref[u] for u in units]
        gated, pv = sweep([(u, j, False, None) for u in units], lambda n, g: cs[n])
        c_max = None
        for u in units:
            acc_ref[u] += pv[u]
            c_new = cs[u] + gated[u][3]
            c_ref[u] = c_new
            c_max = jnp.max(c_new) if c_max is None else jnp.maximum(c_max, jnp.max(c_new))
        return j - 1, c_max

    lax.while_loop(cond, body, (i - 2, c_max))

    lane = lax.broadcasted_iota(I32, (t, LANES), 1)
    for hh in units:
        acc = acc_ref[hh]
        o_ref[:, _head_lanes(hh)] = jnp.where(lane < HEAD_DIM, acc[:t], acc[t:]).astype(BF16)


def _sb_attention(proj, batch, seq, d):
    n = batch * seq
    groups = d // (SB_GROUPS_PER_STEP * LANES)
    t = min(256, seq)
    nq = seq // t
    idx = jnp.arange(t)
    tri = (idx[:, None] > idx[None, :]).astype(BF16)
    q_spec, k_spec, v_spec = _attn_specs(t, seq, nq, groups, SB_GROUPS_PER_STEP)
    stat = pltpu.VMEM((SB_GROUPS_PER_STEP, 2 * t, LANES), F32)
    return pl.pallas_call(
        functools.partial(_sb_attn_kernel, t=t),
        out_shape=jax.ShapeDtypeStruct((n, d), BF16),
        grid=(batch, groups, nq),
        in_specs=[q_spec, k_spec, v_spec, pl.BlockSpec((t, t), lambda b, h, i: (0, 0))],
        out_specs=q_spec,
        scratch_shapes=[stat, stat],
        compiler_params=_cparams(("arbitrary", "arbitrary", "arbitrary")),
        name="sb_attention",
    )(proj, proj, proj, tri)


def _first_index_of(mask, lane, sentinel):
    return jnp.min(jnp.where(mask, lane, float(sentinel)), axis=1, keepdims=True)


def _top2(v, lane):
    neg = -jnp.inf
    m1 = jnp.max(v, axis=1, keepdims=True)
    i1 = _first_index_of(v == m1, lane, N_EXPERTS)
    v2 = jnp.where(lane == i1, neg, v)
    m2 = jnp.max(v2, axis=1, keepdims=True)
    i2 = _first_index_of(v2 == m2, lane, N_EXPERTS)
    return m1, i1, m2, i2


def _route(scores, sel):
    lane_i = lax.broadcasted_iota(I32, sel.shape, 1)
    lane = lane_i.astype(F32)
    grp = lax.shift_right_logical(lane_i, int(math.log2(GROUP_SIZE))).astype(F32)
    neg = -jnp.inf
    best = None
    g_best = None
    for g in range(N_GROUPS):
        m1, _, m2, _ = _top2(jnp.where(grp == float(g), sel, neg), lane)
        score = m1 + m2
        if g == 0:
            best, g_best = score, jnp.zeros(score.shape, F32)
        else:
            better = score > best
            g_best = jnp.where(better, float(g), g_best)
            best = jnp.where(better, score, best)
    _, e1, _, e2 = _top2(jnp.where(grp == g_best, sel, neg), lane)
    hot1, hot2 = lane == e1, lane == e2
    s1 = jnp.sum(jnp.where(hot1, scores, 0.0), axis=1, keepdims=True)
    s2 = jnp.sum(jnp.where(hot2, scores, 0.0), axis=1, keepdims=True)
    tot = s1 + s2
    return s1 / tot, s2 / tot, e1, e2, hot1, hot2


def _outproj_route_kernel(o_ref, x_ref, wo_ref, ga_ref, g_ref, sc_ref, sh_ref, wr_ref, br_ref, tril_ref,
                          xo_ref, h_ref, meta_ref, meta_t_ref, cnt_ref, *, ts, n_sub):
    @pl.when(pl.program_id(0) == 0)
    def _():
        cnt_ref[...] = jnp.zeros(cnt_ref.shape, F32)

    subs = range(n_sub)
    rows = [pl.ds(s * ts, ts) for s in subs]
    mix = [jnp.dot(o_ref[rows[s], :], wo_ref[...], preferred_element_type=F32) for s in subs]
    hs = []
    for s in subs:
        xn = x_ref[rows[s], :] + ga_ref[...] * mix[s]
        xo_ref[rows[s], :] = xn
        h = _rms(xn, g_ref[...]) * (1.0 + sc_ref[...]) + sh_ref[...]
        _store_rows(h_ref, s * ts, h)
        hs.append(h)

    logits = [_dot3(hs[s], wr_ref[...]) for s in subs]
    routed = []
    for s in subs:
        e = jnp.exp(logits[s] - jnp.max(logits[s], axis=1, keepdims=True))
        scores = e / jnp.sum(e, axis=1, keepdims=True)
        routed.append(_route(scores, scores + br_ref[...]))

    tril = tril_ref[...]
    ones = [(jnp.where(r[4], 1.0, 0.0), jnp.where(r[5], 1.0, 0.0)) for r in routed]
    prefix = [(jnp.dot(tril, o1.astype(BF16), preferred_element_type=F32),
               jnp.dot(tril, o2.astype(BF16), preferred_element_type=F32)) for o1, o2 in ones]
    base = cnt_ref[...]
    ml = lax.broadcasted_iota(I32, (ts, LANES), 1)
    for s in subs:
        g1, g2, e1, e2, hot1, hot2 = routed[s]
        tot1 = jnp.sum(ones[s][0], axis=0, keepdims=True)
        tot2 = jnp.sum(ones[s][1], axis=0, keepdims=True)
        r1 = jnp.sum(jnp.where(hot1, base + prefix[s][0], 0.0), axis=1, keepdims=True)
        r2 = jnp.sum(jnp.where(hot2, base + tot1 + prefix[s][1], 0.0), axis=1, keepdims=True)
        base = base + tot1 + tot2
        meta = jnp.where(ml == 0, g1, 0.0)
        for k, col in enumerate((g2, e1, e2, r1, r2)):
            meta = jnp.where(ml == k + 1, col, meta)
        meta_ref[rows[s], :] = meta
        meta_t_ref[:, rows[s]] = meta.T[:meta_t_ref.shape[0]]
    cnt_ref[...] = base


def _outproj_route(o, x, wo_bf16, ga, g, sc, sh, w_router, b_router, seq):
    n, d = x.shape
    tm = min(512, seq)
    ts = min(128, tm)
    per_b = seq // tm
    row = lambda i: (i, 0)
    bat = lambda i: (i // per_b, 0, 0)
    const = lambda i: (0, 0)
    idx = jnp.arange(ts)
    tril = (idx[None, :] < idx[:, None]).astype(BF16)
    return pl.pallas_call(
        functools.partial(_outproj_route_kernel, ts=ts, n_sub=tm // ts),
        out_shape=(
            jax.ShapeDtypeStruct((n, d), F32),
            jax.ShapeDtypeStruct((n * ROW_TILE, LANES), F32),
            jax.ShapeDtypeStruct((n, LANES), F32),
            jax.ShapeDtypeStruct((8, n), F32),
            jax.ShapeDtypeStruct((1, N_EXPERTS), F32),
        ),
        grid=(n // tm,),
        in_specs=[
            pl.BlockSpec((tm, d), row),
            pl.BlockSpec((tm, d), row),
            pl.BlockSpec((d, d), const),
            pl.BlockSpec((None, 1, d), bat),
            pl.BlockSpec((1, d), const),
            pl.BlockSpec((None, 1, d), bat),
            pl.BlockSpec((None, 1, d), bat),
            pl.BlockSpec((d, N_EXPERTS), const),
            pl.BlockSpec((1, N_EXPERTS), const),
            pl.BlockSpec((ts, ts), const),
        ],
        out_specs=(
            pl.BlockSpec((tm, d), row),
            pl.BlockSpec((tm * ROW_TILE, LANES), row),
            pl.BlockSpec((tm, LANES), row),
            pl.BlockSpec((8, tm), lambda i: (0, i)),
            pl.BlockSpec((1, N_EXPERTS), const),
        ),
        compiler_params=_cparams(("arbitrary",)),
        name="outproj_route",
    )(o, x, wo_bf16, ga, g.reshape(1, d), sc, sh, w_router, b_router.reshape(1, N_EXPERTS), tril)


def _row_copy(src, src_row, dst, dst_row, sem, rows=1):
    n = rows * ROW_TILE
    return pltpu.make_async_copy(src.at[pl.ds(pl.multiple_of(src_row * ROW_TILE, ROW_TILE), n)],
                                 dst.at[pl.ds(pl.multiple_of(dst_row * ROW_TILE, ROW_TILE), n)], sem)


def _load_rows(ref, start_row, rows):
    return jnp.concatenate([ref[pl.ds(start_row * ROW_TILE + k, rows, stride=ROW_TILE), :]
                            for k in range(ROW_TILE)], axis=1)


def _store_rows(ref, start_row, value):
    for k in range(ROW_TILE):
        ref[pl.ds(start_row * ROW_TILE + k, value.shape[0], stride=ROW_TILE), :] = value[:, k * LANES:(k + 1) * LANES]


def _dispatch_kernel(tail_ref, dest_hbm, h_ref, rows_hbm, dest_smem, stage, zero_buf, idx_sem, row_sems, *, tc, n_blk):
    c = pl.program_id(0)
    last = pl.num_programs(0) - 1

    @pl.when(c == 0)
    def _():
        zero_buf[...] = jnp.zeros(zero_buf.shape, F32)

        def fill(row):
            return _row_copy(zero_buf, 0, rows_hbm, row, row_sems.at[0], rows=EXPERT_ROWS)

        def fill_tail(e, carry):
            cp = fill(tail_ref[e])
            cp.start()
            cp.wait()
            return carry

        def start_unused_block(b, carry):
            fill(b * EXPERT_ROWS).start()
            return carry

        def wait_unused_block(b, carry):
            fill(b * EXPERT_ROWS).wait()
            return carry

        lax.fori_loop(0, N_EXPERTS, fill_tail, 0)
        lax.fori_loop(tail_ref[N_EXPERTS], n_blk, start_unused_block, 0)
        lax.fori_loop(tail_ref[N_EXPERTS], n_blk, wait_unused_block, 0)

    idx_copy = pltpu.make_async_copy(dest_hbm.at[pl.ds(c * 2 * tc, 2 * tc)], dest_smem, idx_sem)
    idx_copy.start()
    idx_copy.wait()

    def drain(slot):
        def body(tk, carry):
            _row_copy(stage.at[slot], 0, rows_hbm, 0, row_sems.at[slot]).wait()
            _row_copy(stage.at[slot], 0, rows_hbm, 0, row_sems.at[slot]).wait()
            return carry

        lax.fori_loop(0, tc, body, 0, unroll=8)

    for slot in range(2):
        @pl.when((c & 1) == slot)
        def _(slot=slot):
            stage[slot] = h_ref[...]

            def issue(tk, carry):
                _row_copy(stage.at[slot], tk, rows_hbm, dest_smem[tk], row_sems.at[slot]).start(priority=0)
                _row_copy(stage.at[slot], tk, rows_hbm, dest_smem[tc + tk], row_sems.at[slot]).start(priority=1)
                return carry

            lax.fori_loop(0, tc, issue, 0, unroll=8)

            @pl.when(c > 0)
            def _():
                drain(1 - slot)

            @pl.when(c == last)
            def _():
                drain(slot)


def _dispatch(tail, dest_flat, h_tiles, n_rows, tc):
    n = h_tiles.shape[0] // ROW_TILE
    any_spec = pl.BlockSpec(memory_space=pl.ANY)
    grid_spec = pltpu.PrefetchScalarGridSpec(
        num_scalar_prefetch=1,
        grid=(n // tc,),
        in_specs=[any_spec, pl.BlockSpec((tc * ROW_TILE, LANES), lambda i, tail: (i, 0))],
        out_specs=any_spec,
        scratch_shapes=[pltpu.SMEM((2 * tc,), I32), pltpu.VMEM((2, tc * ROW_TILE, LANES), F32),
                        pltpu.VMEM((EXPERT_ROWS * ROW_TILE, LANES), F32),
                        pltpu.SemaphoreType.DMA, pltpu.SemaphoreType.DMA((2,))],
    )
    return pl.pallas_call(
        functools.partial(_dispatch_kernel, tc=tc, n_blk=n_rows // EXPERT_ROWS),
        out_shape=jax.ShapeDtypeStruct((n_rows * ROW_TILE, LANES), F32),
        grid_spec=grid_spec,
        compiler_params=_cparams(("arbitrary",)),
        name="moe_dispatch",
    )(tail, dest_flat, h_tiles)


def _experts_kernel(blk_e_ref, n_used_ref, x_ref, wg_ref, wu_ref, wd_ref, y_ref, wg_s, wu_s, wd_s):
    i = pl.program_id(0)
    used = i < n_used_ref[0]
    prev = blk_e_ref[jnp.maximum(i - 1, 0)]
    new_expert = (i == 0) | (blk_e_ref[i] != prev)

    @pl.when(used & new_expert)
    def _():
        wg_s[...] = wg_ref[...].astype(BF16)
        wu_s[...] = wu_ref[...].astype(BF16)
        wd_s[...] = wd_ref[...].astype(BF16)

    @pl.when(used)
    def _():
        groups = range(EXPERT_ROWS // EXPERT_SUB_ROWS)
        dot = functools.partial(jnp.dot, preferred_element_type=F32)
        xb = [_load_rows(x_ref, s * EXPERT_SUB_ROWS, EXPERT_SUB_ROWS).astype(BF16) for s in groups]
        gate = [dot(xb[s], wg_s[...]) for s in groups]
        up = [dot(xb[s], wu_s[...]) for s in groups]
        hid = [(gate[s] / (1.0 + jnp.exp(-gate[s])) * up[s]).astype(BF16) for s in groups]
        for s in groups:
            _store_rows(y_ref, s * EXPERT_SUB_ROWS, dot(hid[s], wd_s[...]))

    @pl.when(jnp.logical_not(used))
    def _():
        y_ref[...] = jnp.zeros(y_ref.shape, F32)


def _experts(blk_e, n_used, x_tiles, w_gate, w_up, w_down, layer):
    d, f = w_gate.shape[-2:]
    n_blk = x_tiles.shape[0] // (EXPERT_ROWS * ROW_TILE)
    blk = (EXPERT_ROWS * ROW_TILE, LANES)
    last_used = lambda n_used_ref: jnp.maximum(n_used_ref[0] - 1, 0)
    x_map = lambda i, be, nu: (jnp.minimum(i, last_used(nu)), 0)
    w_map = lambda i, be, nu: (layer, be[jnp.minimum(i, last_used(nu))], 0, 0)
    grid_spec = pltpu.PrefetchScalarGridSpec(
        num_scalar_prefetch=2,
        grid=(n_blk,),
        in_specs=[
            pl.BlockSpec(blk, x_map),
            pl.BlockSpec((None, None, d, f), w_map),
            pl.BlockSpec((None, None, d, f), w_map),
            pl.BlockSpec((None, None, f, d), w_map),
        ],
        out_specs=pl.BlockSpec(blk, lambda i, be, nu: (i, 0)),
        scratch_shapes=[pltpu.VMEM((d, f), BF16), pltpu.VMEM((d, f), BF16), pltpu.VMEM((f, d), BF16)],
    )
    return pl.pallas_call(
        _experts_kernel,
        out_shape=jax.ShapeDtypeStruct(x_tiles.shape, F32),
        grid_spec=grid_spec,
        compiler_params=_cparams(("arbitrary",)),
        name="moe_experts",
    )(blk_e, n_used, x_tiles, w_gate, w_up, w_down)


def _combine_kernel(*refs, tc, final):
    if final:
        dest_hbm, y_hbm, x_ref, meta_ref, gm_ref, fg_ref, o_ref = refs[:7]
    else:
        dest_hbm, y_hbm, x_ref, meta_ref, gm_ref, o_ref = refs[:6]
    dest_a, dest_b, a0, a1, b0, b1, idx_sem, row_sems = refs[-8:]
    dest_smem = (dest_a, dest_b)
    bufs = ((a0, a1), (b0, b1))
    c = pl.program_id(0)
    last = pl.num_programs(0) - 1

    def fetch(step, slot):
        idx_copy = pltpu.make_async_copy(dest_hbm.at[pl.ds(step * 2 * tc, 2 * tc)], dest_smem[slot], idx_sem)
        idx_copy.start()
        idx_copy.wait()

        def issue(tk, carry):
            _row_copy(y_hbm, dest_smem[slot][tk], bufs[slot][0], tk, row_sems.at[slot]).start(priority=0)
            _row_copy(y_hbm, dest_smem[slot][tc + tk], bufs[slot][1], tk, row_sems.at[slot]).start(priority=1)
            return carry

        lax.fori_loop(0, tc, issue, 0, unroll=8)

    def drain(slot):
        def body(tk, carry):
            _row_copy(y_hbm, 0, bufs[slot][0], 0, row_sems.at[slot]).wait()
            _row_copy(y_hbm, 0, bufs[slot][1], 0, row_sems.at[slot]).wait()
            return carry

        lax.fori_loop(0, tc, body, 0, unroll=8)

    @pl.when(c == 0)
    def _():
        fetch(0, 0)

    for slot in range(2):
        @pl.when((c & 1) == slot)
        def _(slot=slot):
            @pl.when(c < last)
            def _():
                fetch(c + 1, 1 - slot)

            drain(slot)
            meta = meta_ref[...]
            y = meta[:, 0:1] * _load_rows(bufs[slot][0], 0, tc) + meta[:, 1:2] * _load_rows(bufs[slot][1], 0, tc)
            xn = x_ref[...] + gm_ref[...] * y
            if final:
                xn = _rms(xn, fg_ref[...])
            o_ref[...] = xn


def _combine(dest_flat, y_rows, x, meta, gm, final_g, seq, tc):
    n, d = x.shape
    per_b = seq // tc
    row = lambda i: (i, 0)
    any_spec = pl.BlockSpec(memory_space=pl.ANY)
    final = final_g is not None
    in_specs = [any_spec, any_spec, pl.BlockSpec((tc, d), row), pl.BlockSpec((tc, LANES), row),
                pl.BlockSpec((None, 1, d), lambda i: (i // per_b, 0, 0))]
    args = [dest_flat, y_rows, x, meta, gm]
    if final:
        in_specs.append(pl.BlockSpec((1, d), lambda i: (0, 0)))
        args.append(final_g.reshape(1, d))
    return pl.pallas_call(
        functools.partial(_combine_kernel, tc=tc, final=final),
        out_shape=jax.ShapeDtypeStruct((n, d), F32),
        grid=(n // tc,),
        in_specs=in_specs,
        out_specs=pl.BlockSpec((tc, d), row),
        scratch_shapes=[pltpu.SMEM((2 * tc,), I32)] * 2
        + [pltpu.VMEM((tc * ROW_TILE, LANES), F32)] * 4
        + [pltpu.SemaphoreType.DMA, pltpu.SemaphoreType.DMA((2,))],
        compiler_params=_cparams(("arbitrary",)),
        name="moe_combine_final" if final else "moe_combine",
    )(*args)


def _moe(h_tiles, meta, meta_t, counts, x, gm, w_gate, w_up, w_down, layer, final_g, seq):
    n, d = x.shape
    assert d == ROW_TILE * LANES
    tc = min(512, seq)
    n_rows = n * TOP_K + N_EXPERTS * EXPERT_ROWS
    n_blk = n_rows // EXPERT_ROWS
    cnt = counts[0].astype(I32)
    padded = ((cnt + EXPERT_ROWS - 1) // EXPERT_ROWS) * EXPERT_ROWS
    seg_end = jnp.cumsum(padded)
    seg_start = seg_end - padded
    n_used = (seg_end[-1:] // EXPERT_ROWS).astype(I32)
    blk_start = jnp.arange(n_blk, dtype=I32) * EXPERT_ROWS
    blk_e = jnp.minimum(jnp.sum(seg_end[None, :] <= blk_start[:, None], axis=1), N_EXPERTS - 1).astype(I32)
    expert = meta_t[2:4].astype(I32)
    rank = meta_t[4:6].astype(I32)
    hot = expert[:, :, None] == jnp.arange(N_EXPERTS, dtype=I32)
    dest = rank + jnp.sum(jnp.where(hot, seg_start, 0), axis=-1)
    dest_flat = dest.reshape(TOP_K, n // tc, tc).transpose(1, 0, 2).reshape(-1)

    tail = jnp.concatenate([jnp.minimum(seg_start + cnt, n_rows - EXPERT_ROWS), n_used]).astype(I32)

    x_rows = _dispatch(tail, dest_flat, h_tiles, n_rows, tc)
    y_rows = _experts(blk_e, n_used, x_rows, w_gate, w_up, w_down, layer)
    return _combine(dest_flat, y_rows, x, meta, gm, final_g, seq, tc)


def kernel(x, c, positions, w_ada, b_ada, norm_g, w_in, w_out, diff_lambda, diff_subln_g, w_router, b_router,
           w_gate, w_up, w_down, final_norm_g):
    batch, seq, d = x.shape
    depth = w_ada.shape[0]
    n = batch * seq
    mod = _modulation(c, w_ada, b_ada)
    xf = x.reshape(n, d)
    for i in range(depth):
        sh_a, sc_a, g_a, sh_m, sc_m, g_m = [m.reshape(batch, 1, d) for m in jnp.split(mod[i], 6, axis=-1)]
        diff = i % 2 == 0
        proj = _norm_proj(xf, norm_g[i, 0], sc_a, sh_a, w_in[i].astype(BF16), positions, diff, seq)
        if diff:
            o = _diff_attention(proj, diff_lambda[i // 2], diff_subln_g[i // 2], batch, seq, d, i)
        else:
            o = _sb_attention(proj, batch, seq, d)
        xf, h, meta, meta_t, counts = _outproj_route(o, xf, w_out[i].astype(BF16), g_a, norm_g[i, 1], sc_m, sh_m,
                                                     w_router, b_router, seq)
        final_g = final_norm_g if i == depth - 1 else None
        xf = _moe(h, meta, meta_t, counts, xf, g_m, w_gate, w_up, w_down, i, final_g, seq)
    return xf.reshape(batch, seq, d)
```

```python
import functools
import math

import jax
import jax.numpy as jnp
from jax import lax
from jax.experimental import pallas as pl
from jax.experimental.pallas import tpu as pltpu

F32 = jnp.float32
BF16 = jnp.bfloat16
I32 = jnp.int32

LANES = 128
HEAD_DIM = 64
ROPE_DIMS = HEAD_DIM // 4
ROPE_THETA = 500000.0
N_EXPERTS = 32
N_GROUPS = 4
GROUP_SIZE = N_EXPERTS // N_GROUPS
TOP_K = 2
NORM_EPS = 1e-6
EXPERT_ROWS = 512
EXPERT_SUB_ROWS = 256
HEADS_PER_STEP = 2
SB_GROUPS_PER_STEP = 4
DIFF_BLOCKS_PER_TRIP = 2
INDEX_STEPS = 8
ROW_TILE = 8
SB_DEAD_LOG2 = -110.0 * math.log2(math.e)
MASK_VALUE = -1e30
VMEM_LIMIT = 56 * 1024 * 1024


def _cparams(semantics, vmem=VMEM_LIMIT):
    return pltpu.CompilerParams(dimension_semantics=semantics, vmem_limit_bytes=vmem)


def _split_bf16(a):
    hi = a.astype(BF16)
    lo = (a - hi.astype(F32)).astype(BF16)
    return hi, lo


def _dot3(a, b):
    ah, al = _split_bf16(a)
    bh, bl = _split_bf16(b)
    d = functools.partial(jnp.dot, preferred_element_type=F32)
    return d(ah, bh) + d(ah, bl) + d(al, bh)


def _rms(x, g):
    ms = jnp.mean(x * x, axis=-1, keepdims=True)
    return x * lax.rsqrt(ms + NORM_EPS) * g


def _mod_kernel(c_ref, w_ref, b_ref, o_ref):
    c = c_ref[...]
    c_act = c / (1.0 + jnp.exp(-c))
    o_ref[...] = _dot3(c_act, w_ref[...]) + b_ref[...]


def _modulation(c, w_ada, b_ada):
    depth, d, six_d = w_ada.shape
    b = c.shape[0]
    rows = 16
    c_pad = jnp.zeros((rows, d), F32).at[:b].set(c)
    tn = 1536
    out = pl.pallas_call(
        _mod_kernel,
        out_shape=jax.ShapeDtypeStruct((depth, rows, six_d), F32),
        grid=(depth, six_d // tn),
        in_specs=[
            pl.BlockSpec((rows, d), lambda l, j: (0, 0)),
            pl.BlockSpec((None, d, tn), lambda l, j: (l, 0, j)),
            pl.BlockSpec((None, 1, tn), lambda l, j: (l, 0, j)),
        ],
        out_specs=pl.BlockSpec((None, rows, tn), lambda l, j: (l, 0, j)),
        compiler_params=_cparams(("arbitrary", "arbitrary")),
        name="adaln_mod",
    )(c_pad, w_ada, b_ada.reshape(depth, 1, six_d))
    return out[:, :b]


Q_SCALE = HEAD_DIM ** -0.5 * math.log2(math.e)


def _proj_kernel(*refs, d, rope):
    if rope:
        x_ref, g_ref, sc_ref, sh_ref, w_ref, pos_ref, pat_ref, o_ref = refs
    else:
        x_ref, g_ref, sc_ref, sh_ref, w_ref, o_ref = refs
    h = _rms(x_ref[...], g_ref[...]) * (1.0 + sc_ref[...]) + sh_ref[...]
    p = jnp.dot(h.astype(BF16), w_ref[...], preferred_element_type=F32)
    if rope:
        ang = pos_ref[...].astype(F32) * pat_ref[...]
        cs, sn = jnp.cos(ang), jnp.sin(ang)
        lane = lax.broadcasted_iota(I32, ang.shape, 1) & (HEAD_DIM - 1)
        half = ROPE_DIMS // 2
        s_lo = jnp.where(lane < half, -sn, 0.0)
        s_hi = jnp.where((lane >= half) & (lane < ROPE_DIMS), sn, 0.0)
    for j in range(2 * d // LANES):
        blk = p[:, j * LANES:(j + 1) * LANES]
        if rope:
            blk = (blk * cs + pltpu.roll(blk, half, 1) * s_hi
                   + pltpu.roll(blk, LANES - half, 1) * s_lo)
        if j < d // LANES:
            blk = blk * Q_SCALE
        o_ref[:, j * LANES:(j + 1) * LANES] = blk.astype(BF16)
    o_ref[:, 2 * d:] = p[:, 2 * d:].astype(BF16)


def _norm_proj(x, g, sc, sh, w_bf16, positions, rope, seq):
    n, d = x.shape
    tm = min(512, seq)
    per_b = seq // tm
    row = lambda i: (i, 0)
    bat = lambda i: (i // per_b, 0, 0)
    in_specs = [
        pl.BlockSpec((tm, d), row),
        pl.BlockSpec((1, d), lambda i: (0, 0)),
        pl.BlockSpec((None, 1, d), bat),
        pl.BlockSpec((None, 1, d), bat),
        pl.BlockSpec((d, 3 * d), lambda i: (0, 0)),
    ]
    args = [x, g.reshape(1, d), sc, sh, w_bf16]
    if rope:
        half = ROPE_DIMS // 2
        inv_freq = jnp.power(jnp.float32(ROPE_THETA), -jnp.arange(half, dtype=F32) * (2.0 / ROPE_DIMS))
        lane = jnp.arange(LANES) % HEAD_DIM
        pat = jnp.where(lane < ROPE_DIMS, inv_freq[lane % half], 0.0).astype(F32).reshape(1, LANES)
        in_specs += [pl.BlockSpec((tm, 1), row), pl.BlockSpec((1, LANES), lambda i: (0, 0))]
        args += [positions.reshape(n, 1), pat]
    return pl.pallas_call(
        functools.partial(_proj_kernel, d=d, rope=rope),
        out_shape=jax.ShapeDtypeStruct((n, 3 * d), BF16),
        grid=(n // tm,),
        in_specs=in_specs,
        out_specs=pl.BlockSpec((tm, 3 * d), row),
        compiler_params=_cparams(("arbitrary",)),
        name="norm_proj_rope" if rope else "norm_proj",
    )(*args)


def _stack_halves(q):
    lane = lax.broadcasted_iota(I32, q.shape, 1)
    zero = jnp.zeros_like(q)
    return jnp.concatenate([jnp.where(lane < HEAD_DIM, q, zero), jnp.where(lane >= HEAD_DIM, q, zero)], axis=0)


def _qk(qs, kb):
    return lax.dot_general(qs, kb, (((1,), (1,)), ((), ())), preferred_element_type=F32)


def _head_lanes(hh):
    return slice(hh * LANES, (hh + 1) * LANES)


def _lane_chunks(x):
    return [x[:, c * LANES:(c + 1) * LANES] for c in range(x.shape[1] // LANES)]


def _block_positions(shape, t, i, j):
    r = lax.broadcasted_iota(I32, shape, 0)
    qpos = i * t + jnp.where(r >= t, r - t, r)
    kpos = j * t + lax.broadcasted_iota(I32, shape, 1)
    return qpos, kpos


def _diff_attn_kernel(lam_ref, g_ref, q_ref, k_ref, v_ref, o_ref, m_ref, l_ref, acc_ref, *, t, lam_init):
    i = pl.program_id(2)
    qs = [_stack_halves(q_ref[:, _head_lanes(hh)]) for hh in range(HEADS_PER_STEP)]
    m_ref[...] = jnp.full(m_ref.shape, MASK_VALUE, F32)
    l_ref[...] = jnp.zeros(l_ref.shape, F32)
    acc_ref[...] = jnp.zeros(acc_ref.shape, F32)

    def scores(hh, j):
        start = pl.multiple_of(j * t, t)
        return _qk(qs[hh], k_ref[pl.ds(start, t), _head_lanes(hh)])

    def softmax_pv(hh, s, j, masked):
        start = pl.multiple_of(j * t, t)
        if masked:
            qpos, kpos = _block_positions(s.shape, t, i, j)
            s = jnp.where(kpos <= qpos, s, MASK_VALUE)
        chunks = _lane_chunks(s)
        m_old = m_ref[hh]
        m_new = jnp.maximum(m_old, jnp.max(functools.reduce(jnp.maximum, chunks), axis=1, keepdims=True))
        alpha = jnp.exp2(m_old - m_new)
        ps = [jnp.exp2(ch - m_new) for ch in chunks]
        l_ref[hh] = l_ref[hh] * alpha + functools.reduce(jnp.add, ps)
        p = jnp.concatenate([x.astype(BF16) for x in ps], axis=1)
        pv = jnp.dot(p, v_ref[pl.ds(start, t), _head_lanes(hh)], preferred_element_type=F32)
        acc_ref[hh] = acc_ref[hh] * alpha + pv
        m_ref[hh] = m_new

    def block_run(js, last_masked):
        work = [(hh, j, last_masked and n == len(js) - 1) for n, j in enumerate(js) for hh in range(HEADS_PER_STEP)]
        s_next = scores(work[0][0], work[0][1])
        for n, (hh, j, masked) in enumerate(work):
            s_cur = s_next
            if n + 1 < len(work):
                s_next = scores(work[n + 1][0], work[n + 1][1])
            softmax_pv(hh, s_cur, j, masked)

    run = DIFF_BLOCKS_PER_TRIP
    shift = run.bit_length() - 1

    def off_diag_run(p, carry):
        block_run([run * p + n for n in range(run)], False)
        return carry

    lax.fori_loop(0, lax.shift_right_logical(i, shift), off_diag_run, 0)

    rem = i & (run - 1)
    for r in range(run):
        @pl.when(rem == r)
        def _(r=r):
            block_run([i - r + n for n in range(r + 1)], True)

    lam = lam_ref[...]
    lam_full = (jnp.exp(jnp.sum(lam[0:1] * lam[1:2], axis=1, keepdims=True))
                - jnp.exp(jnp.sum(lam[2:3] * lam[3:4], axis=1, keepdims=True)) + lam_init)
    for hh in range(HEADS_PER_STEP):
        o_all = acc_ref[hh] / jnp.sum(l_ref[hh], axis=1, keepdims=True)
        o = o_all[:t] - lam_full * o_all[t:]
        o_ref[:, _head_lanes(hh)] = (_rms(o, g_ref[...]) * (1.0 - lam_init)).astype(BF16)


def _attn_specs(t, seq, nq, groups, per_step):
    w = per_step * LANES
    q_spec = pl.BlockSpec((t, w), lambda b, h, i: (b * nq + i, h))
    k_spec = pl.BlockSpec((seq, w), lambda b, h, i: (b, groups + h))
    v_spec = pl.BlockSpec((seq, w), lambda b, h, i: (b, 2 * groups + h))
    return q_spec, k_spec, v_spec


def _diff_attention(proj, lam, subln_g, batch, seq, d, layer_idx):
    n = batch * seq
    groups = d // (HEADS_PER_STEP * LANES)
    t = min(512, seq)
    nq = seq // t
    lam_init = 0.8 - 0.6 * math.exp(-0.3 * layer_idx)
    q_spec, k_spec, v_spec = _attn_specs(t, seq, nq, groups, HEADS_PER_STEP)
    stat = pltpu.VMEM((HEADS_PER_STEP, 2 * t, LANES), F32)
    return pl.pallas_call(
        functools.partial(_diff_attn_kernel, t=t, lam_init=lam_init),
        out_shape=jax.ShapeDtypeStruct((n, d), BF16),
        grid=(batch, groups, nq),
        in_specs=[
            pl.BlockSpec((4, HEAD_DIM), lambda b, h, i: (0, 0)),
            pl.BlockSpec((1, LANES), lambda b, h, i: (0, 0)),
            q_spec, k_spec, v_spec,
        ],
        out_specs=q_spec,
        scratch_shapes=[stat, stat, stat],
        compiler_params=_cparams(("arbitrary", "arbitrary", "arbitrary")),
        name="diff_attention",
    )(lam, subln_g.reshape(1, LANES), proj, proj, proj)


def _sb_attn_kernel(q_ref, k_ref, v_ref, tri_ref, o_ref, c_ref, acc_ref, *, t):
    i = pl.program_id(2)
    units = list(range(SB_GROUPS_PER_STEP))
    qs = [_stack_halves(q_ref[:, _head_lanes(u)]) for u in units]

    def gates(u, j, diag, guard):
        start = pl.multiple_of(j * t, t)
        z = _qk(qs[u], k_ref[pl.ds(start, t), _head_lanes(u)])
        log_beta = jnp.minimum(z, 0.0) - jnp.log2(1.0 + jnp.exp2(-jnp.abs(z)))
        log_rest = log_beta - z
        keep = None
        if diag:
            qpos, kpos = _block_positions(z.shape, t, i, j)
            keep = kpos < qpos
        if guard is not None:
            keep = lax.broadcasted_iota(I32, z.shape, 0) < guard
        if keep is not None:
            log_rest = jnp.where(keep, log_rest, 0.0)
        row_sum = jnp.sum(functools.reduce(jnp.add, _lane_chunks(log_rest)), axis=1, keepdims=True)
        return log_beta, log_rest, keep, row_sum

    def later_keys_sum(log_rest):
        return jnp.dot(log_rest.astype(BF16), tri_ref[...], preferred_element_type=F32)

    def weighted_values(u, j, log_beta, within, keep, c):
        start = pl.multiple_of(j * t, t)
        arg = log_beta + within
        if c is not None:
            arg = jnp.concatenate([ch + c for ch in _lane_chunks(arg)], axis=1)
        a = jnp.exp2(arg)
        if keep is not None:
            a = jnp.where(keep, a, 0.0)
        return jnp.dot(a.astype(BF16), v_ref[pl.ds(start, t), _head_lanes(u)], preferred_element_type=F32)

    def sweep(work, carried):
        gated, within, pv = {}, {}, {}
        for step in range(len(work) + 2):
            if step < len(work):
                gated[step] = gates(*work[step])
            if 1 <= step <= len(work):
                within[step - 1] = later_keys_sum(gated[step - 1][1])
            if step >= 2:
                n = step - 2
                u, j, _, _ = work[n]
                pv[n] = weighted_values(u, j, gated[n][0], within[n], gated[n][2], carried(n, gated))
        return gated, pv

    has_prev = jnp.where(i > 0, 2 * t, 0)
    j_prev = jnp.maximum(i - 1, 0)
    n_u = len(units)
    work = [(u, i, True, None) for u in units] + [(u, j_prev, False, has_prev) for u in units]
    gated, pv = sweep(work, lambda n, g: None if n < n_u else jnp.broadcast_to(g[n - n_u][3], (2 * t, LANES)))
    c_max = None
    for u in units:
        acc_ref[u] = pv[u] + pv[n_u + u]
        c2 = jnp.broadcast_to(gated[u][3], (2 * t, LANES)) + gated[n_u + u][3]
        c_ref[u] = c2
        c_max = jnp.max(c2) if c_max is None else jnp.maximum(c_max, jnp.max(c2))

    def cond(state):
        j, c_max = state
        return (j >= 0) & (c_max > SB_DEAD_LOG2)

    def body(state):
        j, _ = state
        cs = [c_ref[u] for u in units]
        gated, pv = sweep([(u, j, False, None) for u in units], lambda n, g: cs[n])
        c_max = None
        for u in units:
            acc_ref[u] += pv[u]
            c_new = cs[u] + gated[u][3]
            c_ref[u] = c_new
            c_max = jnp.max(c_new) if c_max is None else jnp.maximum(c_max, jnp.max(c_new))
        return j - 1, c_max

    lax.while_loop(cond, body, (i - 2, c_max))

    lane = lax.broadcasted_iota(I32, (t, LANES), 1)
    for hh in units:
        acc = acc_ref[hh]
        o_ref[:, _head_lanes(hh)] = jnp.where(lane < HEAD_DIM, acc[:t], acc[t:]).astype(BF16)


def _sb_attention(proj, batch, seq, d):
    n = batch * seq
    groups = d // (SB_GROUPS_PER_STEP * LANES)
    t = min(256, seq)
    nq = seq // t
    idx = jnp.arange(t)
    tri = (idx[:, None] > idx[None, :]).astype(BF16)
    q_spec, k_spec, v_spec = _attn_specs(t, seq, nq, groups, SB_GROUPS_PER_STEP)
    stat = pltpu.VMEM((SB_GROUPS_PER_STEP, 2 * t, LANES), F32)
    return pl.pallas_call(
        functools.partial(_sb_attn_kernel, t=t),
        out_shape=jax.ShapeDtypeStruct((n, d), BF16),
        grid=(batch, groups, nq),
        in_specs=[q_spec, k_spec, v_spec, pl.BlockSpec((t, t), lambda b, h, i: (0, 0))],
        out_specs=q_spec,
        scratch_shapes=[stat, stat],
        compiler_params=_cparams(("arbitrary", "arbitrary", "arbitrary")),
        name="sb_attention",
    )(proj, proj, proj, tri)


def _first_index_of(mask, lane, sentinel):
    return jnp.min(jnp.where(mask, lane, float(sentinel)), axis=1, keepdims=True)


def _top2(v, lane):
    neg = -jnp.inf
    m1 = jnp.max(v, axis=1, keepdims=True)
    i1 = _first_index_of(v == m1, lane, N_EXPERTS)
    v2 = jnp.where(lane == i1, neg, v)
    m2 = jnp.max(v2, axis=1, keepdims=True)
    i2 = _first_index_of(v2 == m2, lane, N_EXPERTS)
    return m1, i1, m2, i2


def _route(scores, sel):
    lane_i = lax.broadcasted_iota(I32, sel.shape, 1)
    lane = lane_i.astype(F32)
    grp = lax.shift_right_logical(lane_i, int(math.log2(GROUP_SIZE))).astype(F32)
    neg = -jnp.inf
    best = None
    g_best = None
    for g in range(N_GROUPS):
        m1, _, m2, _ = _top2(jnp.where(grp == float(g), sel, neg), lane)
        score = m1 + m2
        if g == 0:
            best, g_best = score, jnp.zeros(score.shape, F32)
        else:
            better = score > best
            g_best = jnp.where(better, float(g), g_best)
            best = jnp.where(better, score, best)
    _, e1, _, e2 = _top2(jnp.where(grp == g_best, sel, neg), lane)
    hot1, hot2 = lane == e1, lane == e2
    s1 = jnp.sum(jnp.where(hot1, scores, 0.0), axis=1, keepdims=True)
    s2 = jnp.sum(jnp.where(hot2, scores, 0.0), axis=1, keepdims=True)
    tot = s1 + s2
    return s1 / tot, s2 / tot, e1, e2, hot1, hot2


def _outproj_route_kernel(o_ref, x_ref, wo_ref, ga_ref, g_ref, sc_ref, sh_ref, wr_ref, br_ref, tril_ref,
                          xo_ref, h_ref, meta_ref, meta_t_ref, cnt_ref, *, ts, n_sub):
    @pl.when(pl.program_id(0) == 0)
    def _():
        cnt_ref[...] = jnp.zeros(cnt_ref.shape, F32)

    subs = range(n_sub)
    rows = [pl.ds(s * ts, ts) for s in subs]
    mix = [jnp.dot(o_ref[rows[s], :], wo_ref[...], preferred_element_type=F32) for s in subs]
    hs = []
    for s in subs:
        xn = x_ref[rows[s], :] + ga_ref[...] * mix[s]
        xo_ref[rows[s], :] = xn
        h = _rms(xn, g_ref[...]) * (1.0 + sc_ref[...]) + sh_ref[...]
        _store_rows(h_ref, s * ts, h)
        hs.append(h)

    logits = [_dot3(hs[s], wr_ref[...]) for s in subs]
    routed = []
    for s in subs:
        e = jnp.exp(logits[s] - jnp.max(logits[s], axis=1, keepdims=True))
        scores = e / jnp.sum(e, axis=1, keepdims=True)
        routed.append(_route(scores, scores + br_ref[...]))

    tril = tril_ref[...]
    ones = [(jnp.where(r[4], 1.0, 0.0), jnp.where(r[5], 1.0, 0.0)) for r in routed]
    prefix = [(jnp.dot(tril, o1.astype(BF16), preferred_element_type=F32),
               jnp.dot(tril, o2.astype(BF16), preferred_element_type=F32)) for o1, o2 in ones]
    base = cnt_ref[...]
    ml = lax.broadcasted_iota(I32, (ts, LANES), 1)
    for s in subs:
        g1, g2, e1, e2, hot1, hot2 = routed[s]
        tot1 = jnp.sum(ones[s][0], axis=0, keepdims=True)
        tot2 = jnp.sum(ones[s][1], axis=0, keepdims=True)
        r1 = jnp.sum(jnp.where(hot1, base + prefix[s][0], 0.0), axis=1, keepdims=True)
        r2 = jnp.sum(jnp.where(hot2, base + tot1 + prefix[s][1], 0.0), axis=1, keepdims=True)
        base = base + tot1 + tot2
        meta = jnp.where(ml == 0, g1, 0.0)
        for k, col in enumerate((g2, e1, e2, r1, r2)):
            meta = jnp.where(ml == k + 1, col, meta)
        meta_ref[rows[s], :] = meta
        meta_t_ref[:, rows[s]] = meta.T[:meta_t_ref.shape[0]]
    cnt_ref[...] = base


def _outproj_route(o, x, wo_bf16, ga, g, sc, sh, w_router, b_router, seq):
    n, d = x.shape
    tm = min(512, seq)
    ts = min(128, tm)
    per_b = seq // tm
    row = lambda i: (i, 0)
    bat = lambda i: (i // per_b, 0, 0)
    const = lambda i: (0, 0)
    idx = jnp.arange(ts)
    tril = (idx[None, :] < idx[:, None]).astype(BF16)
    return pl.pallas_call(
        functools.partial(_outproj_route_kernel, ts=ts, n_sub=tm // ts),
        out_shape=(
            jax.ShapeDtypeStruct((n, d), F32),
            jax.ShapeDtypeStruct((n * ROW_TILE, LANES), F32),
            jax.ShapeDtypeStruct((n, LANES), F32),
            jax.ShapeDtypeStruct((8, n), F32),
            jax.ShapeDtypeStruct((1, N_EXPERTS), F32),
        ),
        grid=(n // tm,),
        in_specs=[
            pl.BlockSpec((tm, d), row),
            pl.BlockSpec((tm, d), row),
            pl.BlockSpec((d, d), const),
            pl.BlockSpec((None, 1, d), bat),
            pl.BlockSpec((1, d), const),
            pl.BlockSpec((None, 1, d), bat),
            pl.BlockSpec((None, 1, d), bat),
            pl.BlockSpec((d, N_EXPERTS), const),
            pl.BlockSpec((1, N_EXPERTS), const),
            pl.BlockSpec((ts, ts), const),
        ],
        out_specs=(
            pl.BlockSpec((tm, d), row),
            pl.BlockSpec((tm * ROW_TILE, LANES), row),
            pl.BlockSpec((tm, LANES), row),
            pl.BlockSpec((8, tm), lambda i: (0, i)),
            pl.BlockSpec((1, N_EXPERTS), const),
        ),
        compiler_params=_cparams(("arbitrary",)),
        name="outproj_route",
    )(o, x, wo_bf16, ga, g.reshape(1, d), sc, sh, w_router, b_router.reshape(1, N_EXPERTS), tril)


def _row_copy(src, src_row, dst, dst_row, sem, rows=1):
    n = rows * ROW_TILE
    return pltpu.make_async_copy(src.at[pl.ds(pl.multiple_of(src_row * ROW_TILE, ROW_TILE), n)],
                                 dst.at[pl.ds(pl.multiple_of(dst_row * ROW_TILE, ROW_TILE), n)], sem)


def _load_rows(ref, start_row, rows):
    return jnp.concatenate([ref[pl.ds(start_row * ROW_TILE + k, rows, stride=ROW_TILE), :]
                            for k in range(ROW_TILE)], axis=1)


def _store_rows(ref, start_row, value):
    for k in range(ROW_TILE):
        ref[pl.ds(start_row * ROW_TILE + k, value.shape[0], stride=ROW_TILE), :] = value[:, k * LANES:(k + 1) * LANES]


def _index_scratch(n, tc):
    steps = math.gcd(INDEX_STEPS, n // tc)
    return pltpu.SMEM((steps * 2 * tc,), I32)


def _load_row_indices(dest_hbm, dest_smem, sem, step, tc):
    n_steps = dest_smem.shape[0] // (2 * tc)
    slot = lax.rem(step, n_steps)

    @pl.when(slot == 0)
    def _():
        cp = pltpu.make_async_copy(dest_hbm.at[pl.ds(step * 2 * tc, dest_smem.shape[0])], dest_smem, sem)
        cp.start()
        cp.wait()

    return slot * 2 * tc


def _dispatch_kernel(tail_ref, dest_hbm, h_ref, rows_hbm, dest_smem, stage, zero_buf, idx_sem, row_sems, *, tc, n_blk):
    c = pl.program_id(0)
    last = pl.num_programs(0) - 1

    @pl.when(c == 0)
    def _():
        zero_buf[...] = jnp.zeros(zero_buf.shape, F32)

        def fill(row):
            return _row_copy(zero_buf, 0, rows_hbm, row, row_sems.at[0], rows=EXPERT_ROWS)

        def fill_tail(e, carry):
            cp = fill(tail_ref[e])
            cp.start()
            cp.wait()
            return carry

        def start_unused_block(b, carry):
            fill(b * EXPERT_ROWS).start()
            return carry

        def wait_unused_block(b, carry):
            fill(b * EXPERT_ROWS).wait()
            return carry

        lax.fori_loop(0, N_EXPERTS, fill_tail, 0)
        lax.fori_loop(tail_ref[N_EXPERTS], n_blk, start_unused_block, 0)
        lax.fori_loop(tail_ref[N_EXPERTS], n_blk, wait_unused_block, 0)

    base = _load_row_indices(dest_hbm, dest_smem, idx_sem, c, tc)

    def drain(slot):
        def body(tk, carry):
            _row_copy(stage.at[slot], 0, rows_hbm, 0, row_sems.at[slot]).wait()
            _row_copy(stage.at[slot], 0, rows_hbm, 0, row_sems.at[slot]).wait()
            return carry

        lax.fori_loop(0, tc, body, 0, unroll=8)

    for slot in range(2):
        @pl.when((c & 1) == slot)
        def _(slot=slot):
            stage[slot] = h_ref[...]

            def issue(tk, carry):
                _row_copy(stage.at[slot], tk, rows_hbm, dest_smem[base + tk], row_sems.at[slot]).start(priority=0)
                _row_copy(stage.at[slot], tk, rows_hbm, dest_smem[base + tc + tk], row_sems.at[slot]).start(priority=1)
                return carry

            lax.fori_loop(0, tc, issue, 0, unroll=8)

            @pl.when(c > 0)
            def _():
                drain(1 - slot)

            @pl.when(c == last)
            def _():
                drain(slot)


def _dispatch(tail, dest_flat, h_tiles, n_rows, tc):
    n = h_tiles.shape[0] // ROW_TILE
    any_spec = pl.BlockSpec(memory_space=pl.ANY)
    grid_spec = pltpu.PrefetchScalarGridSpec(
        num_scalar_prefetch=1,
        grid=(n // tc,),
        in_specs=[any_spec, pl.BlockSpec((tc * ROW_TILE, LANES), lambda i, tail: (i, 0))],
        out_specs=any_spec,
        scratch_shapes=[_index_scratch(n, tc), pltpu.VMEM((2, tc * ROW_TILE, LANES), F32),
                        pltpu.VMEM((EXPERT_ROWS * ROW_TILE, LANES), F32),
                        pltpu.SemaphoreType.DMA, pltpu.SemaphoreType.DMA((2,))],
    )
    return pl.pallas_call(
        functools.partial(_dispatch_kernel, tc=tc, n_blk=n_rows // EXPERT_ROWS),
        out_shape=jax.ShapeDtypeStruct((n_rows * ROW_TILE, LANES), F32),
        grid_spec=grid_spec,
        compiler_params=_cparams(("arbitrary",)),
        name="moe_dispatch",
    )(tail, dest_flat, h_tiles)


def _experts_kernel(blk_e_ref, n_used_ref, x_ref, wg_ref, wu_ref, wd_ref, y_ref, wg_s, wu_s, wd_s):
    i = pl.program_id(0)
    used = i < n_used_ref[0]
    prev = blk_e_ref[jnp.maximum(i - 1, 0)]
    new_expert = (i == 0) | (blk_e_ref[i] != prev)

    @pl.when(used & new_expert)
    def _():
        wg_s[...] = wg_ref[...].astype(BF16)
        wu_s[...] = wu_ref[...].astype(BF16)
        wd_s[...] = wd_ref[...].astype(BF16)

    @pl.when(used)
    def _():
        groups = range(EXPERT_ROWS // EXPERT_SUB_ROWS)
        dot = functools.partial(jnp.dot, preferred_element_type=F32)
        xb = [_load_rows(x_ref, s * EXPERT_SUB_ROWS, EXPERT_SUB_ROWS).astype(BF16) for s in groups]
        gate = [dot(xb[s], wg_s[...]) for s in groups]
        up = [dot(xb[s], wu_s[...]) for s in groups]
        hid = [(gate[s] / (1.0 + jnp.exp(-gate[s])) * up[s]).astype(BF16) for s in groups]
        for s in groups:
            _store_rows(y_ref, s * EXPERT_SUB_ROWS, dot(hid[s], wd_s[...]))

    @pl.when(jnp.logical_not(used))
    def _():
        y_ref[...] = jnp.zeros(y_ref.shape, F32)


def _experts(blk_e, n_used, x_tiles, w_gate, w_up, w_down, layer):
    d, f = w_gate.shape[-2:]
    n_blk = x_tiles.shape[0] // (EXPERT_ROWS * ROW_TILE)
    blk = (EXPERT_ROWS * ROW_TILE, LANES)
    last_used = lambda n_used_ref: jnp.maximum(n_used_ref[0] - 1, 0)
    x_map = lambda i, be, nu: (jnp.minimum(i, last_used(nu)), 0)
    w_map = lambda i, be, nu: (layer, be[jnp.minimum(i, last_used(nu))], 0, 0)
    grid_spec = pltpu.PrefetchScalarGridSpec(
        num_scalar_prefetch=2,
        grid=(n_blk,),
        in_specs=[
            pl.BlockSpec(blk, x_map),
            pl.BlockSpec((None, None, d, f), w_map),
            pl.BlockSpec((None, None, d, f), w_map),
            pl.BlockSpec((None, None, f, d), w_map),
        ],
        out_specs=pl.BlockSpec(blk, lambda i, be, nu: (i, 0)),
        scratch_shapes=[pltpu.VMEM((d, f), BF16), pltpu.VMEM((d, f), BF16), pltpu.VMEM((f, d), BF16)],
    )
    return pl.pallas_call(
        _experts_kernel,
        out_shape=jax.ShapeDtypeStruct(x_tiles.shape, F32),
        grid_spec=grid_spec,
        compiler_params=_cparams(("arbitrary",)),
        name="moe_experts",
    )(blk_e, n_used, x_tiles, w_gate, w_up, w_down)


def _combine_kernel(*refs, tc, final):
    if final:
        dest_hbm, y_hbm, x_ref, meta_ref, gm_ref, fg_ref, o_ref = refs[:7]
    else:
        dest_hbm, y_hbm, x_ref, meta_ref, gm_ref, o_ref = refs[:6]
    dest_smem, a0, a1, b0, b1, idx_sem, row_sems = refs[-7:]
    bufs = ((a0, a1), (b0, b1))
    c = pl.program_id(0)
    last = pl.num_programs(0) - 1

    def fetch(step, slot):
        base = _load_row_indices(dest_hbm, dest_smem, idx_sem, step, tc)

        def issue(tk, carry):
            _row_copy(y_hbm, dest_smem[base + tk], bufs[slot][0], tk, row_sems.at[slot]).start(priority=0)
            _row_copy(y_hbm, dest_smem[base + tc + tk], bufs[slot][1], tk, row_sems.at[slot]).start(priority=1)
            return carry

        lax.fori_loop(0, tc, issue, 0, unroll=8)

    def drain(slot):
        def body(tk, carry):
            _row_copy(y_hbm, 0, bufs[slot][0], 0, row_sems.at[slot]).wait()
            _row_copy(y_hbm, 0, bufs[slot][1], 0, row_sems.at[slot]).wait()
            return carry

        lax.fori_loop(0, tc, body, 0, unroll=8)

    @pl.when(c == 0)
    def _():
        fetch(0, 0)

    for slot in range(2):
        @pl.when((c & 1) == slot)
        def _(slot=slot):
            @pl.when(c < last)
            def _():
                fetch(c + 1, 1 - slot)

            drain(slot)
            meta = meta_ref[...]
            y = meta[:, 0:1] * _load_rows(bufs[slot][0], 0, tc) + meta[:, 1:2] * _load_rows(bufs[slot][1], 0, tc)
            xn = x_ref[...] + gm_ref[...] * y
            if final:
                xn = _rms(xn, fg_ref[...])
            o_ref[...] = xn


def _combine(dest_flat, y_rows, x, meta, gm, final_g, seq, tc):
    n, d = x.shape
    per_b = seq // tc
    row = lambda i: (i, 0)
    any_spec = pl.BlockSpec(memory_space=pl.ANY)
    final = final_g is not None
    in_specs = [any_spec, any_spec, pl.BlockSpec((tc, d), row), pl.BlockSpec((tc, LANES), row),
                pl.BlockSpec((None, 1, d), lambda i: (i // per_b, 0, 0))]
    args = [dest_flat, y_rows, x, meta, gm]
    if final:
        in_specs.append(pl.BlockSpec((1, d), lambda i: (0, 0)))
        args.append(final_g.reshape(1, d))
    return pl.pallas_call(
        functools.partial(_combine_kernel, tc=tc, final=final),
        out_shape=jax.ShapeDtypeStruct((n, d), F32),
        grid=(n // tc,),
        in_specs=in_specs,
        out_specs=pl.BlockSpec((tc, d), row),
        scratch_shapes=[_index_scratch(n, tc)]
        + [pltpu.VMEM((tc * ROW_TILE, LANES), F32)] * 4
        + [pltpu.SemaphoreType.DMA, pltpu.SemaphoreType.DMA((2,))],
        compiler_params=_cparams(("arbitrary",)),
        name="moe_combine_final" if final else "moe_combine",
    )(*args)


def _moe(h_tiles, meta, meta_t, counts, x, gm, w_gate, w_up, w_down, layer, final_g, seq):
    n, d = x.shape
    assert d == ROW_TILE * LANES
    tc = min(512, seq)
    n_rows = n * TOP_K + N_EXPERTS * EXPERT_ROWS
    n_blk = n_rows // EXPERT_ROWS
    cnt = counts[0].astype(I32)
    padded = ((cnt + EXPERT_ROWS - 1) // EXPERT_ROWS) * EXPERT_ROWS
    seg_end = jnp.cumsum(padded)
    seg_start = seg_end - padded
    n_used = (seg_end[-1:] // EXPERT_ROWS).astype(I32)
    blk_start = jnp.arange(n_blk, dtype=I32) * EXPERT_ROWS
    blk_e = jnp.minimum(jnp.sum(seg_end[None, :] <= blk_start[:, None], axis=1), N_EXPERTS - 1).astype(I32)
    expert = meta_t[2:4].astype(I32)
    rank = meta_t[4:6].astype(I32)
    hot = expert[:, :, None] == jnp.arange(N_EXPERTS, dtype=I32)
    dest = rank + jnp.sum(jnp.where(hot, seg_start, 0), axis=-1)
    dest_flat = dest.reshape(TOP_K, n // tc, tc).transpose(1, 0, 2).reshape(-1)

    tail = jnp.concatenate([jnp.minimum(seg_start + cnt, n_rows - EXPERT_ROWS), n_used]).astype(I32)

    x_rows = _dispatch(tail, dest_flat, h_tiles, n_rows, tc)
    y_rows = _experts(blk_e, n_used, x_rows, w_gate, w_up, w_down, layer)
    return _combine(dest_flat, y_rows, x, meta, gm, final_g, seq, tc)


def kernel(x, c, positions, w_ada, b_ada, norm_g, w_in, w_out, diff_lambda, diff_subln_g, w_router, b_router,
           w_gate, w_up, w_down, final_norm_g):
    batch, seq, d = x.shape
    depth = w_ada.shape[0]
    n = batch * seq
    mod = _modulation(c, w_ada, b_ada)
    xf = x.reshape(n, d)
    for i in range(depth):
        sh_a, sc_a, g_a, sh_m, sc_m, g_m = [m.reshape(batch, 1, d) for m in jnp.split(mod[i], 6, axis=-1)]
        diff = i % 2 == 0
        proj = _norm_proj(xf, norm_g[i, 0], sc_a, sh_a, w_in[i].astype(BF16), positions, diff, seq)
        if diff:
            o = _diff_attention(proj, diff_lambda[i // 2], diff_subln_g[i // 2], batch, seq, d, i)
        else:
            o = _sb_attention(proj, batch, seq, d)
        xf, h, meta, meta_t, counts = _outproj_route(o, xf, w_out[i].astype(BF16), g_a, norm_g[i, 1], sc_m, sh_m,
                                                     w_router, b_router, seq)
        final_g = final_norm_g if i == depth - 1 else None
        xf = _moe(h, meta, meta_t, counts, xf, g_m, w_gate, w_up, w_down, i, final_g, seq)
    return xf.reshape(batch, seq, d)
```

```python
import functools
import math

import jax
import jax.numpy as jnp
from jax import lax
from jax.experimental import pallas as pl
from jax.experimental.pallas import tpu as pltpu

F32 = jnp.float32
BF16 = jnp.bfloat16
I32 = jnp.int32

LANES = 128
HEAD_DIM = 64
ROPE_DIMS = HEAD_DIM // 4
ROPE_THETA = 500000.0
N_EXPERTS = 32
N_GROUPS = 4
GROUP_SIZE = N_EXPERTS // N_GROUPS
TOP_K = 2
NORM_EPS = 1e-6
EXPERT_ROWS = 512
EXPERT_SUB_ROWS = 256
HEADS_PER_STEP = 2
SB_GROUPS_PER_STEP = 4
DIFF_BLOCKS_PER_TRIP = 2
INDEX_STEPS = 16
ROW_TILE = 8
SB_DEAD_LOG2 = -110.0 * math.log2(math.e)
MASK_VALUE = -1e30
VMEM_LIMIT = 56 * 1024 * 1024


def _cparams(semantics, vmem=VMEM_LIMIT):
    return pltpu.CompilerParams(dimension_semantics=semantics, vmem_limit_bytes=vmem)


def _split_bf16(a):
    hi = a.astype(BF16)
    lo = (a - hi.astype(F32)).astype(BF16)
    return hi, lo


def _dot3(a, b):
    ah, al = _split_bf16(a)
    bh, bl = _split_bf16(b)
    d = functools.partial(jnp.dot, preferred_element_type=F32)
    return d(ah, bh) + d(ah, bl) + d(al, bh)


def _rms(x, g):
    ms = jnp.mean(x * x, axis=-1, keepdims=True)
    return x * lax.rsqrt(ms + NORM_EPS) * g


def _mod_kernel(c_ref, w_ref, b_ref, o_ref):
    c = c_ref[...]
    c_act = c / (1.0 + jnp.exp(-c))
    o_ref[...] = _dot3(c_act, w_ref[...]) + b_ref[...]


def _modulation(c, w_ada, b_ada):
    depth, d, six_d = w_ada.shape
    b = c.shape[0]
    rows = 16
    c_pad = jnp.zeros((rows, d), F32).at[:b].set(c)
    tn = 1536
    out = pl.pallas_call(
        _mod_kernel,
        out_shape=jax.ShapeDtypeStruct((depth, rows, six_d), F32),
        grid=(depth, six_d // tn),
        in_specs=[
            pl.BlockSpec((rows, d), lambda l, j: (0, 0)),
            pl.BlockSpec((None, d, tn), lambda l, j: (l, 0, j)),
            pl.BlockSpec((None, 1, tn), lambda l, j: (l, 0, j)),
        ],
        out_specs=pl.BlockSpec((None, rows, tn), lambda l, j: (l, 0, j)),
        compiler_params=_cparams(("arbitrary", "arbitrary")),
        name="adaln_mod",
    )(c_pad, w_ada, b_ada.reshape(depth, 1, six_d))
    return out[:, :b]


Q_SCALE = HEAD_DIM ** -0.5 * math.log2(math.e)


def _proj_kernel(*refs, d, rope):
    if rope:
        x_ref, g_ref, sc_ref, sh_ref, w_ref, pos_ref, pat_ref, o_ref = refs
    else:
        x_ref, g_ref, sc_ref, sh_ref, w_ref, o_ref = refs
    h = _rms(x_ref[...], g_ref[...]) * (1.0 + sc_ref[...]) + sh_ref[...]
    p = jnp.dot(h.astype(BF16), w_ref[...], preferred_element_type=F32)
    if rope:
        ang = pos_ref[...].astype(F32) * pat_ref[...]
        cs, sn = jnp.cos(ang), jnp.sin(ang)
        lane = lax.broadcasted_iota(I32, ang.shape, 1) & (HEAD_DIM - 1)
        half = ROPE_DIMS // 2
        s_lo = jnp.where(lane < half, -sn, 0.0)
        s_hi = jnp.where((lane >= half) & (lane < ROPE_DIMS), sn, 0.0)
    for j in range(2 * d // LANES):
        blk = p[:, j * LANES:(j + 1) * LANES]
        if rope:
            blk = (blk * cs + pltpu.roll(blk, half, 1) * s_hi
                   + pltpu.roll(blk, LANES - half, 1) * s_lo)
        if j < d // LANES:
            blk = blk * Q_SCALE
        o_ref[:, j * LANES:(j + 1) * LANES] = blk.astype(BF16)
    o_ref[:, 2 * d:] = p[:, 2 * d:].astype(BF16)


def _norm_proj(x, g, sc, sh, w_bf16, positions, rope, seq):
    n, d = x.shape
    tm = min(512, seq)
    per_b = seq // tm
    row = lambda i: (i, 0)
    bat = lambda i: (i // per_b, 0, 0)
    in_specs = [
        pl.BlockSpec((tm, d), row),
        pl.BlockSpec((1, d), lambda i: (0, 0)),
        pl.BlockSpec((None, 1, d), bat),
        pl.BlockSpec((None, 1, d), bat),
        pl.BlockSpec((d, 3 * d), lambda i: (0, 0)),
    ]
    args = [x, g.reshape(1, d), sc, sh, w_bf16]
    if rope:
        half = ROPE_DIMS // 2
        inv_freq = jnp.power(jnp.float32(ROPE_THETA), -jnp.arange(half, dtype=F32) * (2.0 / ROPE_DIMS))
        lane = jnp.arange(LANES) % HEAD_DIM
        pat = jnp.where(lane < ROPE_DIMS, inv_freq[lane % half], 0.0).astype(F32).reshape(1, LANES)
        in_specs += [pl.BlockSpec((tm, 1), row), pl.BlockSpec((1, LANES), lambda i: (0, 0))]
        args += [positions.reshape(n, 1), pat]
    return pl.pallas_call(
        functools.partial(_proj_kernel, d=d, rope=rope),
        out_shape=jax.ShapeDtypeStruct((n, 3 * d), BF16),
        grid=(n // tm,),
        in_specs=in_specs,
        out_specs=pl.BlockSpec((tm, 3 * d), row),
        compiler_params=_cparams(("arbitrary",)),
        name="norm_proj_rope" if rope else "norm_proj",
    )(*args)


def _stack_halves(q):
    lane = lax.broadcasted_iota(I32, q.shape, 1)
    zero = jnp.zeros_like(q)
    return jnp.concatenate([jnp.where(lane < HEAD_DIM, q, zero), jnp.where(lane >= HEAD_DIM, q, zero)], axis=0)


def _qk(qs, kb):
    return lax.dot_general(qs, kb, (((1,), (1,)), ((), ())), preferred_element_type=F32)


def _head_lanes(hh):
    return slice(hh * LANES, (hh + 1) * LANES)


def _lane_chunks(x):
    return [x[:, c * LANES:(c + 1) * LANES] for c in range(x.shape[1] // LANES)]


def _block_positions(shape, t, i, j):
    r = lax.broadcasted_iota(I32, shape, 0)
    qpos = i * t + jnp.where(r >= t, r - t, r)
    kpos = j * t + lax.broadcasted_iota(I32, shape, 1)
    return qpos, kpos


def _diff_attn_kernel(lam_ref, g_ref, q_ref, k_ref, v_ref, o_ref, m_ref, l_ref, acc_ref, *, t, lam_init):
    i = pl.program_id(2)
    qs = [_stack_halves(q_ref[:, _head_lanes(hh)]) for hh in range(HEADS_PER_STEP)]
    m_ref[...] = jnp.full(m_ref.shape, MASK_VALUE, F32)
    l_ref[...] = jnp.zeros(l_ref.shape, F32)
    acc_ref[...] = jnp.zeros(acc_ref.shape, F32)

    def scores(hh, j):
        start = pl.multiple_of(j * t, t)
        return _qk(qs[hh], k_ref[pl.ds(start, t), _head_lanes(hh)])

    def softmax_pv(hh, s, j, masked):
        start = pl.multiple_of(j * t, t)
        if masked:
            qpos, kpos = _block_positions(s.shape, t, i, j)
            s = jnp.where(kpos <= qpos, s, MASK_VALUE)
        chunks = _lane_chunks(s)
        m_old = m_ref[hh]
        m_new = jnp.maximum(m_old, jnp.max(functools.reduce(jnp.maximum, chunks), axis=1, keepdims=True))
        alpha = jnp.exp2(m_old - m_new)
        ps = [jnp.exp2(ch - m_new) for ch in chunks]
        l_ref[hh] = l_ref[hh] * alpha + functools.reduce(jnp.add, ps)
        p = jnp.concatenate([x.astype(BF16) for x in ps], axis=1)
        pv = jnp.dot(p, v_ref[pl.ds(start, t), _head_lanes(hh)], preferred_element_type=F32)
        acc_ref[hh] = acc_ref[hh] * alpha + pv
        m_ref[hh] = m_new

    def block_run(js, last_masked):
        work = [(hh, j, last_masked and n == len(js) - 1) for n, j in enumerate(js) for hh in range(HEADS_PER_STEP)]
        s_next = scores(work[0][0], work[0][1])
        for n, (hh, j, masked) in enumerate(work):
            s_cur = s_next
            if n + 1 < len(work):
                s_next = scores(work[n + 1][0], work[n + 1][1])
            softmax_pv(hh, s_cur, j, masked)

    run = DIFF_BLOCKS_PER_TRIP
    shift = run.bit_length() - 1

    def off_diag_run(p, carry):
        block_run([run * p + n for n in range(run)], False)
        return carry

    lax.fori_loop(0, lax.shift_right_logical(i, shift), off_diag_run, 0)

    rem = i & (run - 1)
    for r in range(run):
        @pl.when(rem == r)
        def _(r=r):
            block_run([i - r + n for n in range(r + 1)], True)

    lam = lam_ref[...]
    lam_full = (jnp.exp(jnp.sum(lam[0:1] * lam[1:2], axis=1, keepdims=True))
                - jnp.exp(jnp.sum(lam[2:3] * lam[3:4], axis=1, keepdims=True)) + lam_init)
    for hh in range(HEADS_PER_STEP):
        o_all = acc_ref[hh] / jnp.sum(l_ref[hh], axis=1, keepdims=True)
        o = o_all[:t] - lam_full * o_all[t:]
        o_ref[:, _head_lanes(hh)] = (_rms(o, g_ref[...]) * (1.0 - lam_init)).astype(BF16)


def _attn_specs(t, seq, nq, groups, per_step):
    w = per_step * LANES
    q_spec = pl.BlockSpec((t, w), lambda b, h, i: (b * nq + i, h))
    k_spec = pl.BlockSpec((seq, w), lambda b, h, i: (b, groups + h))
    v_spec = pl.BlockSpec((seq, w), lambda b, h, i: (b, 2 * groups + h))
    return q_spec, k_spec, v_spec


def _diff_attention(proj, lam, subln_g, batch, seq, d, layer_idx):
    n = batch * seq
    groups = d // (HEADS_PER_STEP * LANES)
    t = min(512, seq)
    nq = seq // t
    lam_init = 0.8 - 0.6 * math.exp(-0.3 * layer_idx)
    q_spec, k_spec, v_spec = _attn_specs(t, seq, nq, groups, HEADS_PER_STEP)
    stat = pltpu.VMEM((HEADS_PER_STEP, 2 * t, LANES), F32)
    return pl.pallas_call(
        functools.partial(_diff_attn_kernel, t=t, lam_init=lam_init),
        out_shape=jax.ShapeDtypeStruct((n, d), BF16),
        grid=(batch, groups, nq),
        in_specs=[
            pl.BlockSpec((4, HEAD_DIM), lambda b, h, i: (0, 0)),
            pl.BlockSpec((1, LANES), lambda b, h, i: (0, 0)),
            q_spec, k_spec, v_spec,
        ],
        out_specs=q_spec,
        scratch_shapes=[stat, stat, stat],
        compiler_params=_cparams(("arbitrary", "arbitrary", "arbitrary")),
        name="diff_attention",
    )(lam, subln_g.reshape(1, LANES), proj, proj, proj)


def _sb_attn_kernel(q_ref, k_ref, v_ref, tri_ref, o_ref, c_ref, acc_ref, *, t):
    i = pl.program_id(2)
    units = list(range(SB_GROUPS_PER_STEP))
    qs = [_stack_halves(q_ref[:, _head_lanes(u)]) for u in units]

    def gates(u, j, diag, guard):
        start = pl.multiple_of(j * t, t)
        z = _qk(qs[u], k_ref[pl.ds(start, t), _head_lanes(u)])
        log_beta = jnp.minimum(z, 0.0) - jnp.log2(1.0 + jnp.exp2(-jnp.abs(z)))
        log_rest = log_beta - z
        keep = None
        if diag:
            qpos, kpos = _block_positions(z.shape, t, i, j)
            keep = kpos < qpos
        if guard is not None:
            keep = lax.broadcasted_iota(I32, z.shape, 0) < guard
        if keep is not None:
            log_rest = jnp.where(keep, log_rest, 0.0)
        row_sum = jnp.sum(functools.reduce(jnp.add, _lane_chunks(log_rest)), axis=1, keepdims=True)
        return log_beta, log_rest, keep, row_sum

    def later_keys_sum(log_rest):
        return jnp.dot(log_rest.astype(BF16), tri_ref[...], preferred_element_type=F32)

    def weighted_values(u, j, log_beta, within, keep, c):
        start = pl.multiple_of(j * t, t)
        arg = log_beta + within
        if c is not None:
            arg = jnp.concatenate([ch + c for ch in _lane_chunks(arg)], axis=1)
        a = jnp.exp2(arg)
        if keep is not None:
            a = jnp.where(keep, a, 0.0)
        return jnp.dot(a.astype(BF16), v_ref[pl.ds(start, t), _head_lanes(u)], preferred_element_type=F32)

    def sweep(work, carried):
        gated, within, pv = {}, {}, {}
        for step in range(len(work) + 2):
            if step < len(work):
                gated[step] = gates(*work[step])
            if 1 <= step <= len(work):
                within[step - 1] = later_keys_sum(gated[step - 1][1])
            if step >= 2:
                n = step - 2
                u, j, _, _ = work[n]
                pv[n] = weighted_values(u, j, gated[n][0], within[n], gated[n][2], carried(n, gated))
        return gated, pv

    has_prev = jnp.where(i > 0, 2 * t, 0)
    j_prev = jnp.maximum(i - 1, 0)
    n_u = len(units)
    work = [(u, i, True, None) for u in units] + [(u, j_prev, False, has_prev) for u in units]
    gated, pv = sweep(work, lambda n, g: None if n < n_u else jnp.broadcast_to(g[n - n_u][3], (2 * t, LANES)))
    c_max = None
    for u in units:
        acc_ref[u] = pv[u] + pv[n_u + u]
        c2 = jnp.broadcast_to(gated[u][3], (2 * t, LANES)) + gated[n_u + u][3]
        c_ref[u] = c2
        c_max = jnp.max(c2) if c_max is None else jnp.maximum(c_max, jnp.max(c2))

    def cond(state):
        j, c_max = state
        return (j >= 0) & (c_max > SB_DEAD_LOG2)

    def body(state):
        j, _ = state
        cs = [c_ref[u] for u in units]
        gated, pv = sweep([(u, j, False, None) for u in units], lambda n, g: cs[n])
        c_max = None
        for u in units:
            acc_ref[u] += pv[u]
            c_new = cs[u] + gated[u][3]
            c_ref[u] = c_new
            c_max = jnp.max(c_new) if c_max is None else jnp.maximum(c_max, jnp.max(c_new))
        return j - 1, c_max

    lax.while_loop(cond, body, (i - 2, c_max))

    lane = lax.broadcasted_iota(I32, (t, LANES), 1)
    for hh in units:
        acc = acc_ref[hh]
        o_ref[:, _head_lanes(hh)] = jnp.where(lane < HEAD_DIM, acc[:t], acc[t:]).astype(BF16)


def _sb_attention(proj, batch, seq, d):
    n = batch * seq
    groups = d // (SB_GROUPS_PER_STEP * LANES)
    t = min(256, seq)
    nq = seq // t
    idx = jnp.arange(t)
    tri = (idx[:, None] > idx[None, :]).astype(BF16)
    q_spec, k_spec, v_spec = _attn_specs(t, seq, nq, groups, SB_GROUPS_PER_STEP)
    stat = pltpu.VMEM((SB_GROUPS_PER_STEP, 2 * t, LANES), F32)
    return pl.pallas_call(
        functools.partial(_sb_attn_kernel, t=t),
        out_shape=jax.ShapeDtypeStruct((n, d), BF16),
        grid=(batch, groups, nq),
        in_specs=[q_spec, k_spec, v_spec, pl.BlockSpec((t, t), lambda b, h, i: (0, 0))],
        out_specs=q_spec,
        scratch_shapes=[stat, stat],
        compiler_params=_cparams(("arbitrary", "arbitrary", "arbitrary")),
        name="sb_attention",
    )(proj, proj, proj, tri)


def _first_index_of(mask, lane, sentinel):
    return jnp.min(jnp.where(mask, lane, float(sentinel)), axis=1, keepdims=True)


def _top2(v, lane):
    neg = -jnp.inf
    m1 = jnp.max(v, axis=1, keepdims=True)
    i1 = _first_index_of(v == m1, lane, N_EXPERTS)
    v2 = jnp.where(lane == i1, neg, v)
    m2 = jnp.max(v2, axis=1, keepdims=True)
    i2 = _first_index_of(v2 == m2, lane, N_EXPERTS)
    return m1, i1, m2, i2


def _route(scores, sel):
    lane_i = lax.broadcasted_iota(I32, sel.shape, 1)
    lane = lane_i.astype(F32)
    grp = lax.shift_right_logical(lane_i, int(math.log2(GROUP_SIZE))).astype(F32)
    neg = -jnp.inf
    best = None
    g_best = None
    for g in range(N_GROUPS):
        m1, _, m2, _ = _top2(jnp.where(grp == float(g), sel, neg), lane)
        score = m1 + m2
        if g == 0:
            best, g_best = score, jnp.zeros(score.shape, F32)
        else:
            better = score > best
            g_best = jnp.where(better, float(g), g_best)
            best = jnp.where(better, score, best)
    _, e1, _, e2 = _top2(jnp.where(grp == g_best, sel, neg), lane)
    hot1, hot2 = lane == e1, lane == e2
    s1 = jnp.sum(jnp.where(hot1, scores, 0.0), axis=1, keepdims=True)
    s2 = jnp.sum(jnp.where(hot2, scores, 0.0), axis=1, keepdims=True)
    tot = s1 + s2
    return s1 / tot, s2 / tot, e1, e2, hot1, hot2


def _outproj_route_kernel(o_ref, x_ref, wo_ref, ga_ref, g_ref, sc_ref, sh_ref, wr_ref, br_ref, tril_ref,
                          xo_ref, h_ref, meta_ref, meta_t_ref, cnt_ref, *, ts, n_sub):
    @pl.when(pl.program_id(0) == 0)
    def _():
        cnt_ref[...] = jnp.zeros(cnt_ref.shape, F32)

    subs = range(n_sub)
    rows = [pl.ds(s * ts, ts) for s in subs]
    mix = [jnp.dot(o_ref[rows[s], :], wo_ref[...], preferred_element_type=F32) for s in subs]
    hs = []
    for s in subs:
        xn = x_ref[rows[s], :] + ga_ref[...] * mix[s]
        xo_ref[rows[s], :] = xn
        h = _rms(xn, g_ref[...]) * (1.0 + sc_ref[...]) + sh_ref[...]
        _store_rows(h_ref, s * ts, h)
        hs.append(h)

    logits = [_dot3(hs[s], wr_ref[...]) for s in subs]
    routed = []
    for s in subs:
        e = jnp.exp(logits[s] - jnp.max(logits[s], axis=1, keepdims=True))
        scores = e / jnp.sum(e, axis=1, keepdims=True)
        routed.append(_route(scores, scores + br_ref[...]))

    tril = tril_ref[...]
    ones = [(jnp.where(r[4], 1.0, 0.0), jnp.where(r[5], 1.0, 0.0)) for r in routed]
    prefix = [(jnp.dot(tril, o1.astype(BF16), preferred_element_type=F32),
               jnp.dot(tril, o2.astype(BF16), preferred_element_type=F32)) for o1, o2 in ones]
    base = cnt_ref[...]
    ml = lax.broadcasted_iota(I32, (ts, LANES), 1)
    for s in subs:
        g1, g2, e1, e2, hot1, hot2 = routed[s]
        tot1 = jnp.sum(ones[s][0], axis=0, keepdims=True)
        tot2 = jnp.sum(ones[s][1], axis=0, keepdims=True)
        r1 = jnp.sum(jnp.where(hot1, base + prefix[s][0], 0.0), axis=1, keepdims=True)
        r2 = jnp.sum(jnp.where(hot2, base + tot1 + prefix[s][1], 0.0), axis=1, keepdims=True)
        base = base + tot1 + tot2
        meta = jnp.where(ml == 0, g1, 0.0)
        for k, col in enumerate((g2, e1, e2, r1, r2)):
            meta = jnp.where(ml == k + 1, col, meta)
        meta_ref[rows[s], :] = meta
        meta_t_ref[:, rows[s]] = meta.T[:meta_t_ref.shape[0]]
    cnt_ref[...] = base


def _outproj_route(o, x, wo_bf16, ga, g, sc, sh, w_router, b_router, seq):
    n, d = x.shape
    tm = min(1024, seq)
    ts = min(128, tm)
    per_b = seq // tm
    row = lambda i: (i, 0)
    bat = lambda i: (i // per_b, 0, 0)
    const = lambda i: (0, 0)
    idx = jnp.arange(ts)
    tril = (idx[None, :] < idx[:, None]).astype(BF16)
    return pl.pallas_call(
        functools.partial(_outproj_route_kernel, ts=ts, n_sub=tm // ts),
        out_shape=(
            jax.ShapeDtypeStruct((n, d), F32),
            jax.ShapeDtypeStruct((n * ROW_TILE, LANES), F32),
            jax.ShapeDtypeStruct((n, LANES), F32),
            jax.ShapeDtypeStruct((8, n), F32),
            jax.ShapeDtypeStruct((1, N_EXPERTS), F32),
        ),
        grid=(n // tm,),
        in_specs=[
            pl.BlockSpec((tm, d), row),
            pl.BlockSpec((tm, d), row),
            pl.BlockSpec((d, d), const),
            pl.BlockSpec((None, 1, d), bat),
            pl.BlockSpec((1, d), const),
            pl.BlockSpec((None, 1, d), bat),
            pl.BlockSpec((None, 1, d), bat),
            pl.BlockSpec((d, N_EXPERTS), const),
            pl.BlockSpec((1, N_EXPERTS), const),
            pl.BlockSpec((ts, ts), const),
        ],
        out_specs=(
            pl.BlockSpec((tm, d), row),
            pl.BlockSpec((tm * ROW_TILE, LANES), row),
            pl.BlockSpec((tm, LANES), row),
            pl.BlockSpec((8, tm), lambda i: (0, i)),
            pl.BlockSpec((1, N_EXPERTS), const),
        ),
        compiler_params=_cparams(("arbitrary",)),
        name="outproj_route",
    )(o, x, wo_bf16, ga, g.reshape(1, d), sc, sh, w_router, b_router.reshape(1, N_EXPERTS), tril)


def _row_copy(src, src_row, dst, dst_row, sem, rows=1):
    n = rows * ROW_TILE
    return pltpu.make_async_copy(src.at[pl.ds(pl.multiple_of(src_row * ROW_TILE, ROW_TILE), n)],
                                 dst.at[pl.ds(pl.multiple_of(dst_row * ROW_TILE, ROW_TILE), n)], sem)


def _load_rows(ref, start_row, rows):
    return jnp.concatenate([ref[pl.ds(start_row * ROW_TILE + k, rows, stride=ROW_TILE), :]
                            for k in range(ROW_TILE)], axis=1)


def _store_rows(ref, start_row, value):
    for k in range(ROW_TILE):
        ref[pl.ds(start_row * ROW_TILE + k, value.shape[0], stride=ROW_TILE), :] = value[:, k * LANES:(k + 1) * LANES]


def _index_scratch(n, tc):
    steps = math.gcd(INDEX_STEPS, n // tc)
    return pltpu.SMEM((steps * 2 * tc,), I32)


def _load_row_indices(dest_hbm, dest_smem, sem, step, tc):
    n_steps = dest_smem.shape[0] // (2 * tc)
    slot = lax.rem(step, n_steps)

    @pl.when(slot == 0)
    def _():
        cp = pltpu.make_async_copy(dest_hbm.at[pl.ds(step * 2 * tc, dest_smem.shape[0])], dest_smem, sem)
        cp.start()
        cp.wait()

    return slot * 2 * tc


def _dispatch_kernel(tail_ref, dest_hbm, h_ref, rows_hbm, dest_smem, stage, zero_buf, idx_sem, row_sems, *, tc, n_blk):
    c = pl.program_id(0)
    last = pl.num_programs(0) - 1

    @pl.when(c == 0)
    def _():
        zero_buf[...] = jnp.zeros(zero_buf.shape, F32)

        def fill(row):
            return _row_copy(zero_buf, 0, rows_hbm, row, row_sems.at[0], rows=EXPERT_ROWS)

        def fill_tail(e, carry):
            cp = fill(tail_ref[e])
            cp.start()
            cp.wait()
            return carry

        def start_unused_block(b, carry):
            fill(b * EXPERT_ROWS).start()
            return carry

        def wait_unused_block(b, carry):
            fill(b * EXPERT_ROWS).wait()
            return carry

        lax.fori_loop(0, N_EXPERTS, fill_tail, 0)
        lax.fori_loop(tail_ref[N_EXPERTS], n_blk, start_unused_block, 0)
        lax.fori_loop(tail_ref[N_EXPERTS], n_blk, wait_unused_block, 0)

    base = _load_row_indices(dest_hbm, dest_smem, idx_sem, c, tc)

    def drain(slot):
        def body(tk, carry):
            _row_copy(stage.at[slot], 0, rows_hbm, 0, row_sems.at[slot]).wait()
            _row_copy(stage.at[slot], 0, rows_hbm, 0, row_sems.at[slot]).wait()
            return carry

        lax.fori_loop(0, tc, body, 0, unroll=8)

    for slot in range(2):
        @pl.when((c & 1) == slot)
        def _(slot=slot):
            stage[slot] = h_ref[...]

            def issue(tk, carry):
                _row_copy(stage.at[slot], tk, rows_hbm, dest_smem[base + tk], row_sems.at[slot]).start(priority=0)
                _row_copy(stage.at[slot], tk, rows_hbm, dest_smem[base + tc + tk], row_sems.at[slot]).start(priority=1)
                return carry

            lax.fori_loop(0, tc, issue, 0, unroll=8)

            @pl.when(c > 0)
            def _():
                drain(1 - slot)

            @pl.when(c == last)
            def _():
                drain(slot)


def _dispatch(tail, dest_flat, h_tiles, n_rows, tc):
    n = h_tiles.shape[0] // ROW_TILE
    any_spec = pl.BlockSpec(memory_space=pl.ANY)
    grid_spec = pltpu.PrefetchScalarGridSpec(
        num_scalar_prefetch=1,
        grid=(n // tc,),
        in_specs=[any_spec, pl.BlockSpec((tc * ROW_TILE, LANES), lambda i, tail: (i, 0))],
        out_specs=any_spec,
        scratch_shapes=[_index_scratch(n, tc), pltpu.VMEM((2, tc * ROW_TILE, LANES), F32),
                        pltpu.VMEM((EXPERT_ROWS * ROW_TILE, LANES), F32),
                        pltpu.SemaphoreType.DMA, pltpu.SemaphoreType.DMA((2,))],
    )
    return pl.pallas_call(
        functools.partial(_dispatch_kernel, tc=tc, n_blk=n_rows // EXPERT_ROWS),
        out_shape=jax.ShapeDtypeStruct((n_rows * ROW_TILE, LANES), F32),
        grid_spec=grid_spec,
        compiler_params=_cparams(("arbitrary",)),
        name="moe_dispatch",
    )(tail, dest_flat, h_tiles)


def _experts_kernel(blk_e_ref, n_used_ref, x_ref, wg_ref, wu_ref, wd_ref, y_ref, wg_s, wu_s, wd_s):
    i = pl.program_id(0)
    used = i < n_used_ref[0]
    prev = blk_e_ref[jnp.maximum(i - 1, 0)]
    new_expert = (i == 0) | (blk_e_ref[i] != prev)

    @pl.when(used & new_expert)
    def _():
        wg_s[...] = wg_ref[...].astype(BF16)
        wu_s[...] = wu_ref[...].astype(BF16)
        wd_s[...] = wd_ref[...].astype(BF16)

    @pl.when(used)
    def _():
        groups = range(EXPERT_ROWS // EXPERT_SUB_ROWS)
        dot = functools.partial(jnp.dot, preferred_element_type=F32)
        xb = [_load_rows(x_ref, s * EXPERT_SUB_ROWS, EXPERT_SUB_ROWS).astype(BF16) for s in groups]
        gate = [dot(xb[s], wg_s[...]) for s in groups]
        up = [dot(xb[s], wu_s[...]) for s in groups]
        hid = [(gate[s] / (1.0 + jnp.exp(-gate[s])) * up[s]).astype(BF16) for s in groups]
        for s in groups:
            _store_rows(y_ref, s * EXPERT_SUB_ROWS, dot(hid[s], wd_s[...]))

    @pl.when(jnp.logical_not(used))
    def _():
        y_ref[...] = jnp.zeros(y_ref.shape, F32)


def _experts(blk_e, n_used, x_tiles, w_gate, w_up, w_down, layer):
    d, f = w_gate.shape[-2:]
    n_blk = x_tiles.shape[0] // (EXPERT_ROWS * ROW_TILE)
    blk = (EXPERT_ROWS * ROW_TILE, LANES)
    last_used = lambda n_used_ref: jnp.maximum(n_used_ref[0] - 1, 0)
    x_map = lambda i, be, nu: (jnp.minimum(i, last_used(nu)), 0)
    w_map = lambda i, be, nu: (layer, be[jnp.minimum(i, last_used(nu))], 0, 0)
    grid_spec = pltpu.PrefetchScalarGridSpec(
        num_scalar_prefetch=2,
        grid=(n_blk,),
        in_specs=[
            pl.BlockSpec(blk, x_map),
            pl.BlockSpec((None, None, d, f), w_map),
            pl.BlockSpec((None, None, d, f), w_map),
            pl.BlockSpec((None, None, f, d), w_map),
        ],
        out_specs=pl.BlockSpec(blk, lambda i, be, nu: (i, 0)),
        scratch_shapes=[pltpu.VMEM((d, f), BF16), pltpu.VMEM((d, f), BF16), pltpu.VMEM((f, d), BF16)],
    )
    return pl.pallas_call(
        _experts_kernel,
        out_shape=jax.ShapeDtypeStruct(x_tiles.shape, F32),
        grid_spec=grid_spec,
        compiler_params=_cparams(("arbitrary",)),
        name="moe_experts",
    )(blk_e, n_used, x_tiles, w_gate, w_up, w_down)


def _combine_kernel(*refs, tc, final):
    if final:
        dest_hbm, y_hbm, x_ref, meta_ref, gm_ref, fg_ref, o_ref = refs[:7]
    else:
        dest_hbm, y_hbm, x_ref, meta_ref, gm_ref, o_ref = refs[:6]
    dest_smem, a0, a1, b0, b1, idx_sem, row_sems = refs[-7:]
    bufs = ((a0, a1), (b0, b1))
    c = pl.program_id(0)
    last = pl.num_programs(0) - 1

    def fetch(step, slot):
        base = _load_row_indices(dest_hbm, dest_smem, idx_sem, step, tc)

        def issue(tk, carry):
            _row_copy(y_hbm, dest_smem[base + tk], bufs[slot][0], tk, row_sems.at[slot]).start(priority=0)
            _row_copy(y_hbm, dest_smem[base + tc + tk], bufs[slot][1], tk, row_sems.at[slot]).start(priority=1)
            return carry

        lax.fori_loop(0, tc, issue, 0, unroll=8)

    def drain(slot):
        def body(tk, carry):
            _row_copy(y_hbm, 0, bufs[slot][0], 0, row_sems.at[slot]).wait()
            _row_copy(y_hbm, 0, bufs[slot][1], 0, row_sems.at[slot]).wait()
            return carry

        lax.fori_loop(0, tc, body, 0, unroll=8)

    @pl.when(c == 0)
    def _():
        fetch(0, 0)

    for slot in range(2):
        @pl.when((c & 1) == slot)
        def _(slot=slot):
            @pl.when(c < last)
            def _():
                fetch(c + 1, 1 - slot)

            drain(slot)
            meta = meta_ref[...]
            y = meta[:, 0:1] * _load_rows(bufs[slot][0], 0, tc) + meta[:, 1:2] * _load_rows(bufs[slot][1], 0, tc)
            xn = x_ref[...] + gm_ref[...] * y
            if final:
                xn = _rms(xn, fg_ref[...])
            o_ref[...] = xn


def _combine(dest_flat, y_rows, x, meta, gm, final_g, seq, tc):
    n, d = x.shape
    per_b = seq // tc
    row = lambda i: (i, 0)
    any_spec = pl.BlockSpec(memory_space=pl.ANY)
    final = final_g is not None
    in_specs = [any_spec, any_spec, pl.BlockSpec((tc, d), row), pl.BlockSpec((tc, LANES), row),
                pl.BlockSpec((None, 1, d), lambda i: (i // per_b, 0, 0))]
    args = [dest_flat, y_rows, x, meta, gm]
    if final:
        in_specs.append(pl.BlockSpec((1, d), lambda i: (0, 0)))
        args.append(final_g.reshape(1, d))
    return pl.pallas_call(
        functools.partial(_combine_kernel, tc=tc, final=final),
        out_shape=jax.ShapeDtypeStruct((n, d), F32),
        grid=(n // tc,),
        in_specs=in_specs,
        out_specs=pl.BlockSpec((tc, d), row),
        scratch_shapes=[_index_scratch(n, tc)]
        + [pltpu.VMEM((tc * ROW_TILE, LANES), F32)] * 4
        + [pltpu.SemaphoreType.DMA, pltpu.SemaphoreType.DMA((2,))],
        compiler_params=_cparams(("arbitrary",)),
        name="moe_combine_final" if final else "moe_combine",
    )(*args)


def _moe(h_tiles, meta, meta_t, counts, x, gm, w_gate, w_up, w_down, layer, final_g, seq):
    n, d = x.shape
    assert d == ROW_TILE * LANES
    tc = min(512, seq)
    n_rows = n * TOP_K + N_EXPERTS * EXPERT_ROWS
    n_blk = n_rows // EXPERT_ROWS
    cnt = counts[0].astype(I32)
    padded = ((cnt + EXPERT_ROWS - 1) // EXPERT_ROWS) * EXPERT_ROWS
    seg_end = jnp.cumsum(padded)
    seg_start = seg_end - padded
    n_used = (seg_end[-1:] // EXPERT_ROWS).astype(I32)
    blk_start = jnp.arange(n_blk, dtype=I32) * EXPERT_ROWS
    blk_e = jnp.minimum(jnp.sum(seg_end[None, :] <= blk_start[:, None], axis=1), N_EXPERTS - 1).astype(I32)
    expert = meta_t[2:4].astype(I32)
    rank = meta_t[4:6].astype(I32)
    hot = expert[:, :, None] == jnp.arange(N_EXPERTS, dtype=I32)
    dest = rank + jnp.sum(jnp.where(hot, seg_start, 0), axis=-1)
    dest_flat = dest.reshape(TOP_K, n // tc, tc).transpose(1, 0, 2).reshape(-1)

    tail = jnp.concatenate([jnp.minimum(seg_start + cnt, n_rows - EXPERT_ROWS), n_used]).astype(I32)

    x_rows = _dispatch(tail, dest_flat, h_tiles, n_rows, tc)
    y_rows = _experts(blk_e, n_used, x_rows, w_gate, w_up, w_down, layer)
    return _combine(dest_flat, y_rows, x, meta, gm, final_g, seq, tc)


def kernel(x, c, positions, w_ada, b_ada, norm_g, w_in, w_out, diff_lambda, diff_subln_g, w_router, b_router,
           w_gate, w_up, w_down, final_norm_g):
    batch, seq, d = x.shape
    depth = w_ada.shape[0]
    n = batch * seq
    mod = _modulation(c, w_ada, b_ada)
    xf = x.reshape(n, d)
    for i in range(depth):
        sh_a, sc_a, g_a, sh_m, sc_m, g_m = [m.reshape(batch, 1, d) for m in jnp.split(mod[i], 6, axis=-1)]
        diff = i % 2 == 0
        proj = _norm_proj(xf, norm_g[i, 0], sc_a, sh_a, w_in[i].astype(BF16), positions, diff, seq)
        if diff:
            o = _diff_attention(proj, diff_lambda[i // 2], diff_subln_g[i // 2], batch, seq, d, i)
        else:
            o = _sb_attention(proj, batch, seq, d)
        xf, h, meta, meta_t, counts = _outproj_route(o, xf, w_out[i].astype(BF16), g_a, norm_g[i, 1], sc_m, sh_m,
                                                     w_router, b_router, seq)
        final_g = final_norm_g if i == depth - 1 else None
        xf = _moe(h, meta, meta_t, counts, xf, g_m, w_gate, w_up, w_down, i, final_g, seq)
    return xf.reshape(batch, seq, d)
```
